```python
import jax, jax.numpy as jnp
from jax import lax
import numpy as np

D_MODEL = 4096
BATCH = 1
SEQ = 8192
DEPTH = 1

N_META = 16
NORM_EPS = 1e-6
DN_HEADS = 16
DN_DK = 128
DN_DV = 128
DN_CONV = 3
DN_CHUNK = 64
SWA_HQ = 16
SWA_HKV = 4
SWA_D = 128
SWA_WINDOW = 128
SWA_BLOCK = 128
N_EXPERTS = 16
EXPERT_FF = 2048
EC_CAPACITY = 2

DN_QK = DN_HEADS * DN_DK
DN_VW = DN_HEADS * DN_DV
DN_CONV_W = 2 * DN_QK + DN_VW
SWA_QW = SWA_HQ * SWA_D
SWA_KVW = SWA_HKV * SWA_D
IN_SPLITS = (DN_CONV_W, DN_VW, DN_HEADS, DN_HEADS, DN_HEADS, DN_HEADS, SWA_QW, SWA_KVW, SWA_KVW, D_MODEL, D_MODEL)
IN_WIDTH = sum(IN_SPLITS)

kernel_name = 'hybrid_deltanet_swa_ec_moe_encoder'


def _rmsnorm(x, w):
    xf = x.astype(jnp.float32)
    y = xf * lax.rsqrt(jnp.mean(xf * xf, axis=-1, keepdims=True) + NORM_EPS)
    return (y * w.astype(jnp.float32)).astype(x.dtype)


def _l2norm(x):
    xf = x.astype(jnp.float32)
    return xf * lax.rsqrt(jnp.sum(xf * xf, axis=-1, keepdims=True) + NORM_EPS)


def _centred_dwconv(x, w):
    K = w.shape[0]
    L = x.shape[1]
    p = K // 2
    xp = jnp.pad(x, ((0, 0), (p, K - 1 - p), (0, 0)))
    y = xp[:, :L] * w[0]
    for i in range(1, K):
        y = y + xp[:, i:i + L] * w[i]
    return y


def _chunk(x, pad_front):
    B, L = x.shape[:2]
    pad = (-L) % DN_CHUNK
    widths = [(0, 0)] * x.ndim
    widths[1] = (pad, 0) if pad_front else (0, pad)
    xp = jnp.pad(x, widths)
    n = xp.shape[1] // DN_CHUNK
    xp = xp.reshape((B, n, DN_CHUNK) + x.shape[2:])
    return jnp.moveaxis(xp, 3, 1)


def _unchunk(o, L, pad_front):
    B, H, N, C, E = o.shape
    o = jnp.moveaxis(o, 1, 3).reshape(B, N * C, H, E)
    pad = N * C - L
    return o[:, pad:] if pad_front else o[:, :L]


def _gated_delta_rule(q, k, v, beta, g):
    C = q.shape[-2]
    incl = jnp.tril(jnp.ones((C, C), dtype=bool))
    strict = jnp.tril(jnp.ones((C, C), dtype=bool), -1)
    gc = jnp.cumsum(g, axis=-1)
    decay = jnp.where(incl, jnp.exp(jnp.where(incl, gc[..., :, None] - gc[..., None, :], 0.0)), 0.0)
    kb = k * beta[..., None]
    m = jnp.where(strict, jnp.einsum('bhnid,bhnjd->bhnij', kb, k) * decay, 0.0)
    eye = jnp.eye(C, dtype=q.dtype)
    t = lax.linalg.triangular_solve(eye + m, jnp.broadcast_to(eye, m.shape), left_side=True, lower=True, unit_diagonal=True)
    u = jnp.einsum('bhnij,bhnje->bhnie', t, v * beta[..., None])
    w = jnp.einsum('bhnij,bhnjd->bhnid', t, kb * jnp.exp(gc)[..., None])
    qk = jnp.einsum('bhnid,bhnjd->bhnij', q, k) * decay
    q_dec = q * jnp.exp(gc)[..., None]
    g_last = gc[..., -1]
    k_dec = k * jnp.exp(g_last[..., None] - gc)[..., None]

    def step(s, xs):
        q_c, k_c, u_c, w_c, qk_c, gl = xs
        v_new = u_c - jnp.einsum('bhcd,bhde->bhce', w_c, s)
        o = jnp.einsum('bhcd,bhde->bhce', q_c, s) + jnp.einsum('bhij,bhje->bhie', qk_c, v_new)
        s = s * jnp.exp(gl)[..., None, None] + jnp.einsum('bhcd,bhce->bhde', k_c, v_new)
        return s, o

    B, H = q.shape[:2]
    s0 = jnp.zeros((B, H, q.shape[-1], v.shape[-1]), jnp.float32)
    xs = tuple(jnp.moveaxis(a, 2, 0) for a in (q_dec, k_dec, u, w, qk, g_last))
    _, o = lax.scan(step, s0, xs)
    return jnp.moveaxis(o, 0, 2)


def _deltanet_mixer(qkv, z, b_f, b_b, a_f, a_b, a_log_fwd, a_log_bwd, dt_bias_fwd, dt_bias_bwd, out_norm_w):
    B, L = qkv.shape[:2]
    q, k, v = jnp.split(qkv, [DN_QK, 2 * DN_QK], axis=-1)
    q = _l2norm(q.reshape(B, L, DN_HEADS, DN_DK)) * (DN_DK ** -0.5)
    k = _l2norm(k.reshape(B, L, DN_HEADS, DN_DK))
    v = v.reshape(B, L, DN_HEADS, DN_DV).astype(jnp.float32)

    def log_decay(a, a_log, dtb):
        return -jnp.exp(a_log.astype(jnp.float32)) * jax.nn.softplus(a.astype(jnp.float32) + dtb.astype(jnp.float32))

    g_f = log_decay(a_f, a_log_fwd, dt_bias_fwd)
    g_b = log_decay(a_b, a_log_bwd, dt_bias_bwd)
    beta_f = jax.nn.sigmoid(b_f.astype(jnp.float32))
    beta_b = jax.nn.sigmoid(b_b.astype(jnp.float32))

    o_f = _gated_delta_rule(*(_chunk(t, True) for t in (q, k, v, beta_f, g_f)))
    o_f = _unchunk(o_f, L, True)
    o_b = _gated_delta_rule(*(_chunk(jnp.flip(t, 1), False) for t in (q, k, v, beta_b, g_b)))
    o_b = jnp.flip(_unchunk(o_b, L, False), 1)
    o = o_f + o_b

    o = o * lax.rsqrt(jnp.mean(o * o, axis=-1, keepdims=True) + NORM_EPS) * out_norm_w.astype(jnp.float32)
    o = o * jax.nn.silu(z.reshape(B, L, DN_HEADS, DN_DV).astype(jnp.float32))
    return o.reshape(B, L, DN_VW).astype(z.dtype)


def _alibi_slopes(n):
    return 2.0 ** (-8.0 * jnp.arange(1, n + 1, dtype=jnp.float32) / n)


def _sink_probs(s, sink):
    m = jnp.maximum(jnp.max(s, axis=-1), sink)
    p = jnp.exp(s - m[..., None])
    return p / (jnp.sum(p, axis=-1, keepdims=True) + jnp.exp(sink - m)[..., None])


def _window_attention(q, k, v, attn_sink):
    B, L = q.shape[:2]
    S = L - N_META
    W = SWA_BLOCK
    NB = S // W
    G = SWA_HQ // SWA_HKV
    scale = SWA_D ** -0.5
    slopes = _alibi_slopes(SWA_HQ).reshape(SWA_HKV, G)
    sink = attn_sink.astype(jnp.float32).reshape(SWA_HKV, G)
    km, kr = k[:, :N_META], k[:, N_META:]
    vm, vr = v[:, :N_META], v[:, N_META:]

    def band(t):
        t = t.reshape(B, NB, W, SWA_HKV, SWA_D)
        t = jnp.pad(t, ((0, 0), (1, 1), (0, 0), (0, 0), (0, 0)))
        return jnp.concatenate([t[:, :-2], t[:, 1:-1], t[:, 2:]], axis=2)

    def with_meta(tm, tb):
        return jnp.concatenate([jnp.broadcast_to(tm[:, None], (B, NB, N_META, SWA_HKV, SWA_D)), tb], axis=2)

    kb = with_meta(km, band(kr))
    vb = with_meta(vm, band(vr))
    qr = q[:, N_META:].reshape(B, NB, W, SWA_HKV, G, SWA_D)
    s = jnp.einsum('bnqhgd,bnkhd->bhgnqk', qr, kb, preferred_element_type=jnp.float32) * scale

    dist = W + jnp.arange(W)[:, None] - jnp.arange(3 * W)[None, :]
    blk = jnp.arange(NB)[:, None]
    kblk = jnp.arange(3 * W)[None, :] // W
    blk_ok = ((kblk != 0) | (blk > 0)) & ((kblk != 2) | (blk < NB - 1))
    ok_band = (jnp.abs(dist) <= SWA_WINDOW)[None] & blk_ok[:, None, :]
    ok = jnp.concatenate([jnp.ones((NB, W, N_META), dtype=bool), ok_band], axis=-1)
    bias = jnp.concatenate([jnp.zeros((W, N_META), jnp.float32), -jnp.abs(dist).astype(jnp.float32)], axis=-1)
    s = jnp.where(ok, s + slopes[:, :, None, None, None] * bias, -jnp.inf)
    p = _sink_probs(s, sink[:, :, None, None])
    o_r = jnp.einsum('bhgnqk,bnkhd->bnqhgd', p.astype(v.dtype), vb).reshape(B, S, SWA_HQ, SWA_D)

    k_mq = jnp.concatenate([km, kr[:, :W]], axis=1)
    v_mq = jnp.concatenate([vm, vr[:, :W]], axis=1)
    qm = q[:, :N_META].reshape(B, N_META, SWA_HKV, G, SWA_D)
    s_m = jnp.einsum('bqhgd,bkhd->bhgqk', qm, k_mq, preferred_element_type=jnp.float32) * scale
    key_pos = jnp.arange(N_META + W)[None, :]
    q_pos = jnp.arange(N_META)[:, None]
    ok_m = (key_pos < N_META) | (key_pos - q_pos <= SWA_WINDOW)
    s_m = jnp.where(ok_m, s_m, -jnp.inf)
    p_m = _sink_probs(s_m, sink[:, :, None])
    o_m = jnp.einsum('bhgqk,bkhd->bqhgd', p_m.astype(v.dtype), v_mq).reshape(B, N_META, SWA_HQ, SWA_D)
    return jnp.concatenate([o_m, o_r], axis=1)


def _expert_choice_moe(h, w_router, w_gate, w_up, w_down):
    B, T, D = h.shape
    C = EC_CAPACITY * T // N_EXPERTS
    aff = jax.nn.softmax(jnp.einsum('btd,de->bte', h, w_router, preferred_element_type=jnp.float32), axis=-1)
    gates, idx = lax.top_k(jnp.swapaxes(aff, 1, 2), C)
    xe = jax.vmap(lambda hb, ib: hb[ib])(h, idx)
    hid = jax.nn.silu(jnp.einsum('becd,edf->becf', xe, w_gate)) * jnp.einsum('becd,edf->becf', xe, w_up)
    ye = jnp.einsum('becf,efd->becd', hid, w_down) * gates.astype(h.dtype)[..., None]
    return jax.vmap(lambda ib, yb: jnp.zeros((T, D), yb.dtype).at[ib.reshape(-1)].add(yb.reshape(-1, D)))(idx, ye)


def _layer(h, norm1_w, w_in, conv_w, a_log_fwd, a_log_bwd, dt_bias_fwd, dt_bias_bwd, out_norm_w,
           w_branch_a, attn_sink, w_branch_b, w_out, norm2_w, w_router, w_gate, w_up, w_down):
    B, L, _ = h.shape
    n = _rmsnorm(h, norm1_w)
    proj = n @ w_in
    offsets = [int(o) for o in np.cumsum(IN_SPLITS)[:-1]]
    qkv_a, z_a, b_f, b_b, a_f, a_b, q_b, k_b, v_b, gate_a, gate_b = jnp.split(proj, offsets, axis=-1)

    qkv_a = jax.nn.silu(_centred_dwconv(qkv_a, conv_w))
    o_a = _deltanet_mixer(qkv_a, z_a, b_f, b_b, a_f, a_b, a_log_fwd, a_log_bwd, dt_bias_fwd, dt_bias_bwd, out_norm_w)
    o_b = _window_attention(q_b.reshape(B, L, SWA_HQ, SWA_D), k_b.reshape(B, L, SWA_HKV, SWA_D),
                            v_b.reshape(B, L, SWA_HKV, SWA_D), attn_sink).reshape(B, L, SWA_QW)

    mixed = jax.nn.sigmoid(gate_a) * (o_a @ w_branch_a) + jax.nn.sigmoid(gate_b) * (o_b @ w_branch_b)
    h = h + mixed @ w_out
    h = h + _expert_choice_moe(_rmsnorm(h, norm2_w), w_router, w_gate, w_up, w_down)
    return h


def setup_inputs(seed: int = 0) -> dict:
    key = jax.random.key(seed)
    ks = jax.random.split(key, 24)
    f32 = jnp.float32
    nrm = lambda k, shape, s: jax.random.normal(k, shape, f32) * s
    dt = jnp.exp(jax.random.uniform(ks[6], (DEPTH, DN_HEADS), f32) * (jnp.log(0.1) - jnp.log(0.001)) + jnp.log(0.001))
    dt2 = jnp.exp(jax.random.uniform(ks[7], (DEPTH, DN_HEADS), f32) * (jnp.log(0.1) - jnp.log(0.001)) + jnp.log(0.001))
    return {
        'x': nrm(ks[0], (BATCH, SEQ, D_MODEL), 1.0),
        'meta_tokens': nrm(ks[1], (N_META, D_MODEL), 1.0),
        'norm1_w': 1.0 + nrm(ks[2], (DEPTH, D_MODEL), 0.02),
        'w_in': nrm(ks[3], (DEPTH, D_MODEL, IN_WIDTH), D_MODEL ** -0.5),
        'conv_w': nrm(ks[4], (DEPTH, DN_CONV, DN_CONV_W), DN_CONV ** -0.5),
        'a_log_fwd': jnp.log(jax.random.uniform(ks[5], (DEPTH, DN_HEADS), f32, 1.0, 16.0)),
        'a_log_bwd': jnp.log(jax.random.uniform(ks[8], (DEPTH, DN_HEADS), f32, 1.0, 16.0)),
        'dt_bias_fwd': dt + jnp.log(-jnp.expm1(-dt)),
        'dt_bias_bwd': dt2 + jnp.log(-jnp.expm1(-dt2)),
        'out_norm_w': 1.0 + nrm(ks[9], (DEPTH, DN_DV), 0.02),
        'w_branch_a': nrm(ks[10], (DEPTH, DN_VW, D_MODEL), DN_VW ** -0.5),
        'attn_sink': nrm(ks[11], (DEPTH, SWA_HQ), 0.5),
        'w_branch_b': nrm(ks[12], (DEPTH, SWA_QW, D_MODEL), SWA_QW ** -0.5),
        'w_out': nrm(ks[13], (DEPTH, D_MODEL, D_MODEL), D_MODEL ** -0.5),
        'norm2_w': 1.0 + nrm(ks[14], (DEPTH, D_MODEL), 0.02),
        'w_router': nrm(ks[15], (DEPTH, D_MODEL, N_EXPERTS), D_MODEL ** -0.5),
        'w_gate': nrm(ks[16], (DEPTH, N_EXPERTS, D_MODEL, EXPERT_FF), D_MODEL ** -0.5),
        'w_up': nrm(ks[17], (DEPTH, N_EXPERTS, D_MODEL, EXPERT_FF), D_MODEL ** -0.5),
        'w_down': nrm(ks[18], (DEPTH, N_EXPERTS, EXPERT_FF, D_MODEL), EXPERT_FF ** -0.5),
        'norm_f_w': 1.0 + nrm(ks[19], (D_MODEL,), 0.02),
    }


def reference(x, meta_tokens, norm1_w, w_in, conv_w, a_log_fwd, a_log_bwd, dt_bias_fwd, dt_bias_bwd,
              out_norm_w, w_branch_a, attn_sink, w_branch_b, w_out, norm2_w, w_router, w_gate, w_up,
              w_down, norm_f_w):
    B = x.shape[0]
    meta = jnp.broadcast_to(meta_tokens.astype(x.dtype)[None], (B, N_META, x.shape[-1]))
    h = jnp.concatenate([meta, x], axis=1)
    for i in range(DEPTH):
        h = _layer(h, norm1_w[i], w_in[i], conv_w[i], a_log_fwd[i], a_log_bwd[i], dt_bias_fwd[i],
                   dt_bias_bwd[i], out_norm_w[i], w_branch_a[i], attn_sink[i], w_branch_b[i], w_out[i],
                   norm2_w[i], w_router[i], w_gate[i], w_up[i], w_down[i])
    return _rmsnorm(h, norm_f_w)[:, N_META:]
```

```python
import functools

import jax
import jax.numpy as jnp
from jax import lax
from jax.experimental import pallas as pl
from jax.experimental.pallas import tpu as pltpu

F32 = jnp.float32
BF16 = jnp.bfloat16

N_META = 16
NORM_EPS = 1e-6
DN_HEADS = 16
DN_DK = 128
DN_DV = 128
SWA_HQ = 16
SWA_HKV = 4
SWA_D = 128
SWA_WINDOW = 128
N_EXPERTS = 16
EC_CAPACITY = 2

LANES = 128
TOKEN_TILE = 256
ATTN_TILE = 128
VMEM_LIMIT = 56 * 1024 * 1024

DN_QK = DN_HEADS * DN_DK
DN_VW = DN_HEADS * DN_DV
SWA_QW = SWA_HQ * SWA_D
SWA_KVW = SWA_HKV * SWA_D


def _params(n_grid):
    return pltpu.CompilerParams(dimension_semantics=("arbitrary",) * n_grid, vmem_limit_bytes=VMEM_LIMIT)


def _pick(n, target, mult):
    best = None
    for t in range(mult, min(n, target) + 1, mult):
        if n % t == 0:
            best = t
    assert best is not None, (n, target, mult)
    return best


def _sigmoid(x):
    return 1.0 / (1.0 + jnp.exp(-x))


def _dot(a, b):
    return jnp.dot(a, b, preferred_element_type=F32)


def _dot_nt(a, b):
    return lax.dot_general(a, b, (((1,), (1,)), ((), ())), preferred_element_type=F32)


def _rms_kernel(x_ref, w_ref, o_ref):
    x = x_ref[...]
    ms = jnp.mean(x * x, axis=-1, keepdims=True)
    o_ref[...] = (x * lax.rsqrt(ms + NORM_EPS) * w_ref[...]).astype(o_ref.dtype)


def _rmsnorm(x, w, out_dtype):
    m, d = x.shape
    tr = _pick(m, 256, 16)
    return pl.pallas_call(
        _rms_kernel,
        grid=(m // tr,),
        in_specs=[pl.BlockSpec((tr, d), lambda i: (i, 0)), pl.BlockSpec((1, d), lambda i: (0, 0))],
        out_specs=pl.BlockSpec((tr, d), lambda i: (i, 0)),
        out_shape=jax.ShapeDtypeStruct((m, d), out_dtype),
        compiler_params=_params(1),
        name="rmsnorm",
    )(x, w.reshape(1, d).astype(F32))


def _mm_kernel(*refs, has_res):
    if has_res:
        a_ref, w_ref, r_ref, o_ref, wb_ref = refs
    else:
        a_ref, w_ref, o_ref, wb_ref = refs

    @pl.when(pl.program_id(1) == 0)
    def _():
        wb_ref[...] = w_ref[...].astype(BF16)

    acc = _dot(a_ref[...], wb_ref[...])
    if has_res:
        acc = acc + r_ref[...]
    o_ref[...] = acc.astype(o_ref.dtype)


def _matmul(a, w, n_cols, col_off, out_dtype, res=None, name="matmul"):
    m, k = a.shape
    tn = _pick(n_cols, 512, LANES)
    while col_off % tn:
        tn -= LANES
    assert n_cols % tn == 0
    tm = _pick(m, 1408 if res is None else 768, 16)
    cb = col_off // tn
    in_specs = [pl.BlockSpec((tm, k), lambda n, i: (i, 0)), pl.BlockSpec((k, tn), lambda n, i: (0, cb + n))]
    args = [a, w]
    if res is not None:
        in_specs.append(pl.BlockSpec((tm, tn), lambda n, i: (i, n)))
        args.append(res)
    return pl.pallas_call(
        functools.partial(_mm_kernel, has_res=res is not None),
        grid=(n_cols // tn, m // tm),
        in_specs=in_specs,
        out_specs=pl.BlockSpec((tm, tn), lambda n, i: (i, n)),
        out_shape=jax.ShapeDtypeStruct((m, n_cols), out_dtype),
        scratch_shapes=[pltpu.VMEM((k, tn), BF16)],
        compiler_params=_params(2),
        name=name,
    )(*args)


def _dn_kernel(q_ref, qp_ref, qn_ref, k_ref, kp_ref, kn_ref, v_ref, vp_ref, vn_ref, cq_ref, ck_ref, cv_ref,
               sm_ref, prm_ref, o_ref, s_ref, *, seq_len, tl):
    d = pl.program_id(0)
    h = pl.program_id(1)
    i = pl.program_id(2)
    nt = pl.num_programs(2)
    ti = i + d * (nt - 1 - 2 * i)

    @pl.when(i == 0)
    def _():
        s_ref[...] = jnp.zeros_like(s_ref)

    row = lax.broadcasted_iota(jnp.int32, (tl, 1), 0)
    validf = ((ti * tl + row) < seq_len).astype(F32)
    has_prev = (ti > 0).astype(F32)
    has_next = (ti < nt - 1).astype(F32)

    def conv_silu(x_ref, p_ref, n_ref, c_ref):
        x = x_ref[...].astype(F32)
        prev = p_ref[...].astype(F32)[15:16, :] * has_prev
        nxt = n_ref[...].astype(F32)[0:1, :] * has_next
        xm1 = jnp.where(row == 0, prev, pltpu.roll(x, 1, 0))
        xp1 = jnp.where(row == tl - 1, nxt, pltpu.roll(x, tl - 1, 0))
        c = c_ref[...]
        y = xm1 * c[0:1] + x * c[1:2] + xp1 * c[2:3]
        return y * _sigmoid(y)

    q = conv_silu(q_ref, qp_ref, qn_ref, cq_ref)
    q = q * lax.rsqrt(jnp.sum(q * q, axis=-1, keepdims=True) + NORM_EPS) * (DN_DK ** -0.5) * validf
    k = conv_silu(k_ref, kp_ref, kn_ref, ck_ref)
    k = k * lax.rsqrt(jnp.sum(k * k, axis=-1, keepdims=True) + NORM_EPS) * validf
    v = conv_silu(v_ref, vp_ref, vn_ref, cv_ref) * validf

    lane = lax.broadcasted_iota(jnp.int32, (1, LANES), 1)
    sm = sm_ref[...]
    prm = prm_ref[...]
    z = sm + prm[1:2]
    softplus = jnp.maximum(z, 0.0) + jnp.log1p(jnp.exp(-jnp.abs(z)))
    g_all = -jnp.exp(prm[0:1]) * softplus
    col = d * DN_HEADS + h
    beta = jnp.sum(jnp.where(lane == col, _sigmoid(sm), 0.0), axis=-1, keepdims=True) * validf
    g = jnp.sum(jnp.where(lane == 2 * DN_HEADS + col, g_all, 0.0), axis=-1, keepdims=True) * validf

    r = lax.broadcasted_iota(jnp.int32, (tl, tl), 0)
    c = lax.broadcasted_iota(jnp.int32, (tl, tl), 1)
    diff = (r - c) * (1 - 2 * d)
    incl = diff >= 0
    strict = diff > 0

    tri = jnp.where(incl, 1.0, 0.0).astype(BF16)
    gb = jnp.broadcast_to(g, (tl, LANES))
    g1 = gb.astype(BF16)
    r1 = gb - g1.astype(F32)
    g2 = r1.astype(BF16)
    g3 = (r1 - g2.astype(F32)).astype(BF16)
    gc = _dot(tri, g1) + _dot(tri, g2) + _dot(tri, g3)
    gtot = jnp.sum(g, axis=0, keepdims=True)

    gi = gc[:, 0:1]
    gj = jnp.transpose(gc)[0:1, :]
    decay = jnp.where(incl, jnp.exp(jnp.where(incl, gi - gj, 0.0)), 0.0)

    kb = k * beta
    k16 = k.astype(BF16)
    m_mat = jnp.where(strict, _dot_nt(kb.astype(BF16), k16) * decay, 0.0)
    qk = _dot_nt(q.astype(BF16), k16) * decay

    def level_mask(b):
        sh = b.bit_length() - 1
        return ((r >> (sh + 1)) == (c >> (sh + 1))) & ((r >> sh) != (c >> sh))

    e = -jnp.where(level_mask(1), m_mat, 0.0)
    b = 2
    while b < tl:
        lb = jnp.where(level_mask(b), m_mat, 0.0)
        y = lb + _dot(lb.astype(BF16), e.astype(BF16))
        e = e - y - _dot(e.astype(BF16), y.astype(BF16))
        b *= 2
    e16 = e.astype(BF16)

    egc = jnp.exp(gc)
    rhs = jnp.concatenate([v * beta, kb * egc], axis=1)
    uw = rhs + _dot(e16, rhs.astype(BF16))
    u = uw[:, :DN_DV]
    w = uw[:, DN_DV:]

    q_dec = q * egc
    k_dec = k * jnp.exp(gtot - gc)
    s = s_ref[...]
    s16 = s.astype(BF16)
    wq = jnp.concatenate([w, q_dec], axis=0).astype(BF16)
    wqs = _dot(wq, s16)
    v_new = u - wqs[:tl]
    v16 = v_new.astype(BF16)
    o_ref[0] = wqs[tl:] + _dot(qk.astype(BF16), v16)
    s_ref[...] = s * jnp.exp(gtot) + _dot(jnp.transpose(k_dec).astype(BF16), v16)


def _deltanet(proj_main, conv_w, gates, prm, seq_len):
    lp = proj_main.shape[0]
    tl = TOKEN_TILE
    nt = lp // tl
    hb = tl // 16
    nhb = lp // 16

    def tile(d, i):
        return i + d * (nt - 1 - 2 * i)

    def main_spec(off):
        return pl.BlockSpec((tl, LANES), lambda d, h, i: (tile(d, i), off + h))

    def prev_spec(off):
        return pl.BlockSpec((16, LANES), lambda d, h, i: (jnp.maximum(tile(d, i) * hb - 1, 0), off + h))

    def next_spec(off):
        return pl.BlockSpec((16, LANES), lambda d, h, i: (jnp.minimum((tile(d, i) + 1) * hb, nhb - 1), off + h))

    def conv_spec(off):
        return pl.BlockSpec((3, LANES), lambda d, h, i: (0, off + h))

    in_specs = []
    args = []
    for off in (0, DN_HEADS, 2 * DN_HEADS):
        in_specs += [main_spec(off), prev_spec(off), next_spec(off)]
        args += [proj_main] * 3
    in_specs += [conv_spec(0), conv_spec(DN_HEADS), conv_spec(2 * DN_HEADS)]
    args += [conv_w] * 3
    in_specs += [pl.BlockSpec((tl, LANES), lambda d, h, i: (tile(d, i), 0)),
                 pl.BlockSpec((8, LANES), lambda d, h, i: (0, 0))]
    args += [gates, prm]
    return pl.pallas_call(
        functools.partial(_dn_kernel, seq_len=seq_len, tl=tl),
        grid=(2, DN_HEADS, nt),
        in_specs=in_specs,
        out_specs=pl.BlockSpec((1, tl, DN_DV), lambda d, h, i: (d, tile(d, i), h)),
        out_shape=jax.ShapeDtypeStruct((2, lp, DN_VW), F32),
        scratch_shapes=[pltpu.VMEM((DN_DK, DN_DV), F32)],
        compiler_params=_params(3),
        name="deltanet",
    )(*args)


def _gnorm_kernel(of_ref, ob_ref, z_ref, w_ref, o_ref, *, nh):
    for j in range(nh):
        sl = slice(DN_DV * j, DN_DV * (j + 1))
        o = of_ref[0, :, sl] + ob_ref[0, :, sl]
        o = o * lax.rsqrt(jnp.mean(o * o, axis=-1, keepdims=True) + NORM_EPS) * w_ref[...]
        z = z_ref[:, sl].astype(F32)
        o_ref[:, sl] = (o * (z * _sigmoid(z))).astype(o_ref.dtype)


def _gated_norm(o_dirs, proj_main, out_norm_w):
    lp = proj_main.shape[0]
    nh = 4
    wb = nh * DN_DV
    tr = _pick(lp, 768, 16)
    zoff = (2 * DN_QK + DN_VW) // wb
    return pl.pallas_call(
        functools.partial(_gnorm_kernel, nh=nh),
        grid=(lp // tr, DN_VW // wb),
        in_specs=[pl.BlockSpec((1, tr, wb), lambda i, j: (0, i, j)),
                  pl.BlockSpec((1, tr, wb), lambda i, j: (1, i, j)),
                  pl.BlockSpec((tr, wb), lambda i, j: (i, zoff + j)),
                  pl.BlockSpec((1, DN_DV), lambda i, j: (0, 0))],
        out_specs=pl.BlockSpec((tr, wb), lambda i, j: (i, j)),
        out_shape=jax.ShapeDtypeStruct((lp, DN_VW), BF16),
        compiler_params=_params(2),
        name="gated_norm",
    )(o_dirs, o_dirs, proj_main, out_norm_w.reshape(1, DN_DV).astype(F32))


def _attn_kernel(q_ref, k0_ref, kp_ref, kc_ref, kn_ref, v0_ref, vp_ref, vc_ref, vn_ref, prm_ref, o_ref, *,
                 seq_len, tq):
    i = pl.program_id(1)
    groups = SWA_HQ // SWA_HKV
    kall = jnp.concatenate([k0_ref[...], kp_ref[...], kc_ref[...], kn_ref[...]], axis=0)
    vall = jnp.concatenate([v0_ref[...], vp_ref[...], vc_ref[...], vn_ref[...]], axis=0)
    pq = i * tq + lax.broadcasted_iota(jnp.int32, (tq, 1), 0)
    ck = lax.broadcasted_iota(jnp.int32, (1, 4 * tq), 1)
    meta_part = ck < tq
    pk = (i - 1) * tq + (ck - tq)
    real_q = pq >= N_META
    dist = pq - pk
    adist = jnp.abs(dist)
    win_part = jnp.logical_not(meta_part)
    in_win = (real_q & (adist <= SWA_WINDOW)) | (jnp.logical_not(real_q) & (-dist <= SWA_WINDOW))
    win_ok = win_part & (pk >= N_META) & (pk < seq_len) & in_win
    ok = (meta_part & (ck < N_META)) | win_ok
    bias = jnp.where(real_q & win_part, -adist.astype(F32), 0.0)
    prm = prm_ref[0]
    scale = SWA_D ** -0.5
    for g in range(groups):
        sl = slice(SWA_D * g, SWA_D * (g + 1))
        slope = prm[g:g + 1, 0:1]
        sink = prm[groups + g:groups + g + 1, 0:1]
        s = _dot_nt(q_ref[:, sl], kall) * scale
        s = jnp.where(ok, s + slope * bias, -jnp.inf)
        m = jnp.maximum(jnp.max(s, axis=-1, keepdims=True), sink)
        p = jnp.exp(s - m)
        den = jnp.sum(p, axis=-1, keepdims=True) + jnp.exp(sink - m)
        o_ref[:, sl] = (_dot(p.astype(BF16), vall) / den).astype(o_ref.dtype)


def _window_attention(rest, attn_sink, seq_len):
    lp = rest.shape[0]
    tq = ATTN_TILE
    nq = lp // tq
    groups = SWA_HQ // SWA_HKV
    koff = SWA_QW // SWA_D
    voff = koff + SWA_HKV
    slopes = 2.0 ** (-8.0 * jnp.arange(1, SWA_HQ + 1, dtype=F32) / SWA_HQ)
    prm = jnp.concatenate([slopes.reshape(SWA_HKV, groups), attn_sink.astype(F32).reshape(SWA_HKV, groups)], axis=1)
    prm = jnp.broadcast_to(prm[:, :, None], (SWA_HKV, 2 * groups, LANES))

    def kv_specs(off):
        return [pl.BlockSpec((tq, SWA_D), lambda h, i: (0, off + h)),
                pl.BlockSpec((tq, SWA_D), lambda h, i: (jnp.maximum(i - 1, 0), off + h)),
                pl.BlockSpec((tq, SWA_D), lambda h, i: (i, off + h)),
                pl.BlockSpec((tq, SWA_D), lambda h, i: (jnp.minimum(i + 1, nq - 1), off + h))]

    qw = groups * SWA_D
    return pl.pallas_call(
        functools.partial(_attn_kernel, seq_len=seq_len, tq=tq),
        grid=(SWA_HKV, nq),
        in_specs=[pl.BlockSpec((tq, qw), lambda h, i: (i, h))] + kv_specs(koff) + kv_specs(voff)
        + [pl.BlockSpec((1, 2 * groups, LANES), lambda h, i: (h, 0, 0))],
        out_specs=pl.BlockSpec((tq, qw), lambda h, i: (i, h)),
        out_shape=jax.ShapeDtypeStruct((lp, SWA_QW), BF16),
        compiler_params=_params(2),
        name="window_attention",
    )(*([rest] * 9), prm)


def _branch_kernel(oa_ref, ob_ref, wa_ref, wb_ref, ga_ref, gb_ref, o_ref, wa16_ref, wb16_ref):
    @pl.when(pl.program_id(1) == 0)
    def _():
        wa16_ref[...] = wa_ref[...].astype(BF16)
        wb16_ref[...] = wb_ref[...].astype(BF16)

    ya = _dot(oa_ref[...], wa16_ref[...])
    yb = _dot(ob_ref[...], wb16_ref[...])
    o = _sigmoid(ga_ref[...].astype(F32)) * ya + _sigmoid(gb_ref[...].astype(F32)) * yb
    o_ref[...] = o.astype(o_ref.dtype)


def _branch_merge(o_a, o_b, w_a, w_b, rest, d_model):
    lp = o_a.shape[0]
    tn = _pick(d_model, 512, LANES)
    tm = _pick(lp, 768, 16)
    ga_off = (SWA_QW + 2 * SWA_KVW) // tn
    gb_off = (SWA_QW + 2 * SWA_KVW + d_model) // tn
    assert (SWA_QW + 2 * SWA_KVW) % tn == 0 and d_model % tn == 0
    return pl.pallas_call(
        _branch_kernel,
        grid=(d_model // tn, lp // tm),
        in_specs=[pl.BlockSpec((tm, DN_VW), lambda n, i: (i, 0)),
                  pl.BlockSpec((tm, SWA_QW), lambda n, i: (i, 0)),
                  pl.BlockSpec((DN_VW, tn), lambda n, i: (0, n)),
                  pl.BlockSpec((SWA_QW, tn), lambda n, i: (0, n)),
                  pl.BlockSpec((tm, tn), lambda n, i: (i, ga_off + n)),
                  pl.BlockSpec((tm, tn), lambda n, i: (i, gb_off + n))],
        out_specs=pl.BlockSpec((tm, tn), lambda n, i: (i, n)),
        out_shape=jax.ShapeDtypeStruct((lp, d_model), BF16),
        scratch_shapes=[pltpu.VMEM((DN_VW, tn), BF16), pltpu.VMEM((SWA_QW, tn), BF16)],
        compiler_params=_params(2),
        name="branch_merge",
    )(o_a, o_b, w_a, w_b, rest, rest)


def _router_kernel(h_ref, w_ref, wr_ref, xg_ref, afft_ref, *, seq_len, tr, d_model):
    i = pl.program_id(0)
    x = h_ref[...]
    xn = x * lax.rsqrt(jnp.mean(x * x, axis=-1, keepdims=True) + NORM_EPS) * w_ref[...]
    logits = _dot(xn.astype(BF16), wr_ref[...].astype(BF16))
    lane = lax.broadcasted_iota(jnp.int32, (1, LANES), 1)
    lm = jnp.where(lane < N_EXPERTS, logits, -jnp.inf)
    ex = jnp.exp(lm - jnp.max(lm, axis=-1, keepdims=True))
    aff = ex / jnp.sum(ex, axis=-1, keepdims=True)
    row = i * tr + lax.broadcasted_iota(jnp.int32, (tr, 1), 0)
    aff = jnp.where((row < seq_len) & (lane < N_EXPERTS), aff, -1.0)
    xg_ref[:, :d_model] = xn
    xg_ref[:, d_model:] = aff
    afft_ref[...] = jnp.transpose(aff)[:N_EXPERTS, :]


def _router(h2, norm2_w, w_router, seq_len):
    lp, d = h2.shape
    tr = TOKEN_TILE
    wr = jnp.zeros((d, LANES), F32).at[:, :N_EXPERTS].set(w_router.astype(F32))
    return pl.pallas_call(
        functools.partial(_router_kernel, seq_len=seq_len, tr=tr, d_model=d),
        grid=(lp // tr,),
        in_specs=[pl.BlockSpec((tr, d), lambda i: (i, 0)),
                  pl.BlockSpec((1, d), lambda i: (0, 0)),
                  pl.BlockSpec((d, LANES), lambda i: (0, 0))],
        out_specs=[pl.BlockSpec((tr, d + LANES), lambda i: (i, 0)),
                   pl.BlockSpec((N_EXPERTS, tr), lambda i: (0, i))],
        out_shape=[jax.ShapeDtypeStruct((lp, d + LANES), F32), jax.ShapeDtypeStruct((N_EXPERTS, lp), F32)],
        compiler_params=_params(1),
        name="router",
    )(h2, norm2_w.reshape(1, d).astype(F32), wr)


def _select_kernel(afft_ref, flat_ref, sel_ref, rank_ref, *, cap, cap_pad, lp):
    aff = afft_ref[...]
    capf = float(cap)

    def count_ge(thr):
        return jnp.sum(jnp.where(aff >= thr, 1.0, 0.0), axis=1, keepdims=True)

    def bisect(_, carry):
        lo, hi = carry
        mid = 0.5 * (lo + hi)
        ge = count_ge(mid) >= capf
        return jnp.where(ge, mid, lo), jnp.where(ge, hi, mid)

    lo0 = jnp.zeros((N_EXPERTS, 1), F32)
    hi0 = jnp.full((N_EXPERTS, 1), 2.0, F32)
    _, hi = lax.fori_loop(0, 40, bisect, (lo0, hi0))

    def refine(st):
        hi, tau, done, _ = st
        cand = jnp.max(jnp.where(aff < hi, aff, -2.0), axis=1, keepdims=True)
        found = jnp.where(count_ge(cand) >= capf, 1.0, 0.0)
        tau = jnp.where(done > 0.0, tau, cand)
        hi = jnp.where(done + found > 0.0, hi, cand)
        done = jnp.maximum(done, found)
        return hi, tau, done, jnp.sum(1.0 - done)

    zero = jnp.zeros((N_EXPERTS, 1), F32)
    _, tau, _, _ = lax.while_loop(lambda st: st[3] > 0.0, refine, (hi, zero, zero, jnp.float32(N_EXPERTS)))
    need = capf - jnp.sum(jnp.where(aff > tau, 1.0, 0.0), axis=1, keepdims=True)

    ra = lax.broadcasted_iota(jnp.int32, (LANES, LANES), 0)
    ca = lax.broadcasted_iota(jnp.int32, (LANES, LANES), 1)
    ut = jnp.where(ra <= ca, 1.0, 0.0).astype(BF16)
    carry_eq = jnp.zeros((N_EXPERTS, 1), F32)
    carry_sel = jnp.zeros((N_EXPERTS, 1), F32)
    for j in range(lp // LANES):
        sl = slice(LANES * j, LANES * (j + 1))
        bj = aff[:, sl]
        eqf = jnp.where(bj == tau, 1.0, 0.0)
        tie_rank = _dot(eqf.astype(BF16), ut) + carry_eq - eqf
        carry_eq = carry_eq + jnp.sum(eqf, axis=1, keepdims=True)
        self = jnp.where((bj > tau) | ((bj == tau) & (tie_rank < need)), 1.0, 0.0)
        rank_ref[:, sl] = _dot(self.astype(BF16), ut) + carry_sel - self
        carry_sel = carry_sel + jnp.sum(self, axis=1, keepdims=True)
        sel_ref[:, sl] = self

    kio = lax.broadcasted_iota(jnp.int32, (N_EXPERTS, 1), 0).astype(F32)
    running = jnp.zeros((1, lp), F32)
    flat = jnp.full((N_EXPERTS, lp), -1.0, F32)
    for e in range(N_EXPERTS):
        se = sel_ref[e:e + 1, :]
        val = float(e * cap_pad) + rank_ref[e:e + 1, :]
        flat = jnp.where((se > 0.0) & (kio == running), val, flat)
        running = running + se
    flat_ref[...] = flat.astype(jnp.int32)


def _select(afft, cap, cap_pad):
    lp = afft.shape[1]
    return pl.pallas_call(
        functools.partial(_select_kernel, cap=cap, cap_pad=cap_pad, lp=lp),
        out_shape=jax.ShapeDtypeStruct((N_EXPERTS, lp), jnp.int32),
        scratch_shapes=[pltpu.VMEM((N_EXPERTS, lp), F32), pltpu.VMEM((N_EXPERTS, lp), F32)],
        compiler_params=pltpu.CompilerParams(vmem_limit_bytes=VMEM_LIMIT),
        name="expert_select",
    )(afft)


def _row_copy(src, s_row, dst, d_row, sem):
    return pltpu.make_async_copy(src.at[pl.ds(s_row, 1)], dst.at[pl.ds(d_row, 1)], sem)


def _dispatch_kernel(xg_ref, flat_ref, xe_ref, flat_smem, zero_ref, sem, csem, *, tt, cap, cap_pad):
    i = pl.program_id(0)
    cp = pltpu.make_async_copy(flat_ref, flat_smem, csem)
    cp.start()

    @pl.when(i == 0)
    def _():
        zero_ref[...] = jnp.zeros_like(zero_ref)
        for e in range(N_EXPERTS):
            for s in range(cap, cap_pad):
                _row_copy(zero_ref, 0, xe_ref, e * cap_pad + s, sem).start()
        for e in range(N_EXPERTS):
            for s in range(cap, cap_pad):
                _row_copy(zero_ref, 0, xe_ref, e * cap_pad + s, sem).wait()

    cp.wait()

    def per_token(t, n):
        def cond(st):
            k, _ = st
            return (k < N_EXPERTS) & (flat_smem[jnp.minimum(k, N_EXPERTS - 1), t] >= 0)

        def body(st):
            k, m = st
            _row_copy(xg_ref, t, xe_ref, flat_smem[k, t], sem).start()
            return k + 1, m + 1

        _, n = lax.while_loop(cond, body, (jnp.int32(0), n))
        return n

    n = lax.fori_loop(0, tt, per_token, jnp.int32(0))

    def drain(_, c):
        _row_copy(xg_ref, 0, xe_ref, 0, sem).wait()
        return c

    lax.fori_loop(0, n, drain, 0)


def _dispatch(xg, flat, cap, cap_pad):
    lp, width = xg.shape
    tt = TOKEN_TILE
    return pl.pallas_call(
        functools.partial(_dispatch_kernel, tt=tt, cap=cap, cap_pad=cap_pad),
        grid=(lp // tt,),
        in_specs=[pl.BlockSpec((tt, width), lambda i: (i, 0)),
                  pl.BlockSpec((N_EXPERTS, tt), lambda i: (0, i))],
        out_specs=pl.BlockSpec(memory_space=pl.ANY),
        out_shape=jax.ShapeDtypeStruct((N_EXPERTS * cap_pad, width), F32),
        scratch_shapes=[pltpu.SMEM((N_EXPERTS, tt), jnp.int32), pltpu.VMEM((8, width), F32),
                        pltpu.SemaphoreType.DMA, pltpu.SemaphoreType.DMA],
        compiler_params=pltpu.CompilerParams(dimension_semantics=("arbitrary",), vmem_limit_bytes=VMEM_LIMIT,
                                             has_side_effects=True),
        name="expert_dispatch",
    )(xg, flat)


def _ffn_a_kernel(x_ref, wg_ref, wu_ref, o_ref, x16_ref):
    @pl.when(pl.program_id(1) == 0)
    def _():
        x16_ref[...] = x_ref[...].astype(BF16)

    x16 = x16_ref[...]
    g = _dot(x16, wg_ref[0].astype(BF16))
    u = _dot(x16, wu_ref[0].astype(BF16))
    o_ref[...] = (g * _sigmoid(g) * u).astype(o_ref.dtype)


def _ffn_a(xe, w_gate, w_up, cap_pad, d_model):
    _, _, ff = w_gate.shape
    tf = _pick(ff, 256, LANES)
    return pl.pallas_call(
        _ffn_a_kernel,
        grid=(N_EXPERTS, ff // tf),
        in_specs=[pl.BlockSpec((cap_pad, d_model), lambda e, f: (e, 0), pipeline_mode=pl.Buffered(1)),
                  pl.BlockSpec((1, d_model, tf), lambda e, f: (e, 0, f)),
                  pl.BlockSpec((1, d_model, tf), lambda e, f: (e, 0, f))],
        out_specs=pl.BlockSpec((cap_pad, tf), lambda e, f: (e, f)),
        out_shape=jax.ShapeDtypeStruct((N_EXPERTS * cap_pad, ff), BF16),
        scratch_shapes=[pltpu.VMEM((cap_pad, d_model), BF16)],
        compiler_params=_params(2),
        name="expert_ffn_in",
    )(xe, w_gate, w_up)


def _ffn_b_kernel(h_ref, wd_ref, aff_ref, o_ref):
    e = pl.program_id(0)
    y = _dot(h_ref[...], wd_ref[0].astype(BF16))
    lane = lax.broadcasted_iota(jnp.int32, (1, LANES), 1)
    gate = jnp.sum(jnp.where(lane == e, aff_ref[...], 0.0), axis=-1, keepdims=True)
    o_ref[...] = y * gate


def _ffn_b(hid, w_down, xe, cap_pad, d_model):
    _, ff, _ = w_down.shape
    td = _pick(d_model, 1024, LANES)
    return pl.pallas_call(
        _ffn_b_kernel,
        grid=(N_EXPERTS, d_model // td),
        in_specs=[pl.BlockSpec((cap_pad, ff), lambda e, n: (e, 0)),
                  pl.BlockSpec((1, ff, td), lambda e, n: (e, 0, n)),
                  pl.BlockSpec((cap_pad, LANES), lambda e, n: (e, d_model // LANES))],
        out_specs=pl.BlockSpec((cap_pad, td), lambda e, n: (e, n)),
        out_shape=jax.ShapeDtypeStruct((N_EXPERTS * cap_pad, d_model), F32),
        compiler_params=_params(2),
        name="expert_ffn_out",
    )(hid, w_down, xe)


def _combine_kernel(flat_ref, h_ref, ye_ref, nw_ref, o_ref, flat_smem, acc_ref, stage_ref, mask_ref,
                    sem, csem, hsem, *, tt):
    i = pl.program_id(0)
    hc = pltpu.make_async_copy(h_ref.at[pl.ds(pl.multiple_of(N_META + i * tt, 8), tt)], acc_ref, hsem)
    hc.start()
    cp = pltpu.make_async_copy(flat_ref, flat_smem, csem)
    cp.start()

    @pl.when(i == 0)
    def _():
        stage_ref[...] = jnp.zeros_like(stage_ref)

    cp.wait()
    hc.wait()
    ones_row = jnp.ones((1, LANES), F32)

    def level(st):
        k, _ = st

        def per_token(t, n):
            f = flat_smem[k, t]
            hit = f >= 0
            mask_ref[pl.ds(t, 1), :] = ones_row * hit.astype(F32)

            @pl.when(hit)
            def _():
                _row_copy(ye_ref, f, stage_ref, t, sem).start()

            return n + hit.astype(jnp.int32)

        n = lax.fori_loop(0, tt, per_token, jnp.int32(0))

        def drain(_, c):
            _row_copy(ye_ref, 0, stage_ref, 0, sem).wait()
            return c

        lax.fori_loop(0, n, drain, 0)
        acc_ref[...] += jnp.where(mask_ref[:, 0:1] > 0.0, stage_ref[...], 0.0)
        return k + 1, n

    lax.while_loop(lambda st: (st[0] < N_EXPERTS) & (st[1] > 0), level, (jnp.int32(0), jnp.int32(1)))
    x = acc_ref[...]
    o_ref[...] = x * lax.rsqrt(jnp.mean(x * x, axis=-1, keepdims=True) + NORM_EPS) * nw_ref[...]


def _combine(flat_real, h2, ye, norm_f_w, n_real):
    d = h2.shape[1]
    tt = _pick(n_real, TOKEN_TILE, LANES)
    return pl.pallas_call(
        functools.partial(_combine_kernel, tt=tt),
        grid=(n_real // tt,),
        in_specs=[pl.BlockSpec((N_EXPERTS, tt), lambda i: (0, i)),
                  pl.BlockSpec(memory_space=pl.ANY),
                  pl.BlockSpec(memory_space=pl.ANY),
                  pl.BlockSpec((1, d), lambda i: (0, 0))],
        out_specs=pl.BlockSpec((tt, d), lambda i: (i, 0)),
        out_shape=jax.ShapeDtypeStruct((n_real, d), F32),
        scratch_shapes=[pltpu.SMEM((N_EXPERTS, tt), jnp.int32), pltpu.VMEM((tt, d), F32), pltpu.VMEM((tt, d), F32),
                        pltpu.VMEM((tt, LANES), F32),
                        pltpu.SemaphoreType.DMA, pltpu.SemaphoreType.DMA, pltpu.SemaphoreType.DMA],
        compiler_params=_params(1),
        name="expert_combine",
    )(flat_real, h2, ye, norm_f_w.reshape(1, d).astype(F32))


def _layer(hp, seq_len, norm1_w, w_in, conv_w, a_log_fwd, a_log_bwd, dt_bias_fwd, dt_bias_bwd, out_norm_w,
           w_branch_a, attn_sink, w_branch_b, w_out, norm2_w, w_router, w_gate, w_up, w_down):
    lp, d = hp.shape
    main_w = 2 * DN_QK + 2 * DN_VW
    gate_w = 4 * DN_HEADS
    rest_w = SWA_QW + 2 * SWA_KVW + 2 * d
    n = _rmsnorm(hp, norm1_w, BF16)
    proj_main = _matmul(n, w_in, main_w, 0, BF16, name="in_proj_main")
    w_gates = jnp.zeros((d, LANES), F32).at[:, :gate_w].set(w_in[:, main_w:main_w + gate_w])
    gates = _matmul(n, w_gates, LANES, 0, F32, name="in_proj_gates")
    w_rest = w_in[:, main_w + gate_w:]
    rest = _matmul(n, w_rest, rest_w, 0, BF16, name="in_proj_rest")

    prm = jnp.zeros((8, LANES), F32)
    prm = prm.at[0, 2 * DN_HEADS:4 * DN_HEADS].set(jnp.concatenate([a_log_fwd, a_log_bwd]).astype(F32))
    prm = prm.at[1, 2 * DN_HEADS:4 * DN_HEADS].set(jnp.concatenate([dt_bias_fwd, dt_bias_bwd]).astype(F32))
    o_dirs = _deltanet(proj_main, conv_w.astype(F32), gates, prm, seq_len)
    o_a = _gated_norm(o_dirs, proj_main, out_norm_w)
    o_b = _window_attention(rest, attn_sink, seq_len)

    mixed = _branch_merge(o_a, o_b, w_branch_a, w_branch_b, rest, d)
    h2 = _matmul(mixed, w_out, d, 0, F32, res=hp, name="out_proj")

    cap = EC_CAPACITY * seq_len // N_EXPERTS
    cap_pad = -(-cap // 16) * 16
    xg, afft = _router(h2, norm2_w, w_router, seq_len)
    flat = _select(afft, cap, cap_pad)
    xe = _dispatch(xg, flat, cap, cap_pad)
    hid = _ffn_a(xe, w_gate, w_up, cap_pad, d)
    ye = _ffn_b(hid, w_down, xe, cap_pad, d)
    return h2, ye, flat


def kernel(x, meta_tokens, norm1_w, w_in, conv_w, a_log_fwd, a_log_bwd, dt_bias_fwd, dt_bias_bwd, out_norm_w,
           w_branch_a, attn_sink, w_branch_b, w_out, norm2_w, w_router, w_gate, w_up, w_down, norm_f_w):
    batch, seq, d = x.shape
    depth = norm1_w.shape[0]
    assert depth == 1, "the final norm is fused into the last layer's expert combine"
    seq_len = N_META + seq
    lp = -(-seq_len // TOKEN_TILE) * TOKEN_TILE
    outs = []
    for b in range(batch):
        h = jnp.concatenate([meta_tokens.astype(x.dtype), x[b]], axis=0)
        hp = jnp.pad(h, ((0, lp - seq_len), (0, 0)))
        h2, ye, flat = _layer(hp, seq_len, norm1_w[0], w_in[0], conv_w[0], a_log_fwd[0], a_log_bwd[0],
                              dt_bias_fwd[0], dt_bias_bwd[0], out_norm_w[0], w_branch_a[0], attn_sink[0],
                              w_branch_b[0], w_out[0], norm2_w[0], w_router[0], w_gate[0], w_up[0], w_down[0])
        outs.append(_combine(flat[:, N_META:seq_len], h2, ye, norm_f_w, seq))
    return jnp.stack(outs, axis=0)
```

```python
import functools

import jax
import jax.numpy as jnp
from jax import lax
from jax.experimental import pallas as pl
from jax.experimental.pallas import tpu as pltpu

F32 = jnp.float32
BF16 = jnp.bfloat16

N_META = 16
NORM_EPS = 1e-6
DN_HEADS = 16
DN_DK = 128
DN_DV = 128
SWA_HQ = 16
SWA_HKV = 4
SWA_D = 128
SWA_WINDOW = 128
N_EXPERTS = 16
EC_CAPACITY = 2

LANES = 128
TOKEN_TILE = 256
DN_CHUNK = 128
DN_HEADS_PER_STEP = 4
ATTN_TILE = 128
ROW_PAD = 8
VMEM_LIMIT = 56 * 1024 * 1024

DN_QK = DN_HEADS * DN_DK
DN_VW = DN_HEADS * DN_DV
SWA_QW = SWA_HQ * SWA_D
SWA_KVW = SWA_HKV * SWA_D


def _params(n_grid):
    return pltpu.CompilerParams(dimension_semantics=("arbitrary",) * n_grid, vmem_limit_bytes=VMEM_LIMIT)


def _pick(n, target, mult):
    best = None
    for t in range(mult, min(n, target) + 1, mult):
        if n % t == 0:
            best = t
    assert best is not None, (n, target, mult)
    return best


def _sigmoid(x):
    return 1.0 / (1.0 + jnp.exp(-x))


def _dot(a, b):
    return jnp.dot(a, b, preferred_element_type=F32)


def _dot_nt(a, b):
    return lax.dot_general(a, b, (((1,), (1,)), ((), ())), preferred_element_type=F32)


def _rms_kernel(x_ref, w_ref, o_ref):
    x = x_ref[...]
    ms = jnp.mean(x * x, axis=-1, keepdims=True)
    o_ref[...] = (x * lax.rsqrt(ms + NORM_EPS) * w_ref[...]).astype(o_ref.dtype)


def _rmsnorm(x, w, out_dtype):
    m, d = x.shape
    tr = _pick(m, 256, 16)
    return pl.pallas_call(
        _rms_kernel,
        grid=(m // tr,),
        in_specs=[pl.BlockSpec((tr, d), lambda i: (i, 0)), pl.BlockSpec((1, d), lambda i: (0, 0))],
        out_specs=pl.BlockSpec((tr, d), lambda i: (i, 0)),
        out_shape=jax.ShapeDtypeStruct((m, d), out_dtype),
        compiler_params=_params(1),
        name="rmsnorm",
    )(x, w.reshape(1, d).astype(F32))


def _mm_kernel(*refs, has_res, w_rows):
    if has_res:
        a_ref, w_ref, r_ref, o_ref, wb_ref = refs
    else:
        a_ref, w_ref, o_ref, wb_ref = refs

    @pl.when(pl.program_id(1) == 0)
    def _():
        w = w_ref[...]
        wb_ref[...] = (jnp.transpose(w) if w_rows else w).astype(BF16)

    acc = _dot(a_ref[...], wb_ref[...])
    if has_res:
        acc = acc + r_ref[...]
    o_ref[...] = acc.astype(o_ref.dtype)


def _matmul(a, w, n_cols, off, out_dtype, res=None, w_rows=False, name="matmul"):
    m, k = a.shape
    tn = _pick(n_cols, 512, LANES)
    tm = _pick(m, 1408 if res is None else 768, 16)
    if w_rows:
        if off % tn == 0:
            w_spec = pl.BlockSpec((tn, k), lambda n, i: (off // tn + n, 0))
        else:
            assert off % 8 == 0
            w_spec = pl.BlockSpec((pl.Element(tn), pl.Element(k)), lambda n, i: (pl.multiple_of(off + n * tn, 8), 0))
    else:
        assert off % tn == 0
        w_spec = pl.BlockSpec((k, tn), lambda n, i: (0, off // tn + n))
    in_specs = [pl.BlockSpec((tm, k), lambda n, i: (i, 0)), w_spec]
    args = [a, w]
    if res is not None:
        in_specs.append(pl.BlockSpec((tm, tn), lambda n, i: (i, n)))
        args.append(res)
    return pl.pallas_call(
        functools.partial(_mm_kernel, has_res=res is not None, w_rows=w_rows),
        grid=(n_cols // tn, m // tm),
        in_specs=in_specs,
        out_specs=pl.BlockSpec((tm, tn), lambda n, i: (i, n)),
        out_shape=jax.ShapeDtypeStruct((m, n_cols), out_dtype),
        scratch_shapes=[pltpu.VMEM((k, tn), BF16)],
        compiler_params=_params(2),
        name=name,
    )(*args)


def _dn_kernel(q_ref, qp_ref, qn_ref, k_ref, kp_ref, kn_ref, v_ref, vp_ref, vn_ref, cq_ref, ck_ref, cv_ref,
               sm_ref, prm_ref, o_ref, s_ref, *, seq_len, tl, backward):
    hg = pl.program_id(0)
    i = pl.program_id(1)
    nt = pl.num_programs(1)
    ti = nt - 1 - i if backward else i
    cs = DN_CHUNK
    cs_shift = cs.bit_length() - 1
    hpb = DN_HEADS_PER_STEP

    @pl.when(i == 0)
    def _():
        s_ref[...] = jnp.zeros_like(s_ref)

    row = lax.broadcasted_iota(jnp.int32, (tl, 1), 0)
    validf = ((ti * tl + row) < seq_len).astype(F32)
    has_prev = (ti > 0).astype(F32)
    has_next = (ti < nt - 1).astype(F32)

    def conv_silu(x_ref, p_ref, n_ref, c_ref):
        x = x_ref[...].astype(F32)
        prev = p_ref[...].astype(F32)[15:16, :] * has_prev
        nxt = n_ref[...].astype(F32)[0:1, :] * has_next
        xm1 = jnp.where(row == 0, prev, pltpu.roll(x, 1, 0))
        xp1 = jnp.where(row == tl - 1, nxt, pltpu.roll(x, tl - 1, 0))
        c = c_ref[...]
        y = xm1 * c[0:1] + x * c[1:2] + xp1 * c[2:3]
        return y * _sigmoid(y)

    q_all = conv_silu(q_ref, qp_ref, qn_ref, cq_ref)
    k_all = conv_silu(k_ref, kp_ref, kn_ref, ck_ref)
    v_all = conv_silu(v_ref, vp_ref, vn_ref, cv_ref)

    lane = lax.broadcasted_iota(jnp.int32, (1, LANES), 1)
    sm = sm_ref[...]
    prm = prm_ref[...]
    z = sm + prm[1:2]
    softplus = jnp.maximum(z, 0.0) + jnp.log1p(jnp.exp(-jnp.abs(z)))
    beta_all = _sigmoid(sm) * validf
    g_all = -jnp.exp(prm[0:1]) * softplus * validf

    r2 = lax.broadcasted_iota(jnp.int32, (tl, tl), 0)
    c2 = lax.broadcasted_iota(jnp.int32, (tl, tl), 1)
    incl2 = ((r2 >> cs_shift) == (c2 >> cs_shift)) & ((r2 <= c2) if backward else (r2 >= c2))
    tri = jnp.where(incl2, 1.0, 0.0).astype(BF16)
    g1 = g_all.astype(BF16)
    rem = g_all - g1.astype(F32)
    g2 = rem.astype(BF16)
    g3 = (rem - g2.astype(F32)).astype(BF16)
    gc_all = _dot(tri, g1) + _dot(tri, g2) + _dot(tri, g3)

    r = lax.broadcasted_iota(jnp.int32, (cs, cs), 0)
    c = lax.broadcasted_iota(jnp.int32, (cs, cs), 1)
    incl = (r <= c) if backward else (r >= c)
    strict = (r < c) if backward else (r > c)
    levels = []
    b = 1
    while b < cs:
        sh = b.bit_length() - 1
        levels.append(((r >> (sh + 1)) == (c >> (sh + 1))) & ((r >> sh) != (c >> sh)))
        b *= 2

    chunk_order = range(tl // cs - 1, -1, -1) if backward else range(tl // cs)
    streams = []
    for j in range(hpb):
        hs = slice(DN_DK * j, DN_DK * (j + 1))
        q = q_all[:, hs]
        q = q * lax.rsqrt(jnp.sum(q * q, axis=-1, keepdims=True) + NORM_EPS) * (DN_DK ** -0.5) * validf
        k = k_all[:, hs]
        k = k * lax.rsqrt(jnp.sum(k * k, axis=-1, keepdims=True) + NORM_EPS) * validf
        v = v_all[:, hs] * validf
        col = (DN_HEADS if backward else 0) + hg * hpb + j
        beta = jnp.sum(jnp.where(lane == col, beta_all, 0.0), axis=-1, keepdims=True)
        g = jnp.sum(jnp.where(lane == 2 * DN_HEADS + col, g_all, 0.0), axis=-1, keepdims=True)
        gcum = jnp.sum(jnp.where(lane == 2 * DN_HEADS + col, gc_all, 0.0), axis=-1, keepdims=True)
        for ci in chunk_order:
            rs = slice(ci * cs, (ci + 1) * cs)
            qc, kc, vc, bc = q[rs], k[rs], v[rs], beta[rs]
            gtot = jnp.sum(g[rs], axis=0, keepdims=True)
            gb = jnp.broadcast_to(gcum[rs], (cs, LANES))
            gj = jnp.transpose(gb)[0:1, :]
            decay = jnp.where(incl, jnp.exp(jnp.where(incl, gb[:, 0:1] - gj, 0.0)), 0.0)
            kb = kc * bc
            k16 = kc.astype(BF16)
            egc = jnp.exp(gb)
            streams.append(dict(
                j=j, rs=rs, hs=hs, gtot=gtot,
                m=jnp.where(strict, _dot_nt(kb.astype(BF16), k16) * decay, 0.0),
                qk=(_dot_nt(qc.astype(BF16), k16) * decay).astype(BF16),
                rhs=jnp.concatenate([vc * bc, kb * egc], axis=1),
                q_dec=qc * egc,
                k_dec_t=jnp.transpose(kc * jnp.exp(gtot - gb)).astype(BF16)))

    for st in streams:
        st["e"] = -jnp.where(levels[0], st["m"], 0.0)
    for mask in levels[1:]:
        for st in streams:
            lb = jnp.where(mask, st["m"], 0.0)
            st["lb"] = lb
            st["y"] = lb + _dot(lb.astype(BF16), st["e"].astype(BF16))
        for st in streams:
            st["e"] = st["e"] - st["y"] - _dot(st["e"].astype(BF16), st["y"].astype(BF16))
    for st in streams:
        st["uw"] = st["rhs"] + _dot(st["e"].astype(BF16), st["rhs"].astype(BF16))

    state = [s_ref[j] for j in range(hpb)]
    for step in range(tl // cs):
        cur = [streams[j * (tl // cs) + step] for j in range(hpb)]
        for st in cur:
            wq = jnp.concatenate([st["uw"][:, DN_DV:], st["q_dec"]], axis=0).astype(BF16)
            st["wqs"] = _dot(wq, state[st["j"]].astype(BF16))
        for st in cur:
            v16 = (st["uw"][:, :DN_DV] - st["wqs"][:cs]).astype(BF16)
            o_ref[st["rs"], st["hs"]] = st["wqs"][cs:] + _dot(st["qk"], v16)
            state[st["j"]] = state[st["j"]] * jnp.exp(st["gtot"]) + _dot(st["k_dec_t"], v16)
    for j in range(hpb):
        s_ref[j] = state[j]


def _deltanet(proj_main, conv_w, gates, prm, seq_len, backward):
    lp = proj_main.shape[0]
    tl = TOKEN_TILE
    nt = lp // tl
    hb = tl // 16
    nhb = lp // 16
    hpb = DN_HEADS_PER_STEP
    wb = hpb * DN_DK
    ngroups = DN_HEADS // hpb

    def tile(i):
        return nt - 1 - i if backward else i

    def main_spec(off):
        return pl.BlockSpec((tl, wb), lambda h, i: (tile(i), off + h))

    def prev_spec(off):
        return pl.BlockSpec((16, wb), lambda h, i: (jnp.maximum(tile(i) * hb - 1, 0), off + h))

    def next_spec(off):
        return pl.BlockSpec((16, wb), lambda h, i: (jnp.minimum((tile(i) + 1) * hb, nhb - 1), off + h))

    def conv_spec(off):
        return pl.BlockSpec((3, wb), lambda h, i: (0, off + h))

    in_specs = []
    args = []
    for off in (0, ngroups, 2 * ngroups):
        in_specs += [main_spec(off), prev_spec(off), next_spec(off)]
        args += [proj_main] * 3
    in_specs += [conv_spec(0), conv_spec(ngroups), conv_spec(2 * ngroups)]
    args += [conv_w] * 3
    in_specs += [pl.BlockSpec((tl, LANES), lambda h, i: (tile(i), 0)),
                 pl.BlockSpec((8, LANES), lambda h, i: (0, 0))]
    args += [gates, prm]
    return pl.pallas_call(
        functools.partial(_dn_kernel, seq_len=seq_len, tl=tl, backward=backward),
        grid=(ngroups, nt),
        in_specs=in_specs,
        out_specs=pl.BlockSpec((tl, wb), lambda h, i: (tile(i), h)),
        out_shape=jax.ShapeDtypeStruct((lp, DN_VW), F32),
        scratch_shapes=[pltpu.VMEM((hpb, DN_DK, DN_DV), F32)],
        compiler_params=_params(2),
        name="deltanet_bwd" if backward else "deltanet_fwd",
    )(*args)


def _gnorm_kernel(of_ref, ob_ref, z_ref, w_ref, o_ref, *, nh):
    for j in range(nh):
        sl = slice(DN_DV * j, DN_DV * (j + 1))
        o = of_ref[:, sl] + ob_ref[:, sl]
        o = o * lax.rsqrt(jnp.mean(o * o, axis=-1, keepdims=True) + NORM_EPS) * w_ref[...]
        z = z_ref[:, sl].astype(F32)
        o_ref[:, sl] = (o * (z * _sigmoid(z))).astype(o_ref.dtype)


def _gated_norm(o_f, o_b, proj_main, out_norm_w):
    lp = proj_main.shape[0]
    nh = 4
    wb = nh * DN_DV
    tr = _pick(lp, 768, 16)
    zoff = (2 * DN_QK + DN_VW) // wb
    return pl.pallas_call(
        functools.partial(_gnorm_kernel, nh=nh),
        grid=(lp // tr, DN_VW // wb),
        in_specs=[pl.BlockSpec((tr, wb), lambda i, j: (i, j)),
                  pl.BlockSpec((tr, wb), lambda i, j: (i, j)),
                  pl.BlockSpec((tr, wb), lambda i, j: (i, zoff + j)),
                  pl.BlockSpec((1, DN_DV), lambda i, j: (0, 0))],
        out_specs=pl.BlockSpec((tr, wb), lambda i, j: (i, j)),
        out_shape=jax.ShapeDtypeStruct((lp, DN_VW), BF16),
        compiler_params=_params(2),
        name="gated_norm",
    )(o_f, o_b, proj_main, out_norm_w.reshape(1, DN_DV).astype(F32))


def _attn_kernel(q_ref, k0_ref, kp_ref, kc_ref, kn_ref, v0_ref, vp_ref, vc_ref, vn_ref, prm_ref, o_ref, *,
                 seq_len, tq):
    i = pl.program_id(1)
    groups = SWA_HQ // SWA_HKV
    kall = jnp.concatenate([k0_ref[...], kp_ref[...], kc_ref[...], kn_ref[...]], axis=0)
    vall = jnp.concatenate([v0_ref[...], vp_ref[...], vc_ref[...], vn_ref[...]], axis=0)
    pq = i * tq + lax.broadcasted_iota(jnp.int32, (tq, 1), 0)
    ck = lax.broadcasted_iota(jnp.int32, (1, 4 * tq), 1)
    meta_part = ck < tq
    pk = (i - 1) * tq + (ck - tq)
    real_q = pq >= N_META
    dist = pq - pk
    adist = jnp.abs(dist)
    win_part = jnp.logical_not(meta_part)
    in_win = (real_q & (adist <= SWA_WINDOW)) | (jnp.logical_not(real_q) & (-dist <= SWA_WINDOW))
    win_ok = win_part & (pk >= N_META) & (pk < seq_len) & in_win
    ok = (meta_part & (ck < N_META)) | win_ok
    bias = jnp.where(ok, jnp.where(real_q & win_part, -adist.astype(F32), 0.0), -jnp.inf)
    prm = prm_ref[0]
    scale = SWA_D ** -0.5
    for g in range(groups):
        sl = slice(SWA_D * g, SWA_D * (g + 1))
        slope = prm[g:g + 1, 0:1]
        sink = prm[groups + g:groups + g + 1, 0:1]
        s = _dot_nt(q_ref[:, sl], kall) * scale + slope * bias
        m = jnp.maximum(jnp.max(s, axis=-1, keepdims=True), sink)
        p = jnp.exp(s - m)
        den = jnp.sum(p, axis=-1, keepdims=True) + jnp.exp(sink - m)
        o_ref[:, sl] = (_dot(p.astype(BF16), vall) / den).astype(o_ref.dtype)


def _window_attention(rest, attn_sink, seq_len):
    lp = rest.shape[0]
    tq = ATTN_TILE
    nq = lp // tq
    groups = SWA_HQ // SWA_HKV
    koff = SWA_QW // SWA_D
    voff = koff + SWA_HKV
    slopes = 2.0 ** (-8.0 * jnp.arange(1, SWA_HQ + 1, dtype=F32) / SWA_HQ)
    prm = jnp.concatenate([slopes.reshape(SWA_HKV, groups), attn_sink.astype(F32).reshape(SWA_HKV, groups)], axis=1)
    prm = jnp.broadcast_to(prm[:, :, None], (SWA_HKV, 2 * groups, LANES))

    def kv_specs(off):
        return [pl.BlockSpec((tq, SWA_D), lambda h, i: (0, off + h)),
                pl.BlockSpec((tq, SWA_D), lambda h, i: (jnp.maximum(i - 1, 0), off + h)),
                pl.BlockSpec((tq, SWA_D), lambda h, i: (i, off + h)),
                pl.BlockSpec((tq, SWA_D), lambda h, i: (jnp.minimum(i + 1, nq - 1), off + h))]

    qw = groups * SWA_D
    return pl.pallas_call(
        functools.partial(_attn_kernel, seq_len=seq_len, tq=tq),
        grid=(SWA_HKV, nq),
        in_specs=[pl.BlockSpec((tq, qw), lambda h, i: (i, h))] + kv_specs(koff) + kv_specs(voff)
        + [pl.BlockSpec((1, 2 * groups, LANES), lambda h, i: (h, 0, 0))],
        out_specs=pl.BlockSpec((tq, qw), lambda h, i: (i, h)),
        out_shape=jax.ShapeDtypeStruct((lp, SWA_QW), BF16),
        compiler_params=_params(2),
        name="window_attention",
    )(*([rest] * 9), prm)


def _branch_kernel(oa_ref, ob_ref, wa_ref, wb_ref, ga_ref, gb_ref, o_ref, wa16_ref, wb16_ref):
    @pl.when(pl.program_id(1) == 0)
    def _():
        wa16_ref[...] = wa_ref[...].astype(BF16)
        wb16_ref[...] = wb_ref[...].astype(BF16)

    ya = _dot(oa_ref[...], wa16_ref[...])
    yb = _dot(ob_ref[...], wb16_ref[...])
    o = _sigmoid(ga_ref[...].astype(F32)) * ya + _sigmoid(gb_ref[...].astype(F32)) * yb
    o_ref[...] = o.astype(o_ref.dtype)


def _branch_merge(o_a, o_b, w_a, w_b, rest, d_model):
    lp = o_a.shape[0]
    tn = _pick(d_model, 512, LANES)
    tm = _pick(lp, 768, 16)
    ga_off = (SWA_QW + 2 * SWA_KVW) // tn
    gb_off = (SWA_QW + 2 * SWA_KVW + d_model) // tn
    assert (SWA_QW + 2 * SWA_KVW) % tn == 0 and d_model % tn == 0
    return pl.pallas_call(
        _branch_kernel,
        grid=(d_model // tn, lp // tm),
        in_specs=[pl.BlockSpec((tm, DN_VW), lambda n, i: (i, 0)),
                  pl.BlockSpec((tm, SWA_QW), lambda n, i: (i, 0)),
                  pl.BlockSpec((DN_VW, tn), lambda n, i: (0, n)),
                  pl.BlockSpec((SWA_QW, tn), lambda n, i: (0, n)),
                  pl.BlockSpec((tm, tn), lambda n, i: (i, ga_off + n)),
                  pl.BlockSpec((tm, tn), lambda n, i: (i, gb_off + n))],
        out_specs=pl.BlockSpec((tm, tn), lambda n, i: (i, n)),
        out_shape=jax.ShapeDtypeStruct((lp, d_model), BF16),
        scratch_shapes=[pltpu.VMEM((DN_VW, tn), BF16), pltpu.VMEM((SWA_QW, tn), BF16)],
        compiler_params=_params(2),
        name="branch_merge",
    )(o_a, o_b, w_a, w_b, rest, rest)


def _router_kernel(h_ref, w_ref, wr_ref, xg_ref, afft_ref, *, seq_len, tr, d_model):
    i = pl.program_id(0)
    x = h_ref[...]
    xn = x * lax.rsqrt(jnp.mean(x * x, axis=-1, keepdims=True) + NORM_EPS) * w_ref[...]
    logits = _dot_nt(xn.astype(BF16), wr_ref[...].astype(BF16))
    lane = lax.broadcasted_iota(jnp.int32, (1, LANES), 1)
    lm = jnp.where(lane < N_EXPERTS, logits, -jnp.inf)
    ex = jnp.exp(lm - jnp.max(lm, axis=-1, keepdims=True))
    aff = ex / jnp.sum(ex, axis=-1, keepdims=True)
    row = i * tr + lax.broadcasted_iota(jnp.int32, (tr, 1), 0)
    aff = jnp.where((row < seq_len) & (lane < N_EXPERTS), aff, -1.0)
    nx = d_model // LANES
    for c in range(nx):
        xg_ref[:, c, :] = xn[:, LANES * c:LANES * (c + 1)]
    xg_ref[:, nx, :] = aff
    xg_ref[:, nx + 1:, :] = jnp.zeros((tr, ROW_PAD - 1, LANES), F32)
    afft_ref[...] = jnp.transpose(aff)[:N_EXPERTS, :]


def _router(h2, norm2_w, w_router, seq_len):
    lp, d = h2.shape
    tr = TOKEN_TILE
    wr = jnp.zeros((LANES, d), F32).at[:N_EXPERTS].set(jnp.swapaxes(w_router, 0, 1).astype(F32))
    nc = d // LANES + ROW_PAD
    return pl.pallas_call(
        functools.partial(_router_kernel, seq_len=seq_len, tr=tr, d_model=d),
        grid=(lp // tr,),
        in_specs=[pl.BlockSpec((tr, d), lambda i: (i, 0)),
                  pl.BlockSpec((1, d), lambda i: (0, 0)),
                  pl.BlockSpec((LANES, d), lambda i: (0, 0))],
        out_specs=[pl.BlockSpec((tr, nc, LANES), lambda i: (i, 0, 0)),
                   pl.BlockSpec((N_EXPERTS, tr), lambda i: (0, i))],
        out_shape=[jax.ShapeDtypeStruct((lp, nc, LANES), F32), jax.ShapeDtypeStruct((N_EXPERTS, lp), F32)],
        compiler_params=_params(1),
        name="router",
    )(h2, norm2_w.reshape(1, d).astype(F32), wr)


def _select_kernel(afft_ref, flat_ref, sel_ref, rank_ref, *, cap, cap_pad, lp):
    aff = afft_ref[...]
    capf = float(cap)

    def count_ge(thr):
        return jnp.sum(jnp.where(aff >= thr, 1.0, 0.0), axis=1, keepdims=True)

    def bisect(_, carry):
        lo, hi = carry
        mid = 0.5 * (lo + hi)
        ge = count_ge(mid) >= capf
        return jnp.where(ge, mid, lo), jnp.where(ge, hi, mid)

    lo0 = jnp.zeros((N_EXPERTS, 1), F32)
    hi0 = jnp.full((N_EXPERTS, 1), 2.0, F32)
    _, hi = lax.fori_loop(0, 40, bisect, (lo0, hi0))

    def refine(st):
        hi, tau, done, _ = st
        cand = jnp.max(jnp.where(aff < hi, aff, -2.0), axis=1, keepdims=True)
        found = jnp.where(count_ge(cand) >= capf, 1.0, 0.0)
        tau = jnp.where(done > 0.0, tau, cand)
        hi = jnp.where(done + found > 0.0, hi, cand)
        done = jnp.maximum(done, found)
        return hi, tau, done, jnp.sum(1.0 - done)

    zero = jnp.zeros((N_EXPERTS, 1), F32)
    _, tau, _, _ = lax.while_loop(lambda st: st[3] > 0.0, refine, (hi, zero, zero, jnp.float32(N_EXPERTS)))
    need = capf - jnp.sum(jnp.where(aff > tau, 1.0, 0.0), axis=1, keepdims=True)

    ra = lax.broadcasted_iota(jnp.int32, (LANES, LANES), 0)
    ca = lax.broadcasted_iota(jnp.int32, (LANES, LANES), 1)
    ut = jnp.where(ra <= ca, 1.0, 0.0).astype(BF16)
    carry_eq = jnp.zeros((N_EXPERTS, 1), F32)
    carry_sel = jnp.zeros((N_EXPERTS, 1), F32)
    for j in range(lp // LANES):
        sl = slice(LANES * j, LANES * (j + 1))
        bj = aff[:, sl]
        eqf = jnp.where(bj == tau, 1.0, 0.0)
        tie_rank = _dot(eqf.astype(BF16), ut) + carry_eq - eqf
        carry_eq = carry_eq + jnp.sum(eqf, axis=1, keepdims=True)
        self = jnp.where((bj > tau) | ((bj == tau) & (tie_rank < need)), 1.0, 0.0)
        rank_ref[:, sl] = _dot(self.astype(BF16), ut) + carry_sel - self
        carry_sel = carry_sel + jnp.sum(self, axis=1, keepdims=True)
        sel_ref[:, sl] = self

    kio = lax.broadcasted_iota(jnp.int32, (N_EXPERTS, 1), 0).astype(F32)
    running = jnp.zeros((1, lp), F32)
    flat = jnp.full((N_EXPERTS, lp), -1.0, F32)
    for e in range(N_EXPERTS):
        se = sel_ref[e:e + 1, :]
        val = float(e * cap_pad) + rank_ref[e:e + 1, :]
        flat = jnp.where((se > 0.0) & (kio == running), val, flat)
        running = running + se
    flat_ref[...] = flat.astype(jnp.int32)


def _select(afft, cap, cap_pad):
    lp = afft.shape[1]
    return pl.pallas_call(
        functools.partial(_select_kernel, cap=cap, cap_pad=cap_pad, lp=lp),
        out_shape=jax.ShapeDtypeStruct((N_EXPERTS, lp), jnp.int32),
        scratch_shapes=[pltpu.VMEM((N_EXPERTS, lp), F32), pltpu.VMEM((N_EXPERTS, lp), F32)],
        compiler_params=pltpu.CompilerParams(vmem_limit_bytes=VMEM_LIMIT),
        name="expert_select",
    )(afft)


def _slab_copy(src, s_tok, dst, d_tok, sem, *, nc):
    s0 = pl.multiple_of(s_tok * nc, 8)
    d0 = pl.multiple_of(d_tok * nc, 8)
    return pltpu.make_async_copy(src.at[pl.ds(s0, nc)], dst.at[pl.ds(d0, nc)], sem)


def _dispatch_kernel(xg_ref, flat_ref, xe_ref, flat_smem, zero_ref, sem, csem, *, tt, cap, cap_pad, nc):
    _row_copy = functools.partial(_slab_copy, nc=nc)
    i = pl.program_id(0)
    cp = pltpu.make_async_copy(flat_ref, flat_smem, csem)
    cp.start()

    @pl.when(i == 0)
    def _():
        zero_ref[...] = jnp.zeros_like(zero_ref)
        for e in range(N_EXPERTS):
            for s in range(cap, cap_pad):
                _row_copy(zero_ref, 0, xe_ref, e * cap_pad + s, sem).start()
        for e in range(N_EXPERTS):
            for s in range(cap, cap_pad):
                _row_copy(zero_ref, 0, xe_ref, e * cap_pad + s, sem).wait()

    cp.wait()

    def per_token(t, n):
        def cond(st):
            k, _ = st
            return (k < N_EXPERTS) & (flat_smem[jnp.minimum(k, N_EXPERTS - 1), t] >= 0)

        def body(st):
            k, m = st
            _row_copy(xg_ref, t, xe_ref, flat_smem[k, t], sem).start()
            return k + 1, m + 1

        _, n = lax.while_loop(cond, body, (jnp.int32(0), n))
        return n

    n = lax.fori_loop(0, tt, per_token, jnp.int32(0))

    def drain(_, c):
        _row_copy(xg_ref, 0, xe_ref, 0, sem).wait()
        return c

    lax.fori_loop(0, n, drain, 0)


def _dispatch(xg, flat, cap, cap_pad):
    lp = flat.shape[1]
    nc = xg.shape[0] // lp
    tt = TOKEN_TILE
    return pl.pallas_call(
        functools.partial(_dispatch_kernel, tt=tt, cap=cap, cap_pad=cap_pad, nc=nc),
        grid=(lp // tt,),
        in_specs=[pl.BlockSpec((tt * nc, LANES), lambda i: (i, 0)),
                  pl.BlockSpec((N_EXPERTS, tt), lambda i: (0, i))],
        out_specs=pl.BlockSpec(memory_space=pl.ANY),
        out_shape=jax.ShapeDtypeStruct((N_EXPERTS * cap_pad * nc, LANES), F32),
        scratch_shapes=[pltpu.SMEM((N_EXPERTS, tt), jnp.int32), pltpu.VMEM((nc, LANES), F32),
                        pltpu.SemaphoreType.DMA, pltpu.SemaphoreType.DMA],
        compiler_params=pltpu.CompilerParams(dimension_semantics=("arbitrary",), vmem_limit_bytes=VMEM_LIMIT,
                                             has_side_effects=True),
        name="expert_dispatch",
    )(xg, flat)


def _ffn_a_kernel(x_ref, wg_ref, wu_ref, o_ref, aff_ref, x16_ref, *, cap_pad, nc, nx):
    @pl.when(pl.program_id(1) == 0)
    def _():
        for c in range(nx):
            x16_ref[:, LANES * c:LANES * (c + 1)] = x_ref[pl.ds(c, cap_pad, stride=nc), :].astype(BF16)
        aff_ref[...] = x_ref[pl.ds(nx, cap_pad, stride=nc), :]

    x16 = x16_ref[...]
    g = _dot(x16, wg_ref[0].astype(BF16))
    u = _dot(x16, wu_ref[0].astype(BF16))
    o_ref[...] = (g * _sigmoid(g) * u).astype(o_ref.dtype)


def _ffn_a(xe, w_gate, w_up, cap_pad, d_model):
    _, _, ff = w_gate.shape
    tf = _pick(ff, 256, LANES)
    nx = d_model // LANES
    nc = nx + ROW_PAD
    return pl.pallas_call(
        functools.partial(_ffn_a_kernel, cap_pad=cap_pad, nc=nc, nx=nx),
        grid=(N_EXPERTS, ff // tf),
        in_specs=[pl.BlockSpec((cap_pad * nc, LANES), lambda e, f: (e, 0), pipeline_mode=pl.Buffered(1)),
                  pl.BlockSpec((1, d_model, tf), lambda e, f: (e, 0, f)),
                  pl.BlockSpec((1, d_model, tf), lambda e, f: (e, 0, f))],
        out_specs=[pl.BlockSpec((cap_pad, tf), lambda e, f: (e, f)),
                   pl.BlockSpec((cap_pad, LANES), lambda e, f: (e, 0))],
        out_shape=[jax.ShapeDtypeStruct((N_EXPERTS * cap_pad, ff), BF16),
                   jax.ShapeDtypeStruct((N_EXPERTS * cap_pad, LANES), F32)],
        scratch_shapes=[pltpu.VMEM((cap_pad, d_model), BF16)],
        compiler_params=_params(2),
        name="expert_ffn_in",
    )(xe, w_gate, w_up)


def _ffn_b_kernel(h_ref, wd_ref, aff_ref, o_ref):
    e = pl.program_id(0)
    y = _dot(h_ref[...], wd_ref[0].astype(BF16))
    lane = lax.broadcasted_iota(jnp.int32, (1, LANES), 1)
    gate = jnp.sum(jnp.where(lane == e, aff_ref[...], 0.0), axis=-1, keepdims=True)
    y = y * gate
    for c in range(y.shape[1] // LANES):
        o_ref[:, c, :] = y[:, LANES * c:LANES * (c + 1)]


def _ffn_b(hid, w_down, aff, cap_pad, d_model):
    _, ff, _ = w_down.shape
    td = _pick(d_model, 1024, 8 * LANES) if d_model % (8 * LANES) == 0 else d_model
    return pl.pallas_call(
        _ffn_b_kernel,
        grid=(N_EXPERTS, d_model // td),
        in_specs=[pl.BlockSpec((cap_pad, ff), lambda e, n: (e, 0)),
                  pl.BlockSpec((1, ff, td), lambda e, n: (e, 0, n)),
                  pl.BlockSpec((cap_pad, LANES), lambda e, n: (e, 0))],
        out_specs=pl.BlockSpec((cap_pad, td // LANES, LANES), lambda e, n: (e, n, 0)),
        out_shape=jax.ShapeDtypeStruct((N_EXPERTS * cap_pad, d_model // LANES, LANES), F32),
        compiler_params=_params(2),
        name="expert_ffn_out",
    )(hid, w_down, aff)


def _combine_kernel(flat_ref, h_ref, ye_ref, nw_ref, o_ref, flat_smem, acc_ref, stage_ref, mask_ref,
                    sem, csem, hsem, *, tt, nx):
    _row_copy = functools.partial(_slab_copy, nc=nx)
    i = pl.program_id(0)
    hc = pltpu.make_async_copy(h_ref.at[pl.ds(pl.multiple_of(N_META + i * tt, 8), tt)], acc_ref, hsem)
    hc.start()
    cp = pltpu.make_async_copy(flat_ref, flat_smem, csem)
    cp.start()

    @pl.when(i == 0)
    def _():
        stage_ref[...] = jnp.zeros_like(stage_ref)

    cp.wait()
    hc.wait()
    ones_row = jnp.ones((1, LANES), F32)

    def level(st):
        k, _ = st

        def per_token(t, n):
            f = flat_smem[k, t]
            hit = f >= 0
            mask_ref[pl.ds(t, 1), :] = ones_row * hit.astype(F32)

            @pl.when(hit)
            def _():
                _row_copy(ye_ref, f, stage_ref, t, sem).start()

            return n + hit.astype(jnp.int32)

        n = lax.fori_loop(0, tt, per_token, jnp.int32(0))

        def drain(_, c):
            _row_copy(ye_ref, 0, stage_ref, 0, sem).wait()
            return c

        lax.fori_loop(0, n, drain, 0)
        hit_rows = mask_ref[...] > 0.0
        for c in range(nx):
            cols = slice(LANES * c, LANES * (c + 1))
            acc_ref[:, cols] += jnp.where(hit_rows, stage_ref[pl.ds(c, tt, stride=nx), :], 0.0)
        return k + 1, n

    lax.while_loop(lambda st: (st[0] < N_EXPERTS) & (st[1] > 0), level, (jnp.int32(0), jnp.int32(1)))
    x = acc_ref[...]
    o_ref[...] = x * lax.rsqrt(jnp.mean(x * x, axis=-1, keepdims=True) + NORM_EPS) * nw_ref[...]


def _combine(flat_real, h2, ye, norm_f_w, n_real):
    d = h2.shape[1]
    nx = d // LANES
    tt = _pick(n_real, TOKEN_TILE, LANES)
    return pl.pallas_call(
        functools.partial(_combine_kernel, tt=tt, nx=nx),
        grid=(n_real // tt,),
        in_specs=[pl.BlockSpec((N_EXPERTS, tt), lambda i: (0, i)),
                  pl.BlockSpec(memory_space=pl.ANY),
                  pl.BlockSpec(memory_space=pl.ANY),
                  pl.BlockSpec((1, d), lambda i: (0, 0))],
        out_specs=pl.BlockSpec((tt, d), lambda i: (i, 0)),
        out_shape=jax.ShapeDtypeStruct((n_real, d), F32),
        scratch_shapes=[pltpu.SMEM((N_EXPERTS, tt), jnp.int32), pltpu.VMEM((tt, d), F32),
                        pltpu.VMEM((tt * nx, LANES), F32), pltpu.VMEM((tt, LANES), F32),
                        pltpu.SemaphoreType.DMA, pltpu.SemaphoreType.DMA, pltpu.SemaphoreType.DMA],
        compiler_params=_params(1),
        name="expert_combine",
    )(flat_real, h2, ye, norm_f_w.reshape(1, d).astype(F32))


def _layer(hp, seq_len, norm1_w, w_in, conv_w, a_log_fwd, a_log_bwd, dt_bias_fwd, dt_bias_bwd, out_norm_w,
           w_branch_a, attn_sink, w_branch_b, w_out, norm2_w, w_router, w_gate, w_up, w_down):
    lp, d = hp.shape
    main_w = 2 * DN_QK + 2 * DN_VW
    gate_w = 4 * DN_HEADS
    rest_w = SWA_QW + 2 * SWA_KVW + 2 * d
    n = _rmsnorm(hp, norm1_w, BF16)
    w_t = jnp.swapaxes(w_in, 0, 1)
    proj_main = _matmul(n, w_t, main_w, 0, BF16, w_rows=True, name="in_proj_main")
    gates = _matmul(n, w_t, LANES, main_w, F32, w_rows=True, name="in_proj_gates")
    rest = _matmul(n, w_t, rest_w, main_w + gate_w, BF16, w_rows=True, name="in_proj_rest")

    prm = jnp.zeros((8, LANES), F32)
    prm = prm.at[0, 2 * DN_HEADS:4 * DN_HEADS].set(jnp.concatenate([a_log_fwd, a_log_bwd]).astype(F32))
    prm = prm.at[1, 2 * DN_HEADS:4 * DN_HEADS].set(jnp.concatenate([dt_bias_fwd, dt_bias_bwd]).astype(F32))
    conv_f = conv_w.astype(F32)
    o_f = _deltanet(proj_main, conv_f, gates, prm, seq_len, backward=False)
    o_r = _deltanet(proj_main, conv_f, gates, prm, seq_len, backward=True)
    o_a = _gated_norm(o_f, o_r, proj_main, out_norm_w)
    o_b = _window_attention(rest, attn_sink, seq_len)

    mixed = _branch_merge(o_a, o_b, w_branch_a, w_branch_b, rest, d)
    h2 = _matmul(mixed, w_out, d, 0, F32, res=hp, name="out_proj")

    cap = EC_CAPACITY * seq_len // N_EXPERTS
    cap_pad = -(-cap // 16) * 16
    xg, afft = _router(h2, norm2_w, w_router, seq_len)
    flat = _select(afft, cap, cap_pad)
    xe = _dispatch(xg.reshape(-1, LANES), flat, cap, cap_pad)
    hid, aff = _ffn_a(xe, w_gate, w_up, cap_pad, d)
    ye = _ffn_b(hid, w_down, aff, cap_pad, d)
    return h2, ye.reshape(-1, LANES), flat


def kernel(x, meta_tokens, norm1_w, w_in, conv_w, a_log_fwd, a_log_bwd, dt_bias_fwd, dt_bias_bwd, out_norm_w,
           w_branch_a, attn_sink, w_branch_b, w_out, norm2_w, w_router, w_gate, w_up, w_down, norm_f_w):
    batch, seq, d = x.shape
    depth = norm1_w.shape[0]
    assert depth == 1, "the final norm is fused into the last layer's expert combine"
    seq_len = N_META + seq
    lp = -(-seq_len // TOKEN_TILE) * TOKEN_TILE
    outs = []
    for b in range(batch):
        h = jnp.concatenate([meta_tokens.astype(x.dtype), x[b]], axis=0)
        hp = jnp.pad(h, ((0, lp - seq_len), (0, 0)))
        h2, ye, flat = _layer(hp, seq_len, norm1_w[0], w_in[0], conv_w[0], a_log_fwd[0], a_log_bwd[0],
                              dt_bias_fwd[0], dt_bias_bwd[0], out_norm_w[0], w_branch_a[0], attn_sink[0],
                              w_branch_b[0], w_out[0], norm2_w[0], w_router[0], w_gate[0], w_up[0], w_down[0])
        outs.append(_combine(flat[:, N_META:seq_len], h2, ye, norm_f_w, seq))
    return jnp.stack(outs, axis=0)
```

```python
import functools

import jax
import jax.numpy as jnp
from jax import lax
from jax.experimental import pallas as pl
from jax.experimental.pallas import tpu as pltpu

F32 = jnp.float32
BF16 = jnp.bfloat16

N_META = 16
NORM_EPS = 1e-6
DN_HEADS = 16
DN_DK = 128
DN_DV = 128
SWA_HQ = 16
SWA_HKV = 4
SWA_D = 128
SWA_WINDOW = 128
N_EXPERTS = 16
EC_CAPACITY = 2

LANES = 128
TOKEN_TILE = 256
DN_CHUNK = 128
DN_HEADS_PER_STEP = 8
ATTN_TILE = 128
ROW_PAD = 8
DMA_ISSUE_UNROLL = 6
COMBINE_CHUNK = 8
VMEM_LIMIT = 56 * 1024 * 1024

DN_QK = DN_HEADS * DN_DK
DN_VW = DN_HEADS * DN_DV
SWA_QW = SWA_HQ * SWA_D
SWA_KVW = SWA_HKV * SWA_D


def _params(n_grid):
    return pltpu.CompilerParams(dimension_semantics=("arbitrary",) * n_grid, vmem_limit_bytes=VMEM_LIMIT)


def _pick(n, target, mult):
    best = None
    for t in range(mult, min(n, target) + 1, mult):
        if n % t == 0:
            best = t
    assert best is not None, (n, target, mult)
    return best


def _sigmoid(x):
    return 1.0 / (1.0 + jnp.exp(-x))


def _dot(a, b):
    return jnp.dot(a, b, preferred_element_type=F32)


def _dot_nt(a, b):
    return lax.dot_general(a, b, (((1,), (1,)), ((), ())), preferred_element_type=F32)


def _rms_kernel(x_ref, w_ref, o_ref):
    x = x_ref[...]
    ms = jnp.mean(x * x, axis=-1, keepdims=True)
    o_ref[...] = (x * lax.rsqrt(ms + NORM_EPS) * w_ref[...]).astype(o_ref.dtype)


def _rmsnorm(x, w, out_dtype):
    m, d = x.shape
    tr = _pick(m, 256, 16)
    return pl.pallas_call(
        _rms_kernel,
        grid=(m // tr,),
        in_specs=[pl.BlockSpec((tr, d), lambda i: (i, 0)), pl.BlockSpec((1, d), lambda i: (0, 0))],
        out_specs=pl.BlockSpec((tr, d), lambda i: (i, 0)),
        out_shape=jax.ShapeDtypeStruct((m, d), out_dtype),
        compiler_params=_params(1),
        name="rmsnorm",
    )(x, w.reshape(1, d).astype(F32))


def _mm_kernel(*refs, has_res, w_rows):
    if has_res:
        a_ref, w_ref, r_ref, o_ref, wb_ref = refs
    else:
        a_ref, w_ref, o_ref, wb_ref = refs

    @pl.when(pl.program_id(1) == 0)
    def _():
        w = w_ref[...]
        wb_ref[...] = (jnp.transpose(w) if w_rows else w).astype(BF16)

    acc = _dot(a_ref[...], wb_ref[...])
    if has_res:
        acc = acc + r_ref[...]
    o_ref[...] = acc.astype(o_ref.dtype)


def _matmul(a, w, n_cols, off, out_dtype, res=None, w_rows=False, name="matmul"):
    m, k = a.shape
    tn = _pick(n_cols, 512, LANES)
    tm = _pick(m, 1408 if res is None else 768, 16)
    if w_rows:
        if off % tn == 0:
            w_spec = pl.BlockSpec((tn, k), lambda n, i: (off // tn + n, 0))
        else:
            assert off % 8 == 0
            w_spec = pl.BlockSpec((pl.Element(tn), pl.Element(k)), lambda n, i: (pl.multiple_of(off + n * tn, 8), 0))
    else:
        assert off % tn == 0
        w_spec = pl.BlockSpec((k, tn), lambda n, i: (0, off // tn + n))
    in_specs = [pl.BlockSpec((tm, k), lambda n, i: (i, 0)), w_spec]
    args = [a, w]
    if res is not None:
        in_specs.append(pl.BlockSpec((tm, tn), lambda n, i: (i, n)))
        args.append(res)
    return pl.pallas_call(
        functools.partial(_mm_kernel, has_res=res is not None, w_rows=w_rows),
        grid=(n_cols // tn, m // tm),
        in_specs=in_specs,
        out_specs=pl.BlockSpec((tm, tn), lambda n, i: (i, n)),
        out_shape=jax.ShapeDtypeStruct((m, n_cols), out_dtype),
        scratch_shapes=[pltpu.VMEM((k, tn), BF16)],
        compiler_params=_params(2),
        name=name,
    )(*args)


def _dn_kernel(q_ref, qp_ref, qn_ref, k_ref, kp_ref, kn_ref, v_ref, vp_ref, vn_ref, cq_ref, ck_ref, cv_ref,
               sm_ref, prm_ref, o_ref, s_ref, *, seq_len, tl, backward):
    hg = pl.program_id(0)
    i = pl.program_id(1)
    nt = pl.num_programs(1)
    ti = nt - 1 - i if backward else i
    cs = DN_CHUNK
    cs_shift = cs.bit_length() - 1
    hpb = DN_HEADS_PER_STEP

    @pl.when(i == 0)
    def _():
        s_ref[...] = jnp.zeros_like(s_ref)

    row = lax.broadcasted_iota(jnp.int32, (tl, 1), 0)
    validf = ((ti * tl + row) < seq_len).astype(F32)
    has_prev = (ti > 0).astype(F32)
    has_next = (ti < nt - 1).astype(F32)

    def conv_silu(x_ref, p_ref, n_ref, c_ref):
        x = x_ref[...].astype(F32)
        prev = p_ref[...].astype(F32)[15:16, :] * has_prev
        nxt = n_ref[...].astype(F32)[0:1, :] * has_next
        xm1 = jnp.where(row == 0, prev, pltpu.roll(x, 1, 0))
        xp1 = jnp.where(row == tl - 1, nxt, pltpu.roll(x, tl - 1, 0))
        c = c_ref[...]
        y = xm1 * c[0:1] + x * c[1:2] + xp1 * c[2:3]
        return y * _sigmoid(y)

    q_all = conv_silu(q_ref, qp_ref, qn_ref, cq_ref)
    k_all = conv_silu(k_ref, kp_ref, kn_ref, ck_ref)
    v_all = conv_silu(v_ref, vp_ref, vn_ref, cv_ref)

    lane = lax.broadcasted_iota(jnp.int32, (1, LANES), 1)
    sm = sm_ref[...]
    prm = prm_ref[...]
    z = sm + prm[1:2]
    softplus = jnp.maximum(z, 0.0) + jnp.log1p(jnp.exp(-jnp.abs(z)))
    beta_all = _sigmoid(sm) * validf
    g_all = -jnp.exp(prm[0:1]) * softplus * validf

    r2 = lax.broadcasted_iota(jnp.int32, (tl, tl), 0)
    c2 = lax.broadcasted_iota(jnp.int32, (tl, tl), 1)
    incl2 = ((r2 >> cs_shift) == (c2 >> cs_shift)) & ((r2 <= c2) if backward else (r2 >= c2))
    tri = jnp.where(incl2, 1.0, 0.0).astype(BF16)
    g1 = g_all.astype(BF16)
    rem = g_all - g1.astype(F32)
    g2 = rem.astype(BF16)
    g3 = (rem - g2.astype(F32)).astype(BF16)
    gc_all = _dot(tri, g1) + _dot(tri, g2) + _dot(tri, g3)

    r = lax.broadcasted_iota(jnp.int32, (cs, cs), 0)
    c = lax.broadcasted_iota(jnp.int32, (cs, cs), 1)
    incl = (r <= c) if backward else (r >= c)
    strict = (r < c) if backward else (r > c)
    levels = []
    b = 1
    while b < cs:
        sh = b.bit_length() - 1
        levels.append(((r >> (sh + 1)) == (c >> (sh + 1))) & ((r >> sh) != (c >> sh)))
        b *= 2

    chunk_order = range(tl // cs - 1, -1, -1) if backward else range(tl // cs)
    streams = []
    for j in range(hpb):
        hs = slice(DN_DK * j, DN_DK * (j + 1))
        q = q_all[:, hs]
        q = q * lax.rsqrt(jnp.sum(q * q, axis=-1, keepdims=True) + NORM_EPS) * (DN_DK ** -0.5) * validf
        k = k_all[:, hs]
        k = k * lax.rsqrt(jnp.sum(k * k, axis=-1, keepdims=True) + NORM_EPS) * validf
        v = v_all[:, hs] * validf
        col = (DN_HEADS if backward else 0) + hg * hpb + j
        beta = jnp.sum(jnp.where(lane == col, beta_all, 0.0), axis=-1, keepdims=True)
        g = jnp.sum(jnp.where(lane == 2 * DN_HEADS + col, g_all, 0.0), axis=-1, keepdims=True)
        gcum = jnp.sum(jnp.where(lane == 2 * DN_HEADS + col, gc_all, 0.0), axis=-1, keepdims=True)
        for ci in chunk_order:
            rs = slice(ci * cs, (ci + 1) * cs)
            qc, kc, vc, bc = q[rs], k[rs], v[rs], beta[rs]
            gtot = jnp.sum(g[rs], axis=0, keepdims=True)
            gb = jnp.broadcast_to(gcum[rs], (cs, LANES))
            gj = jnp.transpose(gb)[0:1, :]
            decay = jnp.where(incl, jnp.exp(jnp.where(incl, gb[:, 0:1] - gj, 0.0)), 0.0)
            kb = kc * bc
            k16 = kc.astype(BF16)
            egc = jnp.exp(gb)
            streams.append(dict(
                j=j, rs=rs, hs=hs, gtot=gtot,
                m=jnp.where(strict, _dot_nt(kb.astype(BF16), k16) * decay, 0.0),
                qk=(_dot_nt(qc.astype(BF16), k16) * decay).astype(BF16),
                rhs=jnp.concatenate([vc * bc, kb * egc], axis=1),
                q_dec=qc * egc,
                k_dec_t=jnp.transpose(kc * jnp.exp(gtot - gb)).astype(BF16)))

    for st in streams:
        st["e"] = -jnp.where(levels[0], st["m"], 0.0)
    for mask in levels[1:]:
        for st in streams:
            lb = jnp.where(mask, st["m"], 0.0)
            st["lb"] = lb
            st["y"] = lb + _dot(lb.astype(BF16), st["e"].astype(BF16))
        for st in streams:
            st["e"] = st["e"] - st["y"] - _dot(st["e"].astype(BF16), st["y"].astype(BF16))
    for st in streams:
        st["uw"] = st["rhs"] + _dot(st["e"].astype(BF16), st["rhs"].astype(BF16))

    state = [s_ref[j] for j in range(hpb)]
    for step in range(tl // cs):
        cur = [streams[j * (tl // cs) + step] for j in range(hpb)]
        for st in cur:
            wq = jnp.concatenate([st["uw"][:, DN_DV:], st["q_dec"]], axis=0).astype(BF16)
            st["wqs"] = _dot(wq, state[st["j"]].astype(BF16))
        for st in cur:
            v16 = (st["uw"][:, :DN_DV] - st["wqs"][:cs]).astype(BF16)
            o_ref[st["rs"], st["hs"]] = st["wqs"][cs:] + _dot(st["qk"], v16)
            state[st["j"]] = state[st["j"]] * jnp.exp(st["gtot"]) + _dot(st["k_dec_t"], v16)
    for j in range(hpb):
        s_ref[j] = state[j]


def _deltanet(proj_main, conv_w, gates, prm, seq_len, backward):
    lp = proj_main.shape[0]
    tl = TOKEN_TILE
    nt = lp // tl
    hb = tl // 16
    nhb = lp // 16
    hpb = DN_HEADS_PER_STEP
    wb = hpb * DN_DK
    ngroups = DN_HEADS // hpb

    def tile(i):
        return nt - 1 - i if backward else i

    def main_spec(off):
        return pl.BlockSpec((tl, wb), lambda h, i: (tile(i), off + h))

    def prev_spec(off):
        return pl.BlockSpec((16, wb), lambda h, i: (jnp.maximum(tile(i) * hb - 1, 0), off + h))

    def next_spec(off):
        return pl.BlockSpec((16, wb), lambda h, i: (jnp.minimum((tile(i) + 1) * hb, nhb - 1), off + h))

    def conv_spec(off):
        return pl.BlockSpec((3, wb), lambda h, i: (0, off + h))

    in_specs = []
    args = []
    for off in (0, ngroups, 2 * ngroups):
        in_specs += [main_spec(off), prev_spec(off), next_spec(off)]
        args += [proj_main] * 3
    in_specs += [conv_spec(0), conv_spec(ngroups), conv_spec(2 * ngroups)]
    args += [conv_w] * 3
    in_specs += [pl.BlockSpec((tl, LANES), lambda h, i: (tile(i), 0)),
                 pl.BlockSpec((8, LANES), lambda h, i: (0, 0))]
    args += [gates, prm]
    return pl.pallas_call(
        functools.partial(_dn_kernel, seq_len=seq_len, tl=tl, backward=backward),
        grid=(ngroups, nt),
        in_specs=in_specs,
        out_specs=pl.BlockSpec((tl, wb), lambda h, i: (tile(i), h)),
        out_shape=jax.ShapeDtypeStruct((lp, DN_VW), F32),
        scratch_shapes=[pltpu.VMEM((hpb, DN_DK, DN_DV), F32)],
        compiler_params=_params(2),
        name="deltanet_bwd" if backward else "deltanet_fwd",
    )(*args)


def _gnorm_kernel(of_ref, ob_ref, z_ref, w_ref, o_ref, *, nh):
    for j in range(nh):
        sl = slice(DN_DV * j, DN_DV * (j + 1))
        o = of_ref[:, sl] + ob_ref[:, sl]
        o = o * lax.rsqrt(jnp.mean(o * o, axis=-1, keepdims=True) + NORM_EPS) * w_ref[...]
        z = z_ref[:, sl].astype(F32)
        o_ref[:, sl] = (o * (z * _sigmoid(z))).astype(o_ref.dtype)


def _gated_norm(o_f, o_b, proj_main, out_norm_w):
    lp = proj_main.shape[0]
    nh = 4
    wb = nh * DN_DV
    tr = _pick(lp, 768, 16)
    zoff = (2 * DN_QK + DN_VW) // wb
    return pl.pallas_call(
        functools.partial(_gnorm_kernel, nh=nh),
        grid=(lp // tr, DN_VW // wb),
        in_specs=[pl.BlockSpec((tr, wb), lambda i, j: (i, j)),
                  pl.BlockSpec((tr, wb), lambda i, j: (i, j)),
                  pl.BlockSpec((tr, wb), lambda i, j: (i, zoff + j)),
                  pl.BlockSpec((1, DN_DV), lambda i, j: (0, 0))],
        out_specs=pl.BlockSpec((tr, wb), lambda i, j: (i, j)),
        out_shape=jax.ShapeDtypeStruct((lp, DN_VW), BF16),
        compiler_params=_params(2),
        name="gated_norm",
    )(o_f, o_b, proj_main, out_norm_w.reshape(1, DN_DV).astype(F32))


def _attn_kernel(q_ref, k0_ref, kp_ref, kc_ref, kn_ref, v0_ref, vp_ref, vc_ref, vn_ref, prm_ref, o_ref, *,
                 seq_len, tq):
    i = pl.program_id(1)
    groups = SWA_HQ // SWA_HKV
    kall = jnp.concatenate([k0_ref[...], kp_ref[...], kc_ref[...], kn_ref[...]], axis=0)
    vall = jnp.concatenate([v0_ref[...], vp_ref[...], vc_ref[...], vn_ref[...]], axis=0)
    pq = i * tq + lax.broadcasted_iota(jnp.int32, (tq, 1), 0)
    ck = lax.broadcasted_iota(jnp.int32, (1, 4 * tq), 1)
    meta_part = ck < tq
    pk = (i - 1) * tq + (ck - tq)
    real_q = pq >= N_META
    dist = pq - pk
    adist = jnp.abs(dist)
    win_part = jnp.logical_not(meta_part)
    in_win = (real_q & (adist <= SWA_WINDOW)) | (jnp.logical_not(real_q) & (-dist <= SWA_WINDOW))
    win_ok = win_part & (pk >= N_META) & (pk < seq_len) & in_win
    ok = (meta_part & (ck < N_META)) | win_ok
    bias = jnp.where(ok, jnp.where(real_q & win_part, -adist.astype(F32), 0.0), -jnp.inf)
    prm = prm_ref[0]
    scale = SWA_D ** -0.5
    for g in range(groups):
        sl = slice(SWA_D * g, SWA_D * (g + 1))
        slope = prm[g:g + 1, 0:1]
        sink = prm[groups + g:groups + g + 1, 0:1]
        s = _dot_nt(q_ref[:, sl], kall) * scale + slope * bias
        m = jnp.maximum(jnp.max(s, axis=-1, keepdims=True), sink)
        p = jnp.exp(s - m)
        den = jnp.sum(p, axis=-1, keepdims=True) + jnp.exp(sink - m)
        o_ref[:, sl] = (_dot(p.astype(BF16), vall) / den).astype(o_ref.dtype)


def _window_attention(rest, attn_sink, seq_len):
    lp = rest.shape[0]
    tq = ATTN_TILE
    nq = lp // tq
    groups = SWA_HQ // SWA_HKV
    koff = SWA_QW // SWA_D
    voff = koff + SWA_HKV
    slopes = 2.0 ** (-8.0 * jnp.arange(1, SWA_HQ + 1, dtype=F32) / SWA_HQ)
    prm = jnp.concatenate([slopes.reshape(SWA_HKV, groups), attn_sink.astype(F32).reshape(SWA_HKV, groups)], axis=1)
    prm = jnp.broadcast_to(prm[:, :, None], (SWA_HKV, 2 * groups, LANES))

    def kv_specs(off):
        return [pl.BlockSpec((tq, SWA_D), lambda h, i: (0, off + h)),
                pl.BlockSpec((tq, SWA_D), lambda h, i: (jnp.maximum(i - 1, 0), off + h)),
                pl.BlockSpec((tq, SWA_D), lambda h, i: (i, off + h)),
                pl.BlockSpec((tq, SWA_D), lambda h, i: (jnp.minimum(i + 1, nq - 1), off + h))]

    qw = groups * SWA_D
    return pl.pallas_call(
        functools.partial(_attn_kernel, seq_len=seq_len, tq=tq),
        grid=(SWA_HKV, nq),
        in_specs=[pl.BlockSpec((tq, qw), lambda h, i: (i, h))] + kv_specs(koff) + kv_specs(voff)
        + [pl.BlockSpec((1, 2 * groups, LANES), lambda h, i: (h, 0, 0))],
        out_specs=pl.BlockSpec((tq, qw), lambda h, i: (i, h)),
        out_shape=jax.ShapeDtypeStruct((lp, SWA_QW), BF16),
        compiler_params=_params(2),
        name="window_attention",
    )(*([rest] * 9), prm)


def _branch_kernel(oa_ref, ob_ref, wa_ref, wb_ref, ga_ref, gb_ref, o_ref, wa16_ref, wb16_ref):
    @pl.when(pl.program_id(1) == 0)
    def _():
        wa16_ref[...] = wa_ref[...].astype(BF16)
        wb16_ref[...] = wb_ref[...].astype(BF16)

    ya = _dot(oa_ref[...], wa16_ref[...])
    yb = _dot(ob_ref[...], wb16_ref[...])
    o = _sigmoid(ga_ref[...].astype(F32)) * ya + _sigmoid(gb_ref[...].astype(F32)) * yb
    o_ref[...] = o.astype(o_ref.dtype)


def _branch_merge(o_a, o_b, w_a, w_b, rest, d_model):
    lp = o_a.shape[0]
    tn = _pick(d_model, 512, LANES)
    tm = _pick(lp, 768, 16)
    ga_off = (SWA_QW + 2 * SWA_KVW) // tn
    gb_off = (SWA_QW + 2 * SWA_KVW + d_model) // tn
    assert (SWA_QW + 2 * SWA_KVW) % tn == 0 and d_model % tn == 0
    return pl.pallas_call(
        _branch_kernel,
        grid=(d_model // tn, lp // tm),
        in_specs=[pl.BlockSpec((tm, DN_VW), lambda n, i: (i, 0)),
                  pl.BlockSpec((tm, SWA_QW), lambda n, i: (i, 0)),
                  pl.BlockSpec((DN_VW, tn), lambda n, i: (0, n)),
                  pl.BlockSpec((SWA_QW, tn), lambda n, i: (0, n)),
                  pl.BlockSpec((tm, tn), lambda n, i: (i, ga_off + n)),
                  pl.BlockSpec((tm, tn), lambda n, i: (i, gb_off + n))],
        out_specs=pl.BlockSpec((tm, tn), lambda n, i: (i, n)),
        out_shape=jax.ShapeDtypeStruct((lp, d_model), BF16),
        scratch_shapes=[pltpu.VMEM((DN_VW, tn), BF16), pltpu.VMEM((SWA_QW, tn), BF16)],
        compiler_params=_params(2),
        name="branch_merge",
    )(o_a, o_b, w_a, w_b, rest, rest)


def _router_kernel(h_ref, w_ref, wr_ref, xg_ref, afft_ref, *, seq_len, tr, d_model):
    i = pl.program_id(0)
    x = h_ref[...]
    xn = x * lax.rsqrt(jnp.mean(x * x, axis=-1, keepdims=True) + NORM_EPS) * w_ref[...]
    logits = _dot_nt(xn.astype(BF16), wr_ref[...].astype(BF16))
    lane = lax.broadcasted_iota(jnp.int32, (1, LANES), 1)
    lm = jnp.where(lane < N_EXPERTS, logits, -jnp.inf)
    ex = jnp.exp(lm - jnp.max(lm, axis=-1, keepdims=True))
    aff = ex / jnp.sum(ex, axis=-1, keepdims=True)
    row = i * tr + lax.broadcasted_iota(jnp.int32, (tr, 1), 0)
    aff = jnp.where((row < seq_len) & (lane < N_EXPERTS), aff, -1.0)
    nx = d_model // LANES
    for c in range(nx):
        xg_ref[:, c, :] = xn[:, LANES * c:LANES * (c + 1)]
    xg_ref[:, nx, :] = aff
    xg_ref[:, nx + 1:, :] = jnp.zeros((tr, ROW_PAD - 1, LANES), F32)
    afft_ref[...] = jnp.transpose(aff)[:N_EXPERTS, :]


def _router(h2, norm2_w, w_router, seq_len):
    lp, d = h2.shape
    tr = TOKEN_TILE
    wr = jnp.zeros((LANES, d), F32).at[:N_EXPERTS].set(jnp.swapaxes(w_router, 0, 1).astype(F32))
    nc = d // LANES + ROW_PAD
    return pl.pallas_call(
        functools.partial(_router_kernel, seq_len=seq_len, tr=tr, d_model=d),
        grid=(lp // tr,),
        in_specs=[pl.BlockSpec((tr, d), lambda i: (i, 0)),
                  pl.BlockSpec((1, d), lambda i: (0, 0)),
                  pl.BlockSpec((LANES, d), lambda i: (0, 0))],
        out_specs=[pl.BlockSpec((tr, nc, LANES), lambda i: (i, 0, 0)),
                   pl.BlockSpec((N_EXPERTS, tr), lambda i: (0, i))],
        out_shape=[jax.ShapeDtypeStruct((lp, nc, LANES), F32), jax.ShapeDtypeStruct((N_EXPERTS, lp), F32)],
        compiler_params=_params(1),
        name="router",
    )(h2, norm2_w.reshape(1, d).astype(F32), wr)


def _select_kernel(afft_ref, idx_ref, lo_ref, sel_ref, csum_ref, cols_ref, *, cap, slot_pad, lp, tile, n_tiles):
    aff = afft_ref[...]
    capf = float(cap)

    def count_ge(thr):
        return jnp.sum(jnp.where(aff >= thr, 1.0, 0.0), axis=1, keepdims=True)

    def bisect(_, carry):
        lo, hi = carry
        mid = 0.5 * (lo + hi)
        ge = count_ge(mid) >= capf
        return jnp.where(ge, mid, lo), jnp.where(ge, hi, mid)

    lo0 = jnp.zeros((N_EXPERTS, 1), F32)
    hi0 = jnp.full((N_EXPERTS, 1), 2.0, F32)
    _, hi = lax.fori_loop(0, 40, bisect, (lo0, hi0))

    def refine(st):
        hi, tau, done, _ = st
        cand = jnp.max(jnp.where(aff < hi, aff, -2.0), axis=1, keepdims=True)
        found = jnp.where(count_ge(cand) >= capf, 1.0, 0.0)
        tau = jnp.where(done > 0.0, tau, cand)
        hi = jnp.where(done + found > 0.0, hi, cand)
        done = jnp.maximum(done, found)
        return hi, tau, done, jnp.sum(1.0 - done)

    zero = jnp.zeros((N_EXPERTS, 1), F32)
    _, tau, _, _ = lax.while_loop(lambda st: st[3] > 0.0, refine, (hi, zero, zero, jnp.float32(N_EXPERTS)))
    need = capf - jnp.sum(jnp.where(aff > tau, 1.0, 0.0), axis=1, keepdims=True)

    ra = lax.broadcasted_iota(jnp.int32, (LANES, LANES), 0)
    ca = lax.broadcasted_iota(jnp.int32, (LANES, LANES), 1)
    ut = jnp.where(ra <= ca, 1.0, 0.0).astype(BF16)
    carry_eq = jnp.zeros((N_EXPERTS, 1), F32)
    carry_sel = jnp.zeros((N_EXPERTS, 1), F32)
    for j in range(lp // LANES):
        sl = slice(LANES * j, LANES * (j + 1))
        bj = aff[:, sl]
        eqf = jnp.where(bj == tau, 1.0, 0.0)
        tie_rank = _dot(eqf.astype(BF16), ut) + carry_eq - eqf
        carry_eq = carry_eq + jnp.sum(eqf, axis=1, keepdims=True)
        self = jnp.where((bj > tau) | ((bj == tau) & (tie_rank < need)), 1.0, 0.0)
        csum_ref[j] = _dot(self.astype(BF16), ut) + carry_sel
        carry_sel = carry_sel + jnp.sum(self, axis=1, keepdims=True)
        sel_ref[:, sl] = self

    lane = lax.broadcasted_iota(jnp.int32, (1, LANES), 1)
    sblk = 64
    cbase = lax.broadcasted_iota(jnp.int32, (sblk, 1), 0)
    cols_ref[...] = jnp.zeros_like(cols_ref)

    def per_expert(e, carry):
        def per_block(sb, carry2):
            cvals = (sb * sblk + cbase).astype(F32)

            def per_tile(j, acc):
                return acc + jnp.where(csum_ref[j, pl.ds(e, 1), :] <= cvals, 1.0, 0.0)

            n_lane_tiles = lp // LANES
            acc = lax.fori_loop(0, n_lane_tiles, per_tile, jnp.zeros((sblk, LANES), F32),
                                unroll=6 if n_lane_tiles % 6 == 0 else 1)
            rows = pl.ds(pl.multiple_of(sb * sblk, sblk), sblk)
            cols_ref[rows, :] = jnp.where(lane == e, jnp.sum(acc, axis=1, keepdims=True), cols_ref[rows, :])
            return carry2

        return lax.fori_loop(0, slot_pad // sblk, per_block, carry)

    lax.fori_loop(0, N_EXPERTS, per_expert, 0)
    idx_ref[...] = jnp.transpose(cols_ref[...])[:N_EXPERTS, :].astype(jnp.int32)

    sel = sel_ref[...]
    tok = lax.broadcasted_iota(jnp.int32, (1, lp), 1)
    table = jnp.zeros((N_EXPERTS, LANES), F32)
    for i in range(n_tiles + 1):
        below = jnp.sum(jnp.where(tok < N_META + tile * i, sel, 0.0), axis=1, keepdims=True)
        table = jnp.where(lane == i, below, table)
    lo_ref[...] = table.astype(jnp.int32)


def _select(afft, cap, slot_pad, tile, n_tiles):
    lp = afft.shape[1]
    assert n_tiles < LANES
    return pl.pallas_call(
        functools.partial(_select_kernel, cap=cap, slot_pad=slot_pad, lp=lp, tile=tile, n_tiles=n_tiles),
        out_shape=[jax.ShapeDtypeStruct((N_EXPERTS, slot_pad), jnp.int32),
                   jax.ShapeDtypeStruct((N_EXPERTS, LANES), jnp.int32)],
        scratch_shapes=[pltpu.VMEM((N_EXPERTS, lp), F32), pltpu.VMEM((lp // LANES, N_EXPERTS, LANES), F32),
                        pltpu.VMEM((slot_pad, LANES), F32)],
        compiler_params=pltpu.CompilerParams(vmem_limit_bytes=VMEM_LIMIT),
        name="expert_select",
    )(afft)


def _slab_copy(src, s_tok, dst, d_tok, sem, *, nc):
    s0 = pl.multiple_of(s_tok * nc, 8)
    d0 = pl.multiple_of(d_tok * nc, 8)
    return pltpu.make_async_copy(src.at[pl.ds(s0, nc)], dst.at[pl.ds(d0, nc)], sem)


def _dispatch_kernel(idx_ref, xg_ref, xe_ref, zero_ref, sem, *, cap, cap_pad, nc):
    copy = functools.partial(_slab_copy, nc=nc)
    e = pl.program_id(0)
    base = e * cap_pad
    zero_ref[...] = jnp.zeros_like(zero_ref)
    for s in range(cap, cap_pad):
        copy(zero_ref, 0, xe_ref, base + s, sem).start()

    def issue(s, c):
        copy(xg_ref, idx_ref[e, s], xe_ref, base + s, sem).start()
        return c

    lax.fori_loop(0, cap, issue, 0, unroll=DMA_ISSUE_UNROLL if cap % DMA_ISSUE_UNROLL == 0 else 1)

    def drain(_, c):
        copy(zero_ref, 0, xe_ref, base, sem).wait()
        return c

    lax.fori_loop(0, cap_pad, drain, 0)


def _dispatch(xg, idx, cap, cap_pad, nc):
    return pl.pallas_call(
        functools.partial(_dispatch_kernel, cap=cap, cap_pad=cap_pad, nc=nc),
        grid_spec=pltpu.PrefetchScalarGridSpec(
            num_scalar_prefetch=1,
            grid=(N_EXPERTS,),
            in_specs=[pl.BlockSpec(memory_space=pl.ANY)],
            out_specs=pl.BlockSpec(memory_space=pl.ANY),
            scratch_shapes=[pltpu.VMEM((nc, LANES), F32), pltpu.SemaphoreType.DMA]),
        out_shape=jax.ShapeDtypeStruct((N_EXPERTS * cap_pad * nc, LANES), F32),
        compiler_params=pltpu.CompilerParams(dimension_semantics=("arbitrary",), vmem_limit_bytes=VMEM_LIMIT,
                                             has_side_effects=True),
        name="expert_dispatch",
    )(idx, xg)


def _ffn_a_kernel(x_ref, wg_ref, wu_ref, o_ref, aff_ref, x16_ref, *, cap_pad, nc, nx):
    @pl.when(pl.program_id(1) == 0)
    def _():
        for c in range(nx):
            x16_ref[:, LANES * c:LANES * (c + 1)] = x_ref[pl.ds(c, cap_pad, stride=nc), :].astype(BF16)
        aff_ref[...] = x_ref[pl.ds(nx, cap_pad, stride=nc), :]

    x16 = x16_ref[...]
    g = _dot(x16, wg_ref[0].astype(BF16))
    u = _dot(x16, wu_ref[0].astype(BF16))
    o_ref[...] = (g * _sigmoid(g) * u).astype(o_ref.dtype)


def _ffn_a(xe, w_gate, w_up, cap_pad, d_model):
    _, _, ff = w_gate.shape
    tf = _pick(ff, 256, LANES)
    nx = d_model // LANES
    nc = nx + ROW_PAD
    return pl.pallas_call(
        functools.partial(_ffn_a_kernel, cap_pad=cap_pad, nc=nc, nx=nx),
        grid=(N_EXPERTS, ff // tf),
        in_specs=[pl.BlockSpec((cap_pad * nc, LANES), lambda e, f: (e, 0), pipeline_mode=pl.Buffered(1)),
                  pl.BlockSpec((1, d_model, tf), lambda e, f: (e, 0, f)),
                  pl.BlockSpec((1, d_model, tf), lambda e, f: (e, 0, f))],
        out_specs=[pl.BlockSpec((cap_pad, tf), lambda e, f: (e, f)),
                   pl.BlockSpec((cap_pad, LANES), lambda e, f: (e, 0))],
        out_shape=[jax.ShapeDtypeStruct((N_EXPERTS * cap_pad, ff), BF16),
                   jax.ShapeDtypeStruct((N_EXPERTS * cap_pad, LANES), F32)],
        scratch_shapes=[pltpu.VMEM((cap_pad, d_model), BF16)],
        compiler_params=_params(2),
        name="expert_ffn_in",
    )(xe, w_gate, w_up)


def _ffn_b_kernel(h_ref, wd_ref, aff_ref, o_ref):
    e = pl.program_id(0)
    y = _dot(h_ref[...], wd_ref[0].astype(BF16))
    lane = lax.broadcasted_iota(jnp.int32, (1, LANES), 1)
    gate = jnp.sum(jnp.where(lane == e, aff_ref[...], 0.0), axis=-1, keepdims=True)
    y = y * gate
    for c in range(y.shape[1] // LANES):
        o_ref[:, c, :] = y[:, LANES * c:LANES * (c + 1)]


def _ffn_b(hid, w_down, aff, cap_pad, d_model):
    _, ff, _ = w_down.shape
    td = _pick(d_model, 1024, 8 * LANES) if d_model % (8 * LANES) == 0 else d_model
    return pl.pallas_call(
        _ffn_b_kernel,
        grid=(N_EXPERTS, d_model // td),
        in_specs=[pl.BlockSpec((cap_pad, ff), lambda e, n: (e, 0)),
                  pl.BlockSpec((1, ff, td), lambda e, n: (e, 0, n)),
                  pl.BlockSpec((cap_pad, LANES), lambda e, n: (e, 0))],
        out_specs=pl.BlockSpec((cap_pad, td // LANES, LANES), lambda e, n: (e, n, 0)),
        out_shape=jax.ShapeDtypeStruct((N_EXPERTS * cap_pad, d_model // LANES, LANES), F32),
        compiler_params=_params(2),
        name="expert_ffn_out",
    )(hid, w_down, aff)


def _combine_kernel(idx_ref, lo_ref, h_ref, ye_ref, nw_ref, o_ref, hbuf_ref, acc_ref, stage_ref, sem, hsem, *,
                    tt, nx, cap_pad):
    i = pl.program_id(0)
    t0 = N_META + i * tt
    ch = COMBINE_CHUNK
    pitch = nx + ROW_PAD
    last_chunk = N_EXPERTS * cap_pad - ch
    hc = pltpu.make_async_copy(h_ref.at[pl.ds(pl.multiple_of(t0, 8), tt)], hbuf_ref, hsem)
    hc.start()

    def chunk_start(e, lo, j):
        return jnp.minimum(e * cap_pad + lo + j * ch, last_chunk)

    def n_chunks(e):
        return (lo_ref[e, i + 1] - lo_ref[e, i] + ch - 1) // ch

    def chunk_copy(e, buf, j):
        src = pl.multiple_of(chunk_start(e, lo_ref[e, i], j) * nx, 8)
        dst = pl.multiple_of(j * ch * nx, 8)
        return pltpu.make_async_copy(ye_ref.at[pl.ds(src, ch * nx)], stage_ref.at[buf, pl.ds(dst, ch * nx)],
                                     sem.at[buf])

    def fetch(e, buf):
        def body(j, c):
            chunk_copy(e, buf, j).start()
            return c

        lax.fori_loop(0, n_chunks(e), body, 0)

    fetch(0, 0)
    hc.wait()
    for c in range(nx):
        acc_ref[pl.ds(c, tt, stride=pitch), :] = hbuf_ref[:, LANES * c:LANES * (c + 1)]

    def per_expert(e, carry):
        buf = e % 2

        def drain(j, c):
            chunk_copy(e, buf, j).wait()
            return c

        lax.fori_loop(0, n_chunks(e), drain, 0)

        @pl.when(e + 1 < N_EXPERTS)
        def _():
            fetch(e + 1, 1 - buf)

        lo = lo_ref[e, i]

        def per_slot(s, c):
            j = (s - lo) // ch
            pos = j * ch + (e * cap_pad + s - chunk_start(e, lo, j))
            src = pl.ds(pl.multiple_of(pos * nx, 8), nx)
            dst = pl.ds(pl.multiple_of((idx_ref[e, s] - t0) * pitch, 8), nx)
            acc_ref[dst, :] = acc_ref[dst, :] + stage_ref[buf, src, :]
            return c

        lax.fori_loop(lo, lo_ref[e, i + 1], per_slot, 0)
        return carry

    lax.fori_loop(0, N_EXPERTS, per_expert, 0)
    for c in range(nx):
        hbuf_ref[:, LANES * c:LANES * (c + 1)] = acc_ref[pl.ds(c, tt, stride=pitch), :]
    x = hbuf_ref[...]
    o_ref[...] = x * lax.rsqrt(jnp.mean(x * x, axis=-1, keepdims=True) + NORM_EPS) * nw_ref[...]


def _combine(idx, lo, h2, ye, norm_f_w, n_real, tt, cap_pad):
    d = h2.shape[1]
    nx = d // LANES
    stage_slots = -(-(tt + COMBINE_CHUNK) // COMBINE_CHUNK) * COMBINE_CHUNK
    return pl.pallas_call(
        functools.partial(_combine_kernel, tt=tt, nx=nx, cap_pad=cap_pad),
        grid_spec=pltpu.PrefetchScalarGridSpec(
            num_scalar_prefetch=2,
            grid=(n_real // tt,),
            in_specs=[pl.BlockSpec(memory_space=pl.ANY),
                      pl.BlockSpec(memory_space=pl.ANY),
                      pl.BlockSpec((1, d), lambda i, *_: (0, 0))],
            out_specs=pl.BlockSpec((tt, d), lambda i, *_: (i, 0)),
            scratch_shapes=[pltpu.VMEM((tt, d), F32), pltpu.VMEM((tt * (nx + ROW_PAD), LANES), F32),
                            pltpu.VMEM((2, stage_slots * nx, LANES), F32),
                            pltpu.SemaphoreType.DMA((2,)), pltpu.SemaphoreType.DMA]),
        out_shape=jax.ShapeDtypeStruct((n_real, d), F32),
        compiler_params=_params(1),
        name="expert_combine",
    )(idx, lo, h2, ye, norm_f_w.reshape(1, d).astype(F32))


def _layer(hp, seq_len, norm1_w, w_in, conv_w, a_log_fwd, a_log_bwd, dt_bias_fwd, dt_bias_bwd, out_norm_w,
           w_branch_a, attn_sink, w_branch_b, w_out, norm2_w, w_router, w_gate, w_up, w_down):
    lp, d = hp.shape
    main_w = 2 * DN_QK + 2 * DN_VW
    gate_w = 4 * DN_HEADS
    rest_w = SWA_QW + 2 * SWA_KVW + 2 * d
    n = _rmsnorm(hp, norm1_w, BF16)
    w_t = jnp.swapaxes(w_in, 0, 1)
    proj_main = _matmul(n, w_t, main_w, 0, BF16, w_rows=True, name="in_proj_main")
    gates = _matmul(n, w_t, LANES, main_w, F32, w_rows=True, name="in_proj_gates")
    rest = _matmul(n, w_t, rest_w, main_w + gate_w, BF16, w_rows=True, name="in_proj_rest")

    prm = jnp.zeros((8, LANES), F32)
    prm = prm.at[0, 2 * DN_HEADS:4 * DN_HEADS].set(jnp.concatenate([a_log_fwd, a_log_bwd]).astype(F32))
    prm = prm.at[1, 2 * DN_HEADS:4 * DN_HEADS].set(jnp.concatenate([dt_bias_fwd, dt_bias_bwd]).astype(F32))
    conv_f = conv_w.astype(F32)
    o_f = _deltanet(proj_main, conv_f, gates, prm, seq_len, backward=False)
    o_r = _deltanet(proj_main, conv_f, gates, prm, seq_len, backward=True)
    o_a = _gated_norm(o_f, o_r, proj_main, out_norm_w)
    o_b = _window_attention(rest, attn_sink, seq_len)

    mixed = _branch_merge(o_a, o_b, w_branch_a, w_branch_b, rest, d)
    h2 = _matmul(mixed, w_out, d, 0, F32, res=hp, name="out_proj")

    cap = EC_CAPACITY * seq_len // N_EXPERTS
    cap_pad = -(-cap // 16) * 16
    xg, afft = _router(h2, norm2_w, w_router, seq_len)
    n_real = seq_len - N_META
    tt = _pick(n_real, TOKEN_TILE, LANES)
    slot_pad = -(-cap_pad // LANES) * LANES
    idx, lo = _select(afft, cap, slot_pad, tt, n_real // tt)
    xe = _dispatch(xg.reshape(-1, LANES), idx, cap, cap_pad, d // LANES + ROW_PAD)
    hid, aff = _ffn_a(xe, w_gate, w_up, cap_pad, d)
    ye = _ffn_b(hid, w_down, aff, cap_pad, d)
    return functools.partial(_combine, idx, lo, h2, ye.reshape(-1, LANES), n_real=n_real, tt=tt, cap_pad=cap_pad)


def kernel(x, meta_tokens, norm1_w, w_in, conv_w, a_log_fwd, a_log_bwd, dt_bias_fwd, dt_bias_bwd, out_norm_w,
           w_branch_a, attn_sink, w_branch_b, w_out, norm2_w, w_router, w_gate, w_up, w_down, norm_f_w):
    batch, seq, d = x.shape
    depth = norm1_w.shape[0]
    assert depth == 1, "the final norm is fused into the last layer's expert combine"
    seq_len = N_META + seq
    lp = -(-seq_len // TOKEN_TILE) * TOKEN_TILE
    outs = []
    for b in range(batch):
        h = jnp.concatenate([meta_tokens.astype(x.dtype), x[b]], axis=0)
        hp = jnp.pad(h, ((0, lp - seq_len), (0, 0)))
        combine = _layer(hp, seq_len, norm1_w[0], w_in[0], conv_w[0], a_log_fwd[0], a_log_bwd[0],
                         dt_bias_fwd[0], dt_bias_bwd[0], out_norm_w[0], w_branch_a[0], attn_sink[0],
                         w_branch_b[0], w_out[0], norm2_w[0], w_router[0], w_gate[0], w_up[0], w_down[0])
        outs.append(combine(norm_f_w=norm_f_w))
    return jnp.stack(outs, axis=0)
```

```python
import functools

import jax
import jax.numpy as jnp
from jax import lax
from jax.experimental import pallas as pl
from jax.experimental.pallas import tpu as pltpu

F32 = jnp.float32
BF16 = jnp.bfloat16

N_META = 16
NORM_EPS = 1e-6
DN_HEADS = 16
DN_DK = 128
DN_DV = 128
SWA_HQ = 16
SWA_HKV = 4
SWA_D = 128
SWA_WINDOW = 128
N_EXPERTS = 16
EC_CAPACITY = 2

LANES = 128
TOKEN_TILE = 256
DN_CHUNK = 128
DN_HEADS_PER_STEP = 8
ATTN_TILE = 128
ROW_PAD = 8
DMA_ISSUE_UNROLL = 6
COMBINE_CHUNK = 8
COMBINE_REGION = 64
VMEM_LIMIT = 56 * 1024 * 1024

DN_QK = DN_HEADS * DN_DK
DN_VW = DN_HEADS * DN_DV
SWA_QW = SWA_HQ * SWA_D
SWA_KVW = SWA_HKV * SWA_D


def _params(n_grid):
    return pltpu.CompilerParams(dimension_semantics=("arbitrary",) * n_grid, vmem_limit_bytes=VMEM_LIMIT)


def _pick(n, target, mult):
    best = None
    for t in range(mult, min(n, target) + 1, mult):
        if n % t == 0:
            best = t
    assert best is not None, (n, target, mult)
    return best


def _sigmoid(x):
    return 1.0 / (1.0 + jnp.exp(-x))


def _dot(a, b):
    return jnp.dot(a, b, preferred_element_type=F32)


def _dot_nt(a, b):
    return lax.dot_general(a, b, (((1,), (1,)), ((), ())), preferred_element_type=F32)


def _rms_kernel(x_ref, w_ref, o_ref):
    x = x_ref[...]
    ms = jnp.mean(x * x, axis=-1, keepdims=True)
    o_ref[...] = (x * lax.rsqrt(ms + NORM_EPS) * w_ref[...]).astype(o_ref.dtype)


def _rmsnorm(x, w, out_dtype):
    m, d = x.shape
    tr = _pick(m, 256, 16)
    return pl.pallas_call(
        _rms_kernel,
        grid=(m // tr,),
        in_specs=[pl.BlockSpec((tr, d), lambda i: (i, 0)), pl.BlockSpec((1, d), lambda i: (0, 0))],
        out_specs=pl.BlockSpec((tr, d), lambda i: (i, 0)),
        out_shape=jax.ShapeDtypeStruct((m, d), out_dtype),
        compiler_params=_params(1),
        name="rmsnorm",
    )(x, w.reshape(1, d).astype(F32))


def _mm_kernel(*refs, has_res, w_rows):
    if has_res:
        a_ref, w_ref, r_ref, o_ref, wb_ref = refs
    else:
        a_ref, w_ref, o_ref, wb_ref = refs

    @pl.when(pl.program_id(1) == 0)
    def _():
        w = w_ref[...]
        wb_ref[...] = (jnp.transpose(w) if w_rows else w).astype(BF16)

    acc = _dot(a_ref[...], wb_ref[...])
    if has_res:
        acc = acc + r_ref[...]
    o_ref[...] = acc.astype(o_ref.dtype)


def _matmul(a, w, n_cols, off, out_dtype, res=None, w_rows=False, name="matmul"):
    m, k = a.shape
    tn = _pick(n_cols, 512, LANES)
    tm = _pick(m, 1408 if res is None else 768, 16)
    if w_rows:
        if off % tn == 0:
            w_spec = pl.BlockSpec((tn, k), lambda n, i: (off // tn + n, 0))
        else:
            assert off % 8 == 0
            w_spec = pl.BlockSpec((pl.Element(tn), pl.Element(k)), lambda n, i: (pl.multiple_of(off + n * tn, 8), 0))
    else:
        assert off % tn == 0
        w_spec = pl.BlockSpec((k, tn), lambda n, i: (0, off // tn + n))
    in_specs = [pl.BlockSpec((tm, k), lambda n, i: (i, 0)), w_spec]
    args = [a, w]
    if res is not None:
        in_specs.append(pl.BlockSpec((tm, tn), lambda n, i: (i, n)))
        args.append(res)
    return pl.pallas_call(
        functools.partial(_mm_kernel, has_res=res is not None, w_rows=w_rows),
        grid=(n_cols // tn, m // tm),
        in_specs=in_specs,
        out_specs=pl.BlockSpec((tm, tn), lambda n, i: (i, n)),
        out_shape=jax.ShapeDtypeStruct((m, n_cols), out_dtype),
        scratch_shapes=[pltpu.VMEM((k, tn), BF16)],
        compiler_params=_params(2),
        name=name,
    )(*args)


def _dn_kernel(q_ref, qp_ref, qn_ref, k_ref, kp_ref, kn_ref, v_ref, vp_ref, vn_ref, cq_ref, ck_ref, cv_ref,
               sm_ref, prm_ref, o_ref, s_ref, *, seq_len, tl, backward):
    hg = pl.program_id(0)
    i = pl.program_id(1)
    nt = pl.num_programs(1)
    ti = nt - 1 - i if backward else i
    cs = DN_CHUNK
    cs_shift = cs.bit_length() - 1
    hpb = DN_HEADS_PER_STEP

    @pl.when(i == 0)
    def _():
        s_ref[...] = jnp.zeros_like(s_ref)

    row = lax.broadcasted_iota(jnp.int32, (tl, 1), 0)
    validf = ((ti * tl + row) < seq_len).astype(F32)
    has_prev = (ti > 0).astype(F32)
    has_next = (ti < nt - 1).astype(F32)

    def conv_silu(x_ref, p_ref, n_ref, c_ref):
        x = x_ref[...].astype(F32)
        prev = p_ref[...].astype(F32)[15:16, :] * has_prev
        nxt = n_ref[...].astype(F32)[0:1, :] * has_next
        xm1 = jnp.where(row == 0, prev, pltpu.roll(x, 1, 0))
        xp1 = jnp.where(row == tl - 1, nxt, pltpu.roll(x, tl - 1, 0))
        c = c_ref[...]
        y = xm1 * c[0:1] + x * c[1:2] + xp1 * c[2:3]
        return y * _sigmoid(y)

    q_all = conv_silu(q_ref, qp_ref, qn_ref, cq_ref)
    k_all = conv_silu(k_ref, kp_ref, kn_ref, ck_ref)
    v_all = conv_silu(v_ref, vp_ref, vn_ref, cv_ref)

    lane = lax.broadcasted_iota(jnp.int32, (1, LANES), 1)
    sm = sm_ref[...]
    prm = prm_ref[...]
    z = sm + prm[1:2]
    softplus = jnp.maximum(z, 0.0) + jnp.log1p(jnp.exp(-jnp.abs(z)))
    beta_all = _sigmoid(sm) * validf
    g_all = -jnp.exp(prm[0:1]) * softplus * validf

    r2 = lax.broadcasted_iota(jnp.int32, (tl, tl), 0)
    c2 = lax.broadcasted_iota(jnp.int32, (tl, tl), 1)
    incl2 = ((r2 >> cs_shift) == (c2 >> cs_shift)) & ((r2 <= c2) if backward else (r2 >= c2))
    tri = jnp.where(incl2, 1.0, 0.0).astype(BF16)
    g1 = g_all.astype(BF16)
    rem = g_all - g1.astype(F32)
    g2 = rem.astype(BF16)
    g3 = (rem - g2.astype(F32)).astype(BF16)
    gc_all = _dot(tri, g1) + _dot(tri, g2) + _dot(tri, g3)

    r = lax.broadcasted_iota(jnp.int32, (cs, cs), 0)
    c = lax.broadcasted_iota(jnp.int32, (cs, cs), 1)
    incl = (r <= c) if backward else (r >= c)
    strict = (r < c) if backward else (r > c)
    levels = []
    b = 1
    while b < cs:
        sh = b.bit_length() - 1
        levels.append(((r >> (sh + 1)) == (c >> (sh + 1))) & ((r >> sh) != (c >> sh)))
        b *= 2

    chunk_order = range(tl // cs - 1, -1, -1) if backward else range(tl // cs)
    streams = []
    for j in range(hpb):
        hs = slice(DN_DK * j, DN_DK * (j + 1))
        q = q_all[:, hs]
        q = q * lax.rsqrt(jnp.sum(q * q, axis=-1, keepdims=True) + NORM_EPS) * (DN_DK ** -0.5) * validf
        k = k_all[:, hs]
        k = k * lax.rsqrt(jnp.sum(k * k, axis=-1, keepdims=True) + NORM_EPS) * validf
        v = v_all[:, hs] * validf
        col = (DN_HEADS if backward else 0) + hg * hpb + j
        beta = jnp.sum(jnp.where(lane == col, beta_all, 0.0), axis=-1, keepdims=True)
        g = jnp.sum(jnp.where(lane == 2 * DN_HEADS + col, g_all, 0.0), axis=-1, keepdims=True)
        gcum = jnp.sum(jnp.where(lane == 2 * DN_HEADS + col, gc_all, 0.0), axis=-1, keepdims=True)
        for ci in chunk_order:
            rs = slice(ci * cs, (ci + 1) * cs)
            qc, kc, vc, bc = q[rs], k[rs], v[rs], beta[rs]
            gtot = jnp.sum(g[rs], axis=0, keepdims=True)
            gb = jnp.broadcast_to(gcum[rs], (cs, LANES))
            gj = jnp.transpose(gb)[0:1, :]
            decay = jnp.where(incl, jnp.exp(jnp.where(incl, gb[:, 0:1] - gj, 0.0)), 0.0)
            kb = kc * bc
            k16 = kc.astype(BF16)
            egc = jnp.exp(gb)
            streams.append(dict(
                j=j, rs=rs, hs=hs, gtot=gtot,
                m=jnp.where(strict, _dot_nt(kb.astype(BF16), k16) * decay, 0.0),
                qk=(_dot_nt(qc.astype(BF16), k16) * decay).astype(BF16),
                rhs=jnp.concatenate([vc * bc, kb * egc], axis=1),
                q_dec=qc * egc,
                k_dec_t=jnp.transpose(kc * jnp.exp(gtot - gb)).astype(BF16)))

    for st in streams:
        st["e"] = -jnp.where(levels[0], st["m"], 0.0)
    for mask in levels[1:]:
        for st in streams:
            lb = jnp.where(mask, st["m"], 0.0)
            st["lb"] = lb
            st["y"] = lb + _dot(lb.astype(BF16), st["e"].astype(BF16))
        for st in streams:
            st["e"] = st["e"] - st["y"] - _dot(st["e"].astype(BF16), st["y"].astype(BF16))
    for st in streams:
        st["uw"] = st["rhs"] + _dot(st["e"].astype(BF16), st["rhs"].astype(BF16))

    state = [s_ref[j] for j in range(hpb)]
    for step in range(tl // cs):
        cur = [streams[j * (tl // cs) + step] for j in range(hpb)]
        for st in cur:
            wq = jnp.concatenate([st["uw"][:, DN_DV:], st["q_dec"]], axis=0).astype(BF16)
            st["wqs"] = _dot(wq, state[st["j"]].astype(BF16))
        for st in cur:
            v16 = (st["uw"][:, :DN_DV] - st["wqs"][:cs]).astype(BF16)
            o_ref[st["rs"], st["hs"]] = st["wqs"][cs:] + _dot(st["qk"], v16)
            state[st["j"]] = state[st["j"]] * jnp.exp(st["gtot"]) + _dot(st["k_dec_t"], v16)
    for j in range(hpb):
        s_ref[j] = state[j]


def _deltanet(proj_main, conv_w, gates, prm, seq_len, backward):
    lp = proj_main.shape[0]
    tl = TOKEN_TILE
    nt = lp // tl
    hb = tl // 16
    nhb = lp // 16
    hpb = DN_HEADS_PER_STEP
    wb = hpb * DN_DK
    ngroups = DN_HEADS // hpb

    def tile(i):
        return nt - 1 - i if backward else i

    def main_spec(off):
        return pl.BlockSpec((tl, wb), lambda h, i: (tile(i), off + h))

    def prev_spec(off):
        return pl.BlockSpec((16, wb), lambda h, i: (jnp.maximum(tile(i) * hb - 1, 0), off + h))

    def next_spec(off):
        return pl.BlockSpec((16, wb), lambda h, i: (jnp.minimum((tile(i) + 1) * hb, nhb - 1), off + h))

    def conv_spec(off):
        return pl.BlockSpec((3, wb), lambda h, i: (0, off + h))

    in_specs = []
    args = []
    for off in (0, ngroups, 2 * ngroups):
        in_specs += [main_spec(off), prev_spec(off), next_spec(off)]
        args += [proj_main] * 3
    in_specs += [conv_spec(0), conv_spec(ngroups), conv_spec(2 * ngroups)]
    args += [conv_w] * 3
    in_specs += [pl.BlockSpec((tl, LANES), lambda h, i: (tile(i), 0)),
                 pl.BlockSpec((8, LANES), lambda h, i: (0, 0))]
    args += [gates, prm]
    return pl.pallas_call(
        functools.partial(_dn_kernel, seq_len=seq_len, tl=tl, backward=backward),
        grid=(ngroups, nt),
        in_specs=in_specs,
        out_specs=pl.BlockSpec((tl, wb), lambda h, i: (tile(i), h)),
        out_shape=jax.ShapeDtypeStruct((lp, DN_VW), F32),
        scratch_shapes=[pltpu.VMEM((hpb, DN_DK, DN_DV), F32)],
        compiler_params=_params(2),
        name="deltanet_bwd" if backward else "deltanet_fwd",
    )(*args)


def _gnorm_kernel(of_ref, ob_ref, z_ref, w_ref, o_ref, *, nh):
    for j in range(nh):
        sl = slice(DN_DV * j, DN_DV * (j + 1))
        o = of_ref[:, sl] + ob_ref[:, sl]
        o = o * lax.rsqrt(jnp.mean(o * o, axis=-1, keepdims=True) + NORM_EPS) * w_ref[...]
        z = z_ref[:, sl].astype(F32)
        o_ref[:, sl] = (o * (z * _sigmoid(z))).astype(o_ref.dtype)


def _gated_norm(o_f, o_b, proj_main, out_norm_w):
    lp = proj_main.shape[0]
    nh = 4
    wb = nh * DN_DV
    tr = _pick(lp, 768, 16)
    zoff = (2 * DN_QK + DN_VW) // wb
    return pl.pallas_call(
        functools.partial(_gnorm_kernel, nh=nh),
        grid=(lp // tr, DN_VW // wb),
        in_specs=[pl.BlockSpec((tr, wb), lambda i, j: (i, j)),
                  pl.BlockSpec((tr, wb), lambda i, j: (i, j)),
                  pl.BlockSpec((tr, wb), lambda i, j: (i, zoff + j)),
                  pl.BlockSpec((1, DN_DV), lambda i, j: (0, 0))],
        out_specs=pl.BlockSpec((tr, wb), lambda i, j: (i, j)),
        out_shape=jax.ShapeDtypeStruct((lp, DN_VW), BF16),
        compiler_params=_params(2),
        name="gated_norm",
    )(o_f, o_b, proj_main, out_norm_w.reshape(1, DN_DV).astype(F32))


def _attn_kernel(q_ref, k0_ref, kp_ref, kc_ref, kn_ref, v0_ref, vp_ref, vc_ref, vn_ref, prm_ref, o_ref, *,
                 seq_len, tq):
    i = pl.program_id(1)
    groups = SWA_HQ // SWA_HKV
    kall = jnp.concatenate([k0_ref[...], kp_ref[...], kc_ref[...], kn_ref[...]], axis=0)
    vall = jnp.concatenate([v0_ref[...], vp_ref[...], vc_ref[...], vn_ref[...]], axis=0)
    pq = i * tq + lax.broadcasted_iota(jnp.int32, (tq, 1), 0)
    ck = lax.broadcasted_iota(jnp.int32, (1, 4 * tq), 1)
    meta_part = ck < tq
    pk = (i - 1) * tq + (ck - tq)
    real_q = pq >= N_META
    dist = pq - pk
    adist = jnp.abs(dist)
    win_part = jnp.logical_not(meta_part)
    in_win = (real_q & (adist <= SWA_WINDOW)) | (jnp.logical_not(real_q) & (-dist <= SWA_WINDOW))
    win_ok = win_part & (pk >= N_META) & (pk < seq_len) & in_win
    ok = (meta_part & (ck < N_META)) | win_ok
    bias = jnp.where(ok, jnp.where(real_q & win_part, -adist.astype(F32), 0.0), -jnp.inf)
    prm = prm_ref[0]
    scale = SWA_D ** -0.5
    heads = range(groups)
    sinks = [prm[groups + g:groups + g + 1, 0:1] for g in heads]
    s = [_dot_nt(q_ref[:, SWA_D * g:SWA_D * (g + 1)], kall) * scale + prm[g:g + 1, 0:1] * bias for g in heads]
    m = [jnp.maximum(jnp.max(s[g], axis=-1, keepdims=True), sinks[g]) for g in heads]
    p = [jnp.exp(s[g] - m[g]) for g in heads]
    den = [jnp.sum(p[g], axis=-1, keepdims=True) + jnp.exp(sinks[g] - m[g]) for g in heads]
    o = [_dot(p[g].astype(BF16), vall) for g in heads]
    for g in heads:
        o_ref[:, SWA_D * g:SWA_D * (g + 1)] = (o[g] / den[g]).astype(o_ref.dtype)


def _window_attention(rest, attn_sink, seq_len):
    lp = rest.shape[0]
    tq = ATTN_TILE
    nq = lp // tq
    groups = SWA_HQ // SWA_HKV
    koff = SWA_QW // SWA_D
    voff = koff + SWA_HKV
    slopes = 2.0 ** (-8.0 * jnp.arange(1, SWA_HQ + 1, dtype=F32) / SWA_HQ)
    prm = jnp.concatenate([slopes.reshape(SWA_HKV, groups), attn_sink.astype(F32).reshape(SWA_HKV, groups)], axis=1)
    prm = jnp.broadcast_to(prm[:, :, None], (SWA_HKV, 2 * groups, LANES))

    def kv_specs(off):
        return [pl.BlockSpec((tq, SWA_D), lambda h, i: (0, off + h)),
                pl.BlockSpec((tq, SWA_D), lambda h, i: (jnp.maximum(i - 1, 0), off + h)),
                pl.BlockSpec((tq, SWA_D), lambda h, i: (i, off + h)),
                pl.BlockSpec((tq, SWA_D), lambda h, i: (jnp.minimum(i + 1, nq - 1), off + h))]

    qw = groups * SWA_D
    return pl.pallas_call(
        functools.partial(_attn_kernel, seq_len=seq_len, tq=tq),
        grid=(SWA_HKV, nq),
        in_specs=[pl.BlockSpec((tq, qw), lambda h, i: (i, h))] + kv_specs(koff) + kv_specs(voff)
        + [pl.BlockSpec((1, 2 * groups, LANES), lambda h, i: (h, 0, 0))],
        out_specs=pl.BlockSpec((tq, qw), lambda h, i: (i, h)),
        out_shape=jax.ShapeDtypeStruct((lp, SWA_QW), BF16),
        compiler_params=_params(2),
        name="window_attention",
    )(*([rest] * 9), prm)


def _branch_kernel(oa_ref, ob_ref, wa_ref, wb_ref, ga_ref, gb_ref, o_ref, wa16_ref, wb16_ref):
    @pl.when(pl.program_id(1) == 0)
    def _():
        wa16_ref[...] = wa_ref[...].astype(BF16)
        wb16_ref[...] = wb_ref[...].astype(BF16)

    ya = _dot(oa_ref[...], wa16_ref[...])
    yb = _dot(ob_ref[...], wb16_ref[...])
    o = _sigmoid(ga_ref[...].astype(F32)) * ya + _sigmoid(gb_ref[...].astype(F32)) * yb
    o_ref[...] = o.astype(o_ref.dtype)


def _branch_merge(o_a, o_b, w_a, w_b, rest, d_model):
    lp = o_a.shape[0]
    tn = _pick(d_model, 512, LANES)
    tm = _pick(lp, 768, 16)
    ga_off = (SWA_QW + 2 * SWA_KVW) // tn
    gb_off = (SWA_QW + 2 * SWA_KVW + d_model) // tn
    assert (SWA_QW + 2 * SWA_KVW) % tn == 0 and d_model % tn == 0
    return pl.pallas_call(
        _branch_kernel,
        grid=(d_model // tn, lp // tm),
        in_specs=[pl.BlockSpec((tm, DN_VW), lambda n, i: (i, 0)),
                  pl.BlockSpec((tm, SWA_QW), lambda n, i: (i, 0)),
                  pl.BlockSpec((DN_VW, tn), lambda n, i: (0, n)),
                  pl.BlockSpec((SWA_QW, tn), lambda n, i: (0, n)),
                  pl.BlockSpec((tm, tn), lambda n, i: (i, ga_off + n)),
                  pl.BlockSpec((tm, tn), lambda n, i: (i, gb_off + n))],
        out_specs=pl.BlockSpec((tm, tn), lambda n, i: (i, n)),
        out_shape=jax.ShapeDtypeStruct((lp, d_model), BF16),
        scratch_shapes=[pltpu.VMEM((DN_VW, tn), BF16), pltpu.VMEM((SWA_QW, tn), BF16)],
        compiler_params=_params(2),
        name="branch_merge",
    )(o_a, o_b, w_a, w_b, rest, rest)


def _router_kernel(h_ref, w_ref, wr_ref, xg_ref, afft_ref, *, seq_len, tr, d_model):
    i = pl.program_id(0)
    x = h_ref[...]
    xn = x * lax.rsqrt(jnp.mean(x * x, axis=-1, keepdims=True) + NORM_EPS) * w_ref[...]
    logits = _dot_nt(xn.astype(BF16), wr_ref[...].astype(BF16))
    lane = lax.broadcasted_iota(jnp.int32, (1, LANES), 1)
    lm = jnp.where(lane < N_EXPERTS, logits, -jnp.inf)
    ex = jnp.exp(lm - jnp.max(lm, axis=-1, keepdims=True))
    aff = ex / jnp.sum(ex, axis=-1, keepdims=True)
    row = i * tr + lax.broadcasted_iota(jnp.int32, (tr, 1), 0)
    aff = jnp.where((row < seq_len) & (lane < N_EXPERTS), aff, -1.0)
    nx = d_model // LANES
    for c in range(nx):
        xg_ref[:, c, :] = xn[:, LANES * c:LANES * (c + 1)]
    xg_ref[:, nx, :] = aff
    xg_ref[:, nx + 1:, :] = jnp.zeros((tr, ROW_PAD - 1, LANES), F32)
    afft_ref[...] = jnp.transpose(aff)[:N_EXPERTS, :]


def _router(h2, norm2_w, w_router, seq_len):
    lp, d = h2.shape
    tr = TOKEN_TILE
    wr = jnp.zeros((LANES, d), F32).at[:N_EXPERTS].set(jnp.swapaxes(w_router, 0, 1).astype(F32))
    nc = d // LANES + ROW_PAD
    return pl.pallas_call(
        functools.partial(_router_kernel, seq_len=seq_len, tr=tr, d_model=d),
        grid=(lp // tr,),
        in_specs=[pl.BlockSpec((tr, d), lambda i: (i, 0)),
                  pl.BlockSpec((1, d), lambda i: (0, 0)),
                  pl.BlockSpec((LANES, d), lambda i: (0, 0))],
        out_specs=[pl.BlockSpec((tr, nc, LANES), lambda i: (i, 0, 0)),
                   pl.BlockSpec((N_EXPERTS, tr), lambda i: (0, i))],
        out_shape=[jax.ShapeDtypeStruct((lp, nc, LANES), F32), jax.ShapeDtypeStruct((N_EXPERTS, lp), F32)],
        compiler_params=_params(1),
        name="router",
    )(h2, norm2_w.reshape(1, d).astype(F32), wr)


def _select_kernel(afft_ref, idx_ref, lo_ref, sel_ref, csum_ref, cols_ref, *, cap, slot_pad, lp, tile, n_tiles):
    aff = afft_ref[...]
    capf = float(cap)

    def count_ge(thr):
        return jnp.sum(jnp.where(aff >= thr, 1.0, 0.0), axis=1, keepdims=True)

    def bisect(_, carry):
        lo, hi = carry
        mid = 0.5 * (lo + hi)
        ge = count_ge(mid) >= capf
        return jnp.where(ge, mid, lo), jnp.where(ge, hi, mid)

    lo0 = jnp.zeros((N_EXPERTS, 1), F32)
    hi0 = jnp.full((N_EXPERTS, 1), 2.0, F32)
    _, hi = lax.fori_loop(0, 40, bisect, (lo0, hi0))

    def refine(st):
        hi, tau, done, _ = st
        cand = jnp.max(jnp.where(aff < hi, aff, -2.0), axis=1, keepdims=True)
        found = jnp.where(count_ge(cand) >= capf, 1.0, 0.0)
        tau = jnp.where(done > 0.0, tau, cand)
        hi = jnp.where(done + found > 0.0, hi, cand)
        done = jnp.maximum(done, found)
        return hi, tau, done, jnp.sum(1.0 - done)

    zero = jnp.zeros((N_EXPERTS, 1), F32)
    _, tau, _, _ = lax.while_loop(lambda st: st[3] > 0.0, refine, (hi, zero, zero, jnp.float32(N_EXPERTS)))
    need = capf - jnp.sum(jnp.where(aff > tau, 1.0, 0.0), axis=1, keepdims=True)

    ra = lax.broadcasted_iota(jnp.int32, (LANES, LANES), 0)
    ca = lax.broadcasted_iota(jnp.int32, (LANES, LANES), 1)
    ut = jnp.where(ra <= ca, 1.0, 0.0).astype(BF16)
    carry_eq = jnp.zeros((N_EXPERTS, 1), F32)
    carry_sel = jnp.zeros((N_EXPERTS, 1), F32)
    for j in range(lp // LANES):
        sl = slice(LANES * j, LANES * (j + 1))
        bj = aff[:, sl]
        eqf = jnp.where(bj == tau, 1.0, 0.0)
        tie_rank = _dot(eqf.astype(BF16), ut) + carry_eq - eqf
        carry_eq = carry_eq + jnp.sum(eqf, axis=1, keepdims=True)
        self = jnp.where((bj > tau) | ((bj == tau) & (tie_rank < need)), 1.0, 0.0)
        csum_ref[j] = _dot(self.astype(BF16), ut) + carry_sel
        carry_sel = carry_sel + jnp.sum(self, axis=1, keepdims=True)
        sel_ref[:, sl] = self

    lane = lax.broadcasted_iota(jnp.int32, (1, LANES), 1)
    sblk = 64
    cbase = lax.broadcasted_iota(jnp.int32, (sblk, 1), 0)
    cols_ref[...] = jnp.zeros_like(cols_ref)

    def per_expert(e, carry):
        def per_block(sb, carry2):
            cvals = (sb * sblk + cbase).astype(F32)

            def per_tile(j, acc):
                return acc + jnp.where(csum_ref[j, pl.ds(e, 1), :] <= cvals, 1.0, 0.0)

            n_lane_tiles = lp // LANES
            acc = lax.fori_loop(0, n_lane_tiles, per_tile, jnp.zeros((sblk, LANES), F32),
                                unroll=6 if n_lane_tiles % 6 == 0 else 1)
            rows = pl.ds(pl.multiple_of(sb * sblk, sblk), sblk)
            cols_ref[rows, :] = jnp.where(lane == e, jnp.sum(acc, axis=1, keepdims=True), cols_ref[rows, :])
            return carry2

        return lax.fori_loop(0, slot_pad // sblk, per_block, carry)

    lax.fori_loop(0, N_EXPERTS, per_expert, 0)
    idx_ref[...] = jnp.transpose(cols_ref[...])[:N_EXPERTS, :].astype(jnp.int32)

    sel = sel_ref[...]
    tok = lax.broadcasted_iota(jnp.int32, (1, lp), 1)
    table = jnp.zeros((N_EXPERTS, LANES), F32)
    for i in range(n_tiles + 1):
        below = jnp.sum(jnp.where(tok < N_META + tile * i, sel, 0.0), axis=1, keepdims=True)
        table = jnp.where(lane == i, below, table)
    lo_ref[...] = table.astype(jnp.int32)


def _select(afft, cap, slot_pad, tile, n_tiles):
    lp = afft.shape[1]
    assert n_tiles < LANES
    return pl.pallas_call(
        functools.partial(_select_kernel, cap=cap, slot_pad=slot_pad, lp=lp, tile=tile, n_tiles=n_tiles),
        out_shape=[jax.ShapeDtypeStruct((N_EXPERTS, slot_pad), jnp.int32),
                   jax.ShapeDtypeStruct((N_EXPERTS, LANES), jnp.int32)],
        scratch_shapes=[pltpu.VMEM((N_EXPERTS, lp), F32), pltpu.VMEM((lp // LANES, N_EXPERTS, LANES), F32),
                        pltpu.VMEM((slot_pad, LANES), F32)],
        compiler_params=pltpu.CompilerParams(vmem_limit_bytes=VMEM_LIMIT),
        name="expert_select",
    )(afft)


def _slab_copy(src, s_tok, dst, d_tok, sem, *, nc):
    s0 = pl.multiple_of(s_tok * nc, 8)
    d0 = pl.multiple_of(d_tok * nc, 8)
    return pltpu.make_async_copy(src.at[pl.ds(s0, nc)], dst.at[pl.ds(d0, nc)], sem)


def _dispatch_kernel(idx_ref, xg_ref, xe_ref, sem, *, cap, cap_pad, nc):
    copy = functools.partial(_slab_copy, nc=nc)
    e = pl.program_id(0)
    xe_ref[cap * nc:, :] = jnp.zeros(((cap_pad - cap) * nc, LANES), F32)

    def issue(s, c):
        copy(xg_ref, idx_ref[e, s], xe_ref, s, sem).start()
        return c

    lax.fori_loop(0, cap, issue, 0, unroll=DMA_ISSUE_UNROLL if cap % DMA_ISSUE_UNROLL == 0 else 1)

    def drain(_, c):
        copy(xg_ref, 0, xe_ref, 0, sem).wait()
        return c

    lax.fori_loop(0, cap, drain, 0)


def _dispatch(xg, idx, cap, cap_pad, nc):
    return pl.pallas_call(
        functools.partial(_dispatch_kernel, cap=cap, cap_pad=cap_pad, nc=nc),
        grid_spec=pltpu.PrefetchScalarGridSpec(
            num_scalar_prefetch=1,
            grid=(N_EXPERTS,),
            in_specs=[pl.BlockSpec(memory_space=pl.ANY)],
            out_specs=pl.BlockSpec((cap_pad * nc, LANES), lambda e, *_: (e, 0)),
            scratch_shapes=[pltpu.SemaphoreType.DMA]),
        out_shape=jax.ShapeDtypeStruct((N_EXPERTS * cap_pad * nc, LANES), F32),
        compiler_params=_params(1),
        name="expert_dispatch",
    )(idx, xg)


def _ffn_a_kernel(x_ref, wg_ref, wu_ref, o_ref, aff_ref, x16_ref, *, cap_pad, nc, nx):
    @pl.when(pl.program_id(1) == 0)
    def _():
        for c in range(nx):
            x16_ref[:, LANES * c:LANES * (c + 1)] = x_ref[pl.ds(c, cap_pad, stride=nc), :].astype(BF16)
        aff_ref[...] = x_ref[pl.ds(nx, cap_pad, stride=nc), :]

    x16 = x16_ref[...]
    g = _dot(x16, wg_ref[0].astype(BF16))
    u = _dot(x16, wu_ref[0].astype(BF16))
    o_ref[...] = (g * _sigmoid(g) * u).astype(o_ref.dtype)


def _ffn_a(xe, w_gate, w_up, cap_pad, d_model):
    _, _, ff = w_gate.shape
    tf = _pick(ff, 256, LANES)
    nx = d_model // LANES
    nc = nx + ROW_PAD
    return pl.pallas_call(
        functools.partial(_ffn_a_kernel, cap_pad=cap_pad, nc=nc, nx=nx),
        grid=(N_EXPERTS, ff // tf),
        in_specs=[pl.BlockSpec((cap_pad * nc, LANES), lambda e, f: (e, 0), pipeline_mode=pl.Buffered(1)),
                  pl.BlockSpec((1, d_model, tf), lambda e, f: (e, 0, f)),
                  pl.BlockSpec((1, d_model, tf), lambda e, f: (e, 0, f))],
        out_specs=[pl.BlockSpec((cap_pad, tf), lambda e, f: (e, f)),
                   pl.BlockSpec((cap_pad, LANES), lambda e, f: (e, 0))],
        out_shape=[jax.ShapeDtypeStruct((N_EXPERTS * cap_pad, ff), BF16),
                   jax.ShapeDtypeStruct((N_EXPERTS * cap_pad, LANES), F32)],
        scratch_shapes=[pltpu.VMEM((cap_pad, d_model), BF16)],
        compiler_params=_params(2),
        name="expert_ffn_in",
    )(xe, w_gate, w_up)


def _ffn_b_kernel(h_ref, wd_ref, aff_ref, o_ref):
    e = pl.program_id(0)
    y = _dot(h_ref[...], wd_ref[0].astype(BF16))
    lane = lax.broadcasted_iota(jnp.int32, (1, LANES), 1)
    gate = jnp.sum(jnp.where(lane == e, aff_ref[...], 0.0), axis=-1, keepdims=True)
    y = y * gate
    for c in range(y.shape[1] // LANES):
        o_ref[:, c, :] = y[:, LANES * c:LANES * (c + 1)]


def _ffn_b(hid, w_down, aff, cap_pad, d_model):
    _, ff, _ = w_down.shape
    td = _pick(d_model, 1024, 8 * LANES) if d_model % (8 * LANES) == 0 else d_model
    return pl.pallas_call(
        _ffn_b_kernel,
        grid=(N_EXPERTS, d_model // td),
        in_specs=[pl.BlockSpec((cap_pad, ff), lambda e, n: (e, 0)),
                  pl.BlockSpec((1, ff, td), lambda e, n: (e, 0, n)),
                  pl.BlockSpec((cap_pad, LANES), lambda e, n: (e, 0))],
        out_specs=pl.BlockSpec((cap_pad, td // LANES, LANES), lambda e, n: (e, n, 0)),
        out_shape=jax.ShapeDtypeStruct((N_EXPERTS * cap_pad, d_model // LANES, LANES), F32),
        compiler_params=_params(2),
        name="expert_ffn_out",
    )(hid, w_down, aff)


def _combine_kernel(idx_ref, lo_ref, h_ref, ye_ref, nw_ref, o_ref, hbuf_ref, acc_ref, stage_ref, sem, hsem, *,
                    tt, nx, cap_pad):
    i = pl.program_id(0)
    t0 = N_META + i * tt
    ch = COMBINE_CHUNK
    reg = COMBINE_REGION
    pitch = nx + ROW_PAD
    hc = pltpu.make_async_copy(h_ref.at[pl.ds(pl.multiple_of(t0, 8), tt)], hbuf_ref, hsem)
    hc.start()

    def round_chunks(e, r):
        left = lo_ref[e, i + 1] - lo_ref[e, i] - r * reg
        return (jnp.minimum(left, reg) + ch - 1) // ch

    def chunk_copy(e, r, j):
        src = pl.multiple_of((e * cap_pad + lo_ref[e, i] + r * reg + j * ch) * nx, 8)
        dst = pl.multiple_of(j * ch * nx, 8)
        return pltpu.make_async_copy(ye_ref.at[pl.ds(src, ch * nx)], stage_ref.at[e, pl.ds(dst, ch * nx)],
                                     sem.at[e])

    def fetch(e, r):
        def body(j, c):
            chunk_copy(e, r, j).start()
            return c

        lax.fori_loop(0, round_chunks(e, r), body, 0)

    def add_round(e, r):
        def drain(j, c):
            chunk_copy(e, r, j).wait()
            return c

        lax.fori_loop(0, round_chunks(e, r), drain, 0)
        first = lo_ref[e, i] + r * reg

        def per_slot(s, c):
            src = pl.ds(pl.multiple_of((s - first) * nx, 8), nx)
            dst = pl.ds(pl.multiple_of((idx_ref[e, s] - t0) * pitch, 8), nx)
            acc_ref[dst, :] = acc_ref[dst, :] + stage_ref[e, src, :]
            return c

        lax.fori_loop(first, jnp.minimum(first + reg, lo_ref[e, i + 1]), per_slot, 0)

    def prefetch(e, c):
        fetch(e, 0)
        return c

    lax.fori_loop(0, N_EXPERTS, prefetch, 0)
    hc.wait()
    for c in range(nx):
        acc_ref[pl.ds(c, tt, stride=pitch), :] = hbuf_ref[:, LANES * c:LANES * (c + 1)]

    def per_expert(e, carry):
        add_round(e, 0)

        def later(r, c):
            fetch(e, r)
            add_round(e, r)
            return c

        n_rounds = (lo_ref[e, i + 1] - lo_ref[e, i] + reg - 1) // reg
        lax.fori_loop(1, n_rounds, later, 0)
        return carry

    lax.fori_loop(0, N_EXPERTS, per_expert, 0)
    for c in range(nx):
        hbuf_ref[:, LANES * c:LANES * (c + 1)] = acc_ref[pl.ds(c, tt, stride=pitch), :]
    x = hbuf_ref[...]
    o_ref[...] = x * lax.rsqrt(jnp.mean(x * x, axis=-1, keepdims=True) + NORM_EPS) * nw_ref[...]


def _combine(idx, lo, h2, ye, norm_f_w, n_real, tt, cap_pad):
    d = h2.shape[1]
    nx = d // LANES
    assert COMBINE_REGION % COMBINE_CHUNK == 0
    return pl.pallas_call(
        functools.partial(_combine_kernel, tt=tt, nx=nx, cap_pad=cap_pad),
        grid_spec=pltpu.PrefetchScalarGridSpec(
            num_scalar_prefetch=2,
            grid=(n_real // tt,),
            in_specs=[pl.BlockSpec(memory_space=pl.ANY),
                      pl.BlockSpec(memory_space=pl.ANY),
                      pl.BlockSpec((1, d), lambda i, *_: (0, 0))],
            out_specs=pl.BlockSpec((tt, d), lambda i, *_: (i, 0)),
            scratch_shapes=[pltpu.VMEM((tt, d), F32), pltpu.VMEM((tt * (nx + ROW_PAD), LANES), F32),
                            pltpu.VMEM((N_EXPERTS, COMBINE_REGION * nx, LANES), F32),
                            pltpu.SemaphoreType.DMA((N_EXPERTS,)), pltpu.SemaphoreType.DMA]),
        out_shape=jax.ShapeDtypeStruct((n_real, d), F32),
        compiler_params=_params(1),
        name="expert_combine",
    )(idx, lo, h2, ye, norm_f_w.reshape(1, d).astype(F32))


def _layer(hp, seq_len, norm1_w, w_in, conv_w, a_log_fwd, a_log_bwd, dt_bias_fwd, dt_bias_bwd, out_norm_w,
           w_branch_a, attn_sink, w_branch_b, w_out, norm2_w, w_router, w_gate, w_up, w_down):
    lp, d = hp.shape
    main_w = 2 * DN_QK + 2 * DN_VW
    gate_w = 4 * DN_HEADS
    rest_w = SWA_QW + 2 * SWA_KVW + 2 * d
    n = _rmsnorm(hp, norm1_w, BF16)
    w_t = jnp.swapaxes(w_in, 0, 1)
    proj_main = _matmul(n, w_t, main_w, 0, BF16, w_rows=True, name="in_proj_main")
    gates = _matmul(n, w_t, LANES, main_w, F32, w_rows=True, name="in_proj_gates")
    rest = _matmul(n, w_t, rest_w, main_w + gate_w, BF16, w_rows=True, name="in_proj_rest")

    prm = jnp.zeros((8, LANES), F32)
    prm = prm.at[0, 2 * DN_HEADS:4 * DN_HEADS].set(jnp.concatenate([a_log_fwd, a_log_bwd]).astype(F32))
    prm = prm.at[1, 2 * DN_HEADS:4 * DN_HEADS].set(jnp.concatenate([dt_bias_fwd, dt_bias_bwd]).astype(F32))
    conv_f = conv_w.astype(F32)
    o_f = _deltanet(proj_main, conv_f, gates, prm, seq_len, backward=False)
    o_r = _deltanet(proj_main, conv_f, gates, prm, seq_len, backward=True)
    o_a = _gated_norm(o_f, o_r, proj_main, out_norm_w)
    o_b = _window_attention(rest, attn_sink, seq_len)

    mixed = _branch_merge(o_a, o_b, w_branch_a, w_branch_b, rest, d)
    h2 = _matmul(mixed, w_out, d, 0, F32, res=hp, name="out_proj")

    cap = EC_CAPACITY * seq_len // N_EXPERTS
    cap_pad = -(-(cap + COMBINE_CHUNK) // 16) * 16
    xg, afft = _router(h2, norm2_w, w_router, seq_len)
    n_real = seq_len - N_META
    tt = _pick(n_real, TOKEN_TILE, LANES)
    slot_pad = -(-cap_pad // LANES) * LANES
    idx, lo = _select(afft, cap, slot_pad, tt, n_real // tt)
    xe = _dispatch(xg.reshape(-1, LANES), idx, cap, cap_pad, d // LANES + ROW_PAD)
    hid, aff = _ffn_a(xe, w_gate, w_up, cap_pad, d)
    ye = _ffn_b(hid, w_down, aff, cap_pad, d)
    return functools.partial(_combine, idx, lo, h2, ye.reshape(-1, LANES), n_real=n_real, tt=tt, cap_pad=cap_pad)


def kernel(x, meta_tokens, norm1_w, w_in, conv_w, a_log_fwd, a_log_bwd, dt_bias_fwd, dt_bias_bwd, out_norm_w,
           w_branch_a, attn_sink, w_branch_b, w_out, norm2_w, w_router, w_gate, w_up, w_down, norm_f_w):
    batch, seq, d = x.shape
    depth = norm1_w.shape[0]
    assert depth == 1, "the final norm is fused into the last layer's expert combine"
    seq_len = N_META + seq
    lp = -(-seq_len // TOKEN_TILE) * TOKEN_TILE
    outs = []
    for b in range(batch):
        h = jnp.concatenate([meta_tokens.astype(x.dtype), x[b]], axis=0)
        hp = jnp.pad(h, ((0, lp - seq_len), (0, 0)))
        combine = _layer(hp, seq_len, norm1_w[0], w_in[0], conv_w[0], a_log_fwd[0], a_log_bwd[0],
                         dt_bias_fwd[0], dt_bias_bwd[0], out_norm_w[0], w_branch_a[0], attn_sink[0],
                         w_branch_b[0], w_out[0], norm2_w[0], w_router[0], w_gate[0], w_up[0], w_down[0])
        outs.append(combine(norm_f_w=norm_f_w))
    return jnp.stack(outs, axis=0)
```

```python
import functools

import jax
import jax.numpy as jnp
from jax import lax
from jax.experimental import pallas as pl
from jax.experimental.pallas import tpu as pltpu

F32 = jnp.float32
BF16 = jnp.bfloat16

N_META = 16
NORM_EPS = 1e-6
DN_HEADS = 16
DN_DK = 128
DN_DV = 128
SWA_HQ = 16
SWA_HKV = 4
SWA_D = 128
SWA_WINDOW = 128
N_EXPERTS = 16
EC_CAPACITY = 2

LANES = 128
TOKEN_TILE = 256
DN_CHUNK = 128
DN_HEADS_PER_STEP = 8
ATTN_TILE = 128
ROW_PAD = 8
DMA_ISSUE_UNROLL = 6
COMBINE_CHUNK = 8
COMBINE_REGION = 64
VMEM_LIMIT = 56 * 1024 * 1024

DN_QK = DN_HEADS * DN_DK
DN_VW = DN_HEADS * DN_DV
SWA_QW = SWA_HQ * SWA_D
SWA_KVW = SWA_HKV * SWA_D


def _params(n_grid):
    return pltpu.CompilerParams(dimension_semantics=("arbitrary",) * n_grid, vmem_limit_bytes=VMEM_LIMIT)


def _pick(n, target, mult):
    best = None
    for t in range(mult, min(n, target) + 1, mult):
        if n % t == 0:
            best = t
    assert best is not None, (n, target, mult)
    return best


def _sigmoid(x):
    return 1.0 / (1.0 + jnp.exp(-x))


def _dot(a, b):
    return jnp.dot(a, b, preferred_element_type=F32)


def _dot_nt(a, b):
    return lax.dot_general(a, b, (((1,), (1,)), ((), ())), preferred_element_type=F32)


def _rms_kernel(x_ref, w_ref, o_ref):
    x = x_ref[...]
    ms = jnp.mean(x * x, axis=-1, keepdims=True)
    o_ref[...] = (x * lax.rsqrt(ms + NORM_EPS) * w_ref[...]).astype(o_ref.dtype)


def _rmsnorm(x, w, out_dtype):
    m, d = x.shape
    tr = _pick(m, 256, 16)
    return pl.pallas_call(
        _rms_kernel,
        grid=(m // tr,),
        in_specs=[pl.BlockSpec((tr, d), lambda i: (i, 0)), pl.BlockSpec((1, d), lambda i: (0, 0))],
        out_specs=pl.BlockSpec((tr, d), lambda i: (i, 0)),
        out_shape=jax.ShapeDtypeStruct((m, d), out_dtype),
        compiler_params=_params(1),
        name="rmsnorm",
    )(x, w.reshape(1, d).astype(F32))


def _mm_kernel(*refs, has_res, w_rows):
    if has_res:
        a_ref, w_ref, r_ref, o_ref, wb_ref = refs
    else:
        a_ref, w_ref, o_ref, wb_ref = refs

    @pl.when(pl.program_id(1) == 0)
    def _():
        w = w_ref[...]
        wb_ref[...] = (jnp.transpose(w) if w_rows else w).astype(BF16)

    acc = _dot(a_ref[...], wb_ref[...])
    if has_res:
        acc = acc + r_ref[...]
    o_ref[...] = acc.astype(o_ref.dtype)


def _matmul(a, w, n_cols, off, out_dtype, res=None, w_rows=False, name="matmul"):
    m, k = a.shape
    tn = _pick(n_cols, 512, LANES)
    tm = _pick(m, 1408 if res is None else 768, 16)
    if w_rows:
        if off % tn == 0:
            w_spec = pl.BlockSpec((tn, k), lambda n, i: (off // tn + n, 0))
        else:
            assert off % 8 == 0
            w_spec = pl.BlockSpec((pl.Element(tn), pl.Element(k)), lambda n, i: (pl.multiple_of(off + n * tn, 8), 0))
    else:
        assert off % tn == 0
        w_spec = pl.BlockSpec((k, tn), lambda n, i: (0, off // tn + n))
    in_specs = [pl.BlockSpec((tm, k), lambda n, i: (i, 0)), w_spec]
    args = [a, w]
    if res is not None:
        in_specs.append(pl.BlockSpec((tm, tn), lambda n, i: (i, n)))
        args.append(res)
    return pl.pallas_call(
        functools.partial(_mm_kernel, has_res=res is not None, w_rows=w_rows),
        grid=(n_cols // tn, m // tm),
        in_specs=in_specs,
        out_specs=pl.BlockSpec((tm, tn), lambda n, i: (i, n)),
        out_shape=jax.ShapeDtypeStruct((m, n_cols), out_dtype),
        scratch_shapes=[pltpu.VMEM((k, tn), BF16)],
        compiler_params=_params(2),
        name=name,
    )(*args)


def _dn_kernel(q_ref, qp_ref, qn_ref, k_ref, kp_ref, kn_ref, v_ref, vp_ref, vn_ref, cq_ref, ck_ref, cv_ref,
               sm_ref, prm_ref, o_ref, s_ref, *, seq_len, tl, backward):
    hg = pl.program_id(0)
    i = pl.program_id(1)
    nt = pl.num_programs(1)
    ti = nt - 1 - i if backward else i
    cs = DN_CHUNK
    cs_shift = cs.bit_length() - 1
    hpb = DN_HEADS_PER_STEP

    @pl.when(i == 0)
    def _():
        s_ref[...] = jnp.zeros_like(s_ref)

    row = lax.broadcasted_iota(jnp.int32, (tl, 1), 0)
    validf = ((ti * tl + row) < seq_len).astype(F32)
    has_prev = (ti > 0).astype(F32)
    has_next = (ti < nt - 1).astype(F32)

    def conv_silu(x_ref, p_ref, n_ref, c_ref):
        x = x_ref[...].astype(F32)
        prev = p_ref[...].astype(F32)[15:16, :] * has_prev
        nxt = n_ref[...].astype(F32)[0:1, :] * has_next
        xm1 = jnp.where(row == 0, prev, pltpu.roll(x, 1, 0))
        xp1 = jnp.where(row == tl - 1, nxt, pltpu.roll(x, tl - 1, 0))
        c = c_ref[...]
        y = xm1 * c[0:1] + x * c[1:2] + xp1 * c[2:3]
        return y * _sigmoid(y)

    q_all = conv_silu(q_ref, qp_ref, qn_ref, cq_ref)
    k_all = conv_silu(k_ref, kp_ref, kn_ref, ck_ref)
    v_all = conv_silu(v_ref, vp_ref, vn_ref, cv_ref)

    lane = lax.broadcasted_iota(jnp.int32, (1, LANES), 1)
    sm = sm_ref[...]
    prm = prm_ref[...]
    z = sm + prm[1:2]
    softplus = jnp.maximum(z, 0.0) + jnp.log1p(jnp.exp(-jnp.abs(z)))
    beta_all = _sigmoid(sm) * validf
    g_all = -jnp.exp(prm[0:1]) * softplus * validf

    r2 = lax.broadcasted_iota(jnp.int32, (tl, tl), 0)
    c2 = lax.broadcasted_iota(jnp.int32, (tl, tl), 1)
    incl2 = ((r2 >> cs_shift) == (c2 >> cs_shift)) & ((r2 <= c2) if backward else (r2 >= c2))
    tri = jnp.where(incl2, 1.0, 0.0).astype(BF16)
    g1 = g_all.astype(BF16)
    rem = g_all - g1.astype(F32)
    g2 = rem.astype(BF16)
    g3 = (rem - g2.astype(F32)).astype(BF16)
    gc_all = _dot(tri, g1) + _dot(tri, g2) + _dot(tri, g3)

    r = lax.broadcasted_iota(jnp.int32, (cs, cs), 0)
    c = lax.broadcasted_iota(jnp.int32, (cs, cs), 1)
    incl = (r <= c) if backward else (r >= c)
    strict = (r < c) if backward else (r > c)
    levels = []
    b = 1
    while b < cs:
        sh = b.bit_length() - 1
        levels.append(((r >> (sh + 1)) == (c >> (sh + 1))) & ((r >> sh) != (c >> sh)))
        b *= 2

    chunk_order = range(tl // cs - 1, -1, -1) if backward else range(tl // cs)
    streams = []
    for j in range(hpb):
        hs = slice(DN_DK * j, DN_DK * (j + 1))
        q = q_all[:, hs]
        q = q * lax.rsqrt(jnp.sum(q * q, axis=-1, keepdims=True) + NORM_EPS) * (DN_DK ** -0.5) * validf
        k = k_all[:, hs]
        k = k * lax.rsqrt(jnp.sum(k * k, axis=-1, keepdims=True) + NORM_EPS) * validf
        v = v_all[:, hs] * validf
        col = (DN_HEADS if backward else 0) + hg * hpb + j
        beta = jnp.sum(jnp.where(lane == col, beta_all, 0.0), axis=-1, keepdims=True)
        g = jnp.sum(jnp.where(lane == 2 * DN_HEADS + col, g_all, 0.0), axis=-1, keepdims=True)
        gcum = jnp.sum(jnp.where(lane == 2 * DN_HEADS + col, gc_all, 0.0), axis=-1, keepdims=True)
        for ci in chunk_order:
            rs = slice(ci * cs, (ci + 1) * cs)
            qc, kc, vc, bc = q[rs], k[rs], v[rs], beta[rs]
            gtot = jnp.sum(g[rs], axis=0, keepdims=True)
            gb = jnp.broadcast_to(gcum[rs], (cs, LANES))
            gj = jnp.transpose(gb)[0:1, :]
            decay = jnp.where(incl, jnp.exp(jnp.where(incl, gb[:, 0:1] - gj, 0.0)), 0.0)
            kb = kc * bc
            k16 = kc.astype(BF16)
            egc = jnp.exp(gb)
            streams.append(dict(
                j=j, rs=rs, hs=hs, gtot=gtot,
                m=jnp.where(strict, _dot_nt(kb.astype(BF16), k16) * decay, 0.0),
                qk=(_dot_nt(qc.astype(BF16), k16) * decay).astype(BF16),
                rhs=jnp.concatenate([vc * bc, kb * egc], axis=1),
                q_dec=qc * egc,
                k_dec_t=jnp.transpose(kc * jnp.exp(gtot - gb)).astype(BF16)))

    for st in streams:
        st["e"] = -jnp.where(levels[0], st["m"], 0.0)
    for mask in levels[1:]:
        for st in streams:
            lb = jnp.where(mask, st["m"], 0.0)
            st["lb"] = lb
            st["y"] = lb + _dot(lb.astype(BF16), st["e"].astype(BF16))
        for st in streams:
            st["e"] = st["e"] - st["y"] - _dot(st["e"].astype(BF16), st["y"].astype(BF16))
    for st in streams:
        st["uw"] = st["rhs"] + _dot(st["e"].astype(BF16), st["rhs"].astype(BF16))

    state = [s_ref[j] for j in range(hpb)]
    for step in range(tl // cs):
        cur = [streams[j * (tl // cs) + step] for j in range(hpb)]
        for st in cur:
            wq = jnp.concatenate([st["uw"][:, DN_DV:], st["q_dec"]], axis=0).astype(BF16)
            st["wqs"] = _dot(wq, state[st["j"]].astype(BF16))
        for st in cur:
            v16 = (st["uw"][:, :DN_DV] - st["wqs"][:cs]).astype(BF16)
            o_ref[st["rs"], st["hs"]] = st["wqs"][cs:] + _dot(st["qk"], v16)
            state[st["j"]] = state[st["j"]] * jnp.exp(st["gtot"]) + _dot(st["k_dec_t"], v16)
    for j in range(hpb):
        s_ref[j] = state[j]


def _deltanet(proj_main, conv_w, gates, prm, seq_len, backward):
    lp = proj_main.shape[0]
    tl = TOKEN_TILE
    nt = lp // tl
    hb = tl // 16
    nhb = lp // 16
    hpb = DN_HEADS_PER_STEP
    wb = hpb * DN_DK
    ngroups = DN_HEADS // hpb

    def tile(i):
        return nt - 1 - i if backward else i

    def main_spec(off):
        return pl.BlockSpec((tl, wb), lambda h, i: (tile(i), off + h))

    def prev_spec(off):
        return pl.BlockSpec((16, wb), lambda h, i: (jnp.maximum(tile(i) * hb - 1, 0), off + h))

    def next_spec(off):
        return pl.BlockSpec((16, wb), lambda h, i: (jnp.minimum((tile(i) + 1) * hb, nhb - 1), off + h))

    def conv_spec(off):
        return pl.BlockSpec((3, wb), lambda h, i: (0, off + h))

    in_specs = []
    args = []
    for off in (0, ngroups, 2 * ngroups):
        in_specs += [main_spec(off), prev_spec(off), next_spec(off)]
        args += [proj_main] * 3
    in_specs += [conv_spec(0), conv_spec(ngroups), conv_spec(2 * ngroups)]
    args += [conv_w] * 3
    in_specs += [pl.BlockSpec((tl, LANES), lambda h, i: (tile(i), 0)),
                 pl.BlockSpec((8, LANES), lambda h, i: (0, 0))]
    args += [gates, prm]
    return pl.pallas_call(
        functools.partial(_dn_kernel, seq_len=seq_len, tl=tl, backward=backward),
        grid=(ngroups, nt),
        in_specs=in_specs,
        out_specs=pl.BlockSpec((tl, wb), lambda h, i: (tile(i), h)),
        out_shape=jax.ShapeDtypeStruct((lp, DN_VW), F32),
        scratch_shapes=[pltpu.VMEM((hpb, DN_DK, DN_DV), F32)],
        compiler_params=_params(2),
        name="deltanet_bwd" if backward else "deltanet_fwd",
    )(*args)


def _gnorm_kernel(of_ref, ob_ref, z_ref, w_ref, o_ref, *, nh):
    for j in range(nh):
        sl = slice(DN_DV * j, DN_DV * (j + 1))
        o = of_ref[:, sl] + ob_ref[:, sl]
        o = o * lax.rsqrt(jnp.mean(o * o, axis=-1, keepdims=True) + NORM_EPS) * w_ref[...]
        z = z_ref[:, sl].astype(F32)
        o_ref[:, sl] = (o * (z * _sigmoid(z))).astype(o_ref.dtype)


def _gated_norm(o_f, o_b, proj_main, out_norm_w):
    lp = proj_main.shape[0]
    nh = 4
    wb = nh * DN_DV
    tr = _pick(lp, 768, 16)
    zoff = (2 * DN_QK + DN_VW) // wb
    return pl.pallas_call(
        functools.partial(_gnorm_kernel, nh=nh),
        grid=(lp // tr, DN_VW // wb),
        in_specs=[pl.BlockSpec((tr, wb), lambda i, j: (i, j)),
                  pl.BlockSpec((tr, wb), lambda i, j: (i, j)),
                  pl.BlockSpec((tr, wb), lambda i, j: (i, zoff + j)),
                  pl.BlockSpec((1, DN_DV), lambda i, j: (0, 0))],
        out_specs=pl.BlockSpec((tr, wb), lambda i, j: (i, j)),
        out_shape=jax.ShapeDtypeStruct((lp, DN_VW), BF16),
        compiler_params=_params(2),
        name="gated_norm",
    )(o_f, o_b, proj_main, out_norm_w.reshape(1, DN_DV).astype(F32))


def _attn_kernel(q_ref, k_ref, v_ref, prm_ref, o_ref, *, seq_len, tq, lp):
    groups = SWA_HQ // SWA_HKV
    heads = range(groups)
    prm = prm_ref[0]
    slopes = [prm[g:g + 1, 0:1] for g in heads]
    sinks = [prm[groups + g:groups + g + 1, 0:1] for g in heads]
    scale = SWA_D ** -0.5
    k_meta = k_ref[0:tq, :]
    v_meta = v_ref[0:tq, :]
    rq = lax.broadcasted_iota(jnp.int32, (tq, 1), 0)
    ck = lax.broadcasted_iota(jnp.int32, (1, 4 * tq), 1)
    meta_part = ck < tq
    win_part = jnp.logical_not(meta_part)

    def tile(i, carry):
        q0 = pl.multiple_of(i * tq, tq)
        k0 = pl.multiple_of(jnp.clip(i * tq - tq, 0, lp - 3 * tq), tq)
        kall = jnp.concatenate([k_meta, k_ref[pl.ds(k0, 3 * tq), :]], axis=0)
        vall = jnp.concatenate([v_meta, v_ref[pl.ds(k0, 3 * tq), :]], axis=0)
        pq = q0 + rq
        pk = k0 + (ck - tq)
        real_q = pq >= N_META
        dist = pq - pk
        adist = jnp.abs(dist)
        in_win = (real_q & (adist <= SWA_WINDOW)) | (jnp.logical_not(real_q) & (-dist <= SWA_WINDOW))
        win_ok = win_part & (pk >= N_META) & (pk < seq_len) & in_win
        ok = (meta_part & (ck < N_META)) | win_ok
        bias = jnp.where(ok, jnp.where(real_q & win_part, -adist.astype(F32), 0.0), -jnp.inf)
        qs = [q_ref[pl.ds(q0, tq), SWA_D * g:SWA_D * (g + 1)] for g in heads]
        s = [_dot_nt(qs[g], kall) * scale + slopes[g] * bias for g in heads]
        m = [jnp.maximum(jnp.max(s[g], axis=-1, keepdims=True), sinks[g]) for g in heads]
        p = [jnp.exp(s[g] - m[g]) for g in heads]
        den = [jnp.sum(p[g], axis=-1, keepdims=True) + jnp.exp(sinks[g] - m[g]) for g in heads]
        o = [_dot(p[g].astype(BF16), vall) for g in heads]
        for g in heads:
            o_ref[pl.ds(q0, tq), SWA_D * g:SWA_D * (g + 1)] = (o[g] / den[g]).astype(o_ref.dtype)
        return carry

    nq = lp // tq
    lax.fori_loop(0, nq, tile, 0, unroll=2 if nq % 2 == 0 else 1)


def _window_attention(rest, attn_sink, seq_len):
    lp = rest.shape[0]
    tq = ATTN_TILE
    assert lp >= 3 * tq
    groups = SWA_HQ // SWA_HKV
    koff = SWA_QW // SWA_D
    voff = koff + SWA_HKV
    slopes = 2.0 ** (-8.0 * jnp.arange(1, SWA_HQ + 1, dtype=F32) / SWA_HQ)
    prm = jnp.concatenate([slopes.reshape(SWA_HKV, groups), attn_sink.astype(F32).reshape(SWA_HKV, groups)], axis=1)
    prm = jnp.broadcast_to(prm[:, :, None], (SWA_HKV, 2 * groups, LANES))
    qw = groups * SWA_D
    return pl.pallas_call(
        functools.partial(_attn_kernel, seq_len=seq_len, tq=tq, lp=lp),
        grid=(SWA_HKV,),
        in_specs=[pl.BlockSpec((lp, qw), lambda h: (0, h)),
                  pl.BlockSpec((lp, SWA_D), lambda h: (0, koff + h)),
                  pl.BlockSpec((lp, SWA_D), lambda h: (0, voff + h)),
                  pl.BlockSpec((1, 2 * groups, LANES), lambda h: (h, 0, 0))],
        out_specs=pl.BlockSpec((lp, qw), lambda h: (0, h)),
        out_shape=jax.ShapeDtypeStruct((lp, SWA_QW), BF16),
        compiler_params=_params(1),
        name="window_attention",
    )(rest, rest, rest, prm)


def _branch_kernel(oa_ref, ob_ref, wa_ref, wb_ref, ga_ref, gb_ref, o_ref, wa16_ref, wb16_ref):
    @pl.when(pl.program_id(1) == 0)
    def _():
        wa16_ref[...] = wa_ref[...].astype(BF16)
        wb16_ref[...] = wb_ref[...].astype(BF16)

    ya = _dot(oa_ref[...], wa16_ref[...])
    yb = _dot(ob_ref[...], wb16_ref[...])
    o = _sigmoid(ga_ref[...].astype(F32)) * ya + _sigmoid(gb_ref[...].astype(F32)) * yb
    o_ref[...] = o.astype(o_ref.dtype)


def _branch_merge(o_a, o_b, w_a, w_b, rest, d_model):
    lp = o_a.shape[0]
    tn = _pick(d_model, 512, LANES)
    tm = _pick(lp, 768, 16)
    ga_off = (SWA_QW + 2 * SWA_KVW) // tn
    gb_off = (SWA_QW + 2 * SWA_KVW + d_model) // tn
    assert (SWA_QW + 2 * SWA_KVW) % tn == 0 and d_model % tn == 0
    return pl.pallas_call(
        _branch_kernel,
        grid=(d_model // tn, lp // tm),
        in_specs=[pl.BlockSpec((tm, DN_VW), lambda n, i: (i, 0)),
                  pl.BlockSpec((tm, SWA_QW), lambda n, i: (i, 0)),
                  pl.BlockSpec((DN_VW, tn), lambda n, i: (0, n)),
                  pl.BlockSpec((SWA_QW, tn), lambda n, i: (0, n)),
                  pl.BlockSpec((tm, tn), lambda n, i: (i, ga_off + n)),
                  pl.BlockSpec((tm, tn), lambda n, i: (i, gb_off + n))],
        out_specs=pl.BlockSpec((tm, tn), lambda n, i: (i, n)),
        out_shape=jax.ShapeDtypeStruct((lp, d_model), BF16),
        scratch_shapes=[pltpu.VMEM((DN_VW, tn), BF16), pltpu.VMEM((SWA_QW, tn), BF16)],
        compiler_params=_params(2),
        name="branch_merge",
    )(o_a, o_b, w_a, w_b, rest, rest)


def _router_kernel(h_ref, w_ref, wr_ref, xg_ref, afft_ref, *, seq_len, tr, d_model):
    i = pl.program_id(0)
    x = h_ref[...]
    xn = x * lax.rsqrt(jnp.mean(x * x, axis=-1, keepdims=True) + NORM_EPS) * w_ref[...]
    logits = _dot_nt(xn.astype(BF16), wr_ref[...].astype(BF16))
    lane = lax.broadcasted_iota(jnp.int32, (1, LANES), 1)
    lm = jnp.where(lane < N_EXPERTS, logits, -jnp.inf)
    ex = jnp.exp(lm - jnp.max(lm, axis=-1, keepdims=True))
    aff = ex / jnp.sum(ex, axis=-1, keepdims=True)
    row = i * tr + lax.broadcasted_iota(jnp.int32, (tr, 1), 0)
    aff = jnp.where((row < seq_len) & (lane < N_EXPERTS), aff, -1.0)
    nx = d_model // LANES
    for c in range(nx):
        xg_ref[:, c, :] = xn[:, LANES * c:LANES * (c + 1)]
    xg_ref[:, nx, :] = aff
    xg_ref[:, nx + 1:, :] = jnp.zeros((tr, ROW_PAD - 1, LANES), F32)
    afft_ref[...] = jnp.transpose(aff)[:N_EXPERTS, :]


def _router(h2, norm2_w, w_router, seq_len):
    lp, d = h2.shape
    tr = TOKEN_TILE
    wr = jnp.zeros((LANES, d), F32).at[:N_EXPERTS].set(jnp.swapaxes(w_router, 0, 1).astype(F32))
    nc = d // LANES + ROW_PAD
    return pl.pallas_call(
        functools.partial(_router_kernel, seq_len=seq_len, tr=tr, d_model=d),
        grid=(lp // tr,),
        in_specs=[pl.BlockSpec((tr, d), lambda i: (i, 0)),
                  pl.BlockSpec((1, d), lambda i: (0, 0)),
                  pl.BlockSpec((LANES, d), lambda i: (0, 0))],
        out_specs=[pl.BlockSpec((tr, nc, LANES), lambda i: (i, 0, 0)),
                   pl.BlockSpec((N_EXPERTS, tr), lambda i: (0, i))],
        out_shape=[jax.ShapeDtypeStruct((lp, nc, LANES), F32), jax.ShapeDtypeStruct((N_EXPERTS, lp), F32)],
        compiler_params=_params(1),
        name="router",
    )(h2, norm2_w.reshape(1, d).astype(F32), wr)


def _select_kernel(afft_ref, idx_ref, lo_ref, sel_ref, csum_ref, cols_ref, *, cap, slot_pad, lp, tile, n_tiles):
    aff = afft_ref[...]
    capf = float(cap)

    def count_ge(thr):
        return jnp.sum(jnp.where(aff >= thr, 1.0, 0.0), axis=1, keepdims=True)

    def bisect(_, carry):
        lo, hi = carry
        mid = 0.5 * (lo + hi)
        ge = count_ge(mid) >= capf
        return jnp.where(ge, mid, lo), jnp.where(ge, hi, mid)

    lo0 = jnp.zeros((N_EXPERTS, 1), F32)
    hi0 = jnp.full((N_EXPERTS, 1), 2.0, F32)
    _, hi = lax.fori_loop(0, 40, bisect, (lo0, hi0))

    def refine(st):
        hi, tau, done, _ = st
        cand = jnp.max(jnp.where(aff < hi, aff, -2.0), axis=1, keepdims=True)
        found = jnp.where(count_ge(cand) >= capf, 1.0, 0.0)
        tau = jnp.where(done > 0.0, tau, cand)
        hi = jnp.where(done + found > 0.0, hi, cand)
        done = jnp.maximum(done, found)
        return hi, tau, done, jnp.sum(1.0 - done)

    zero = jnp.zeros((N_EXPERTS, 1), F32)
    _, tau, _, _ = lax.while_loop(lambda st: st[3] > 0.0, refine, (hi, zero, zero, jnp.float32(N_EXPERTS)))
    need = capf - jnp.sum(jnp.where(aff > tau, 1.0, 0.0), axis=1, keepdims=True)

    ra = lax.broadcasted_iota(jnp.int32, (LANES, LANES), 0)
    ca = lax.broadcasted_iota(jnp.int32, (LANES, LANES), 1)
    ut = jnp.where(ra <= ca, 1.0, 0.0).astype(BF16)
    carry_eq = jnp.zeros((N_EXPERTS, 1), F32)
    carry_sel = jnp.zeros((N_EXPERTS, 1), F32)
    for j in range(lp // LANES):
        sl = slice(LANES * j, LANES * (j + 1))
        bj = aff[:, sl]
        eqf = jnp.where(bj == tau, 1.0, 0.0)
        tie_rank = _dot(eqf.astype(BF16), ut) + carry_eq - eqf
        carry_eq = carry_eq + jnp.sum(eqf, axis=1, keepdims=True)
        self = jnp.where((bj > tau) | ((bj == tau) & (tie_rank < need)), 1.0, 0.0)
        csum_ref[j] = _dot(self.astype(BF16), ut) + carry_sel
        carry_sel = carry_sel + jnp.sum(self, axis=1, keepdims=True)
        sel_ref[:, sl] = self

    lane = lax.broadcasted_iota(jnp.int32, (1, LANES), 1)
    sblk = 64
    cbase = lax.broadcasted_iota(jnp.int32, (sblk, 1), 0)
    cols_ref[...] = jnp.zeros_like(cols_ref)

    def per_expert(e, carry):
        def per_block(sb, carry2):
            cvals = (sb * sblk + cbase).astype(F32)

            def per_tile(j, acc):
                return acc + jnp.where(csum_ref[j, pl.ds(e, 1), :] <= cvals, 1.0, 0.0)

            n_lane_tiles = lp // LANES
            acc = lax.fori_loop(0, n_lane_tiles, per_tile, jnp.zeros((sblk, LANES), F32),
                                unroll=6 if n_lane_tiles % 6 == 0 else 1)
            rows = pl.ds(pl.multiple_of(sb * sblk, sblk), sblk)
            cols_ref[rows, :] = jnp.where(lane == e, jnp.sum(acc, axis=1, keepdims=True), cols_ref[rows, :])
            return carry2

        return lax.fori_loop(0, slot_pad // sblk, per_block, carry)

    lax.fori_loop(0, N_EXPERTS, per_expert, 0)
    idx_ref[...] = jnp.transpose(cols_ref[...])[:N_EXPERTS, :].astype(jnp.int32)

    sel = sel_ref[...]
    tok = lax.broadcasted_iota(jnp.int32, (1, lp), 1)
    table = jnp.zeros((N_EXPERTS, LANES), F32)
    for i in range(n_tiles + 1):
        below = jnp.sum(jnp.where(tok < N_META + tile * i, sel, 0.0), axis=1, keepdims=True)
        table = jnp.where(lane == i, below, table)
    lo_ref[...] = table.astype(jnp.int32)


def _select(afft, cap, slot_pad, tile, n_tiles):
    lp = afft.shape[1]
    assert n_tiles < LANES
    return pl.pallas_call(
        functools.partial(_select_kernel, cap=cap, slot_pad=slot_pad, lp=lp, tile=tile, n_tiles=n_tiles),
        out_shape=[jax.ShapeDtypeStruct((N_EXPERTS, slot_pad), jnp.int32),
                   jax.ShapeDtypeStruct((N_EXPERTS, LANES), jnp.int32)],
        scratch_shapes=[pltpu.VMEM((N_EXPERTS, lp), F32), pltpu.VMEM((lp // LANES, N_EXPERTS, LANES), F32),
                        pltpu.VMEM((slot_pad, LANES), F32)],
        compiler_params=pltpu.CompilerParams(vmem_limit_bytes=VMEM_LIMIT),
        name="expert_select",
    )(afft)


def _slab_copy(src, s_tok, dst, d_tok, sem, *, nc):
    s0 = pl.multiple_of(s_tok * nc, 8)
    d0 = pl.multiple_of(d_tok * nc, 8)
    return pltpu.make_async_copy(src.at[pl.ds(s0, nc)], dst.at[pl.ds(d0, nc)], sem)


def _dispatch_kernel(idx_ref, xg_ref, x_ref, aff_ref, slab_ref, sem, *, cap, cap_pad, nc, nx):
    copy = functools.partial(_slab_copy, nc=nc)
    e = pl.program_id(0)
    h0 = -(-(cap_pad // 2) // 16) * 16
    halves = ((0, h0, min(cap, h0)), (h0, cap_pad - h0, max(cap - h0, 0)))

    def fetch(ex, hf):
        first, _, used = halves[hf]

        def issue(s, c):
            copy(xg_ref, idx_ref[ex, first + s], slab_ref.at[hf], s, sem.at[hf]).start()
            return c

        lax.fori_loop(0, used, issue, 0, unroll=DMA_ISSUE_UNROLL if used % DMA_ISSUE_UNROLL == 0 else 1)

    def finish(hf):
        first, size, used = halves[hf]

        def drain(_, c):
            copy(xg_ref, 0, slab_ref.at[hf], 0, sem.at[hf]).wait()
            return c

        lax.fori_loop(0, used, drain, 0)
        rows = slice(first, first + size)
        for c in range(nx):
            x_ref[rows, LANES * c:LANES * (c + 1)] = slab_ref[hf, pl.ds(c, size, stride=nc), :].astype(BF16)
        aff_ref[rows, :] = slab_ref[hf, pl.ds(nx, size, stride=nc), :]

    @pl.when(e == 0)
    def _():
        for hf, (_, size, used) in enumerate(halves):
            if used < size:
                slab_ref[hf, used * nc:size * nc, :] = jnp.zeros(((size - used) * nc, LANES), F32)
        fetch(0, 0)

    fetch(e, 1)
    finish(0)

    @pl.when(e + 1 < N_EXPERTS)
    def _():
        fetch(e + 1, 0)

    finish(1)


def _dispatch(xg, idx, cap, cap_pad, d_model):
    nx = d_model // LANES
    nc = nx + ROW_PAD
    h0 = -(-(cap_pad // 2) // 16) * 16
    return pl.pallas_call(
        functools.partial(_dispatch_kernel, cap=cap, cap_pad=cap_pad, nc=nc, nx=nx),
        grid_spec=pltpu.PrefetchScalarGridSpec(
            num_scalar_prefetch=1,
            grid=(N_EXPERTS,),
            in_specs=[pl.BlockSpec(memory_space=pl.ANY)],
            out_specs=[pl.BlockSpec((cap_pad, d_model), lambda e, *_: (e, 0)),
                       pl.BlockSpec((cap_pad, LANES), lambda e, *_: (e, 0))],
            scratch_shapes=[pltpu.VMEM((2, h0 * nc, LANES), F32), pltpu.SemaphoreType.DMA((2,))]),
        out_shape=[jax.ShapeDtypeStruct((N_EXPERTS * cap_pad, d_model), BF16),
                   jax.ShapeDtypeStruct((N_EXPERTS * cap_pad, LANES), F32)],
        compiler_params=_params(1),
        name="expert_dispatch",
    )(idx, xg)


def _ffn_a_kernel(x_ref, wg_ref, wu_ref, o_ref):
    x16 = x_ref[...]
    g = _dot(x16, wg_ref[0].astype(BF16))
    u = _dot(x16, wu_ref[0].astype(BF16))
    o_ref[...] = (g * _sigmoid(g) * u).astype(o_ref.dtype)


def _ffn_a(xe, w_gate, w_up, cap_pad, d_model):
    _, _, ff = w_gate.shape
    tf = _pick(ff, 256, LANES)
    return pl.pallas_call(
        _ffn_a_kernel,
        grid=(N_EXPERTS, ff // tf),
        in_specs=[pl.BlockSpec((cap_pad, d_model), lambda e, f: (e, 0)),
                  pl.BlockSpec((1, d_model, tf), lambda e, f: (e, 0, f)),
                  pl.BlockSpec((1, d_model, tf), lambda e, f: (e, 0, f))],
        out_specs=pl.BlockSpec((cap_pad, tf), lambda e, f: (e, f)),
        out_shape=jax.ShapeDtypeStruct((N_EXPERTS * cap_pad, ff), BF16),
        compiler_params=_params(2),
        name="expert_ffn_in",
    )(xe, w_gate, w_up)


def _ffn_b_kernel(h_ref, wd_ref, aff_ref, o_ref):
    e = pl.program_id(0)
    y = _dot(h_ref[...], wd_ref[0].astype(BF16))
    lane = lax.broadcasted_iota(jnp.int32, (1, LANES), 1)
    gate = jnp.sum(jnp.where(lane == e, aff_ref[...], 0.0), axis=-1, keepdims=True)
    y = y * gate
    for c in range(y.shape[1] // LANES):
        o_ref[:, c, :] = y[:, LANES * c:LANES * (c + 1)]


def _ffn_b(hid, w_down, aff, cap_pad, d_model):
    _, ff, _ = w_down.shape
    td = _pick(d_model, 1024, 8 * LANES) if d_model % (8 * LANES) == 0 else d_model
    return pl.pallas_call(
        _ffn_b_kernel,
        grid=(N_EXPERTS, d_model // td),
        in_specs=[pl.BlockSpec((cap_pad, ff), lambda e, n: (e, 0)),
                  pl.BlockSpec((1, ff, td), lambda e, n: (e, 0, n)),
                  pl.BlockSpec((cap_pad, LANES), lambda e, n: (e, 0))],
        out_specs=pl.BlockSpec((cap_pad, td // LANES, LANES), lambda e, n: (e, n, 0)),
        out_shape=jax.ShapeDtypeStruct((N_EXPERTS * cap_pad, d_model // LANES, LANES), F32),
        compiler_params=_params(2),
        name="expert_ffn_out",
    )(hid, w_down, aff)


def _combine_kernel(idx_ref, lo_ref, h_ref, ye_ref, nw_ref, o_ref, hbuf_ref, acc_ref, stage_ref, sem, hsem, *,
                    tt, nx, cap_pad):
    i = pl.program_id(0)
    t0 = N_META + i * tt
    ch = COMBINE_CHUNK
    reg = COMBINE_REGION
    pitch = nx + ROW_PAD
    hc = pltpu.make_async_copy(h_ref.at[pl.ds(pl.multiple_of(t0, 8), tt)], hbuf_ref, hsem)
    hc.start()

    def round_chunks(e, r):
        left = lo_ref[e, i + 1] - lo_ref[e, i] - r * reg
        return (jnp.minimum(left, reg) + ch - 1) // ch

    def chunk_copy(e, r, j):
        src = pl.multiple_of((e * cap_pad + lo_ref[e, i] + r * reg + j * ch) * nx, 8)
        dst = pl.multiple_of(j * ch * nx, 8)
        return pltpu.make_async_copy(ye_ref.at[pl.ds(src, ch * nx)], stage_ref.at[e, pl.ds(dst, ch * nx)],
                                     sem.at[e])

    def fetch(e, r):
        def body(j, c):
            chunk_copy(e, r, j).start()
            return c

        lax.fori_loop(0, round_chunks(e, r), body, 0)

    def add_round(e, r):
        def drain(j, c):
            chunk_copy(e, r, j).wait()
            return c

        lax.fori_loop(0, round_chunks(e, r), drain, 0)
        first = lo_ref[e, i] + r * reg

        def per_slot(s, c):
            src = pl.ds(pl.multiple_of((s - first) * nx, 8), nx)
            dst = pl.ds(pl.multiple_of((idx_ref[e, s] - t0) * pitch, 8), nx)
            acc_ref[dst, :] = acc_ref[dst, :] + stage_ref[e, src, :]
            return c

        lax.fori_loop(first, jnp.minimum(first + reg, lo_ref[e, i + 1]), per_slot, 0)

    def prefetch(e, c):
        fetch(e, 0)
        return c

    lax.fori_loop(0, N_EXPERTS, prefetch, 0)
    hc.wait()
    for c in range(nx):
        acc_ref[pl.ds(c, tt, stride=pitch), :] = hbuf_ref[:, LANES * c:LANES * (c + 1)]

    def per_expert(e, carry):
        add_round(e, 0)

        def later(r, c):
            fetch(e, r)
            add_round(e, r)
            return c

        n_rounds = (lo_ref[e, i + 1] - lo_ref[e, i] + reg - 1) // reg
        lax.fori_loop(1, n_rounds, later, 0)
        return carry

    lax.fori_loop(0, N_EXPERTS, per_expert, 0)
    for c in range(nx):
        hbuf_ref[:, LANES * c:LANES * (c + 1)] = acc_ref[pl.ds(c, tt, stride=pitch), :]
    x = hbuf_ref[...]
    o_ref[...] = x * lax.rsqrt(jnp.mean(x * x, axis=-1, keepdims=True) + NORM_EPS) * nw_ref[...]


def _combine(idx, lo, h2, ye, norm_f_w, n_real, tt, cap_pad):
    d = h2.shape[1]
    nx = d // LANES
    assert COMBINE_REGION % COMBINE_CHUNK == 0
    return pl.pallas_call(
        functools.partial(_combine_kernel, tt=tt, nx=nx, cap_pad=cap_pad),
        grid_spec=pltpu.PrefetchScalarGridSpec(
            num_scalar_prefetch=2,
            grid=(n_real // tt,),
            in_specs=[pl.BlockSpec(memory_space=pl.ANY),
                      pl.BlockSpec(memory_space=pl.ANY),
                      pl.BlockSpec((1, d), lambda i, *_: (0, 0))],
            out_specs=pl.BlockSpec((tt, d), lambda i, *_: (i, 0)),
            scratch_shapes=[pltpu.VMEM((tt, d), F32), pltpu.VMEM((tt * (nx + ROW_PAD), LANES), F32),
                            pltpu.VMEM((N_EXPERTS, COMBINE_REGION * nx, LANES), F32),
                            pltpu.SemaphoreType.DMA((N_EXPERTS,)), pltpu.SemaphoreType.DMA]),
        out_shape=jax.ShapeDtypeStruct((n_real, d), F32),
        compiler_params=_params(1),
        name="expert_combine",
    )(idx, lo, h2, ye, norm_f_w.reshape(1, d).astype(F32))


def _layer(hp, seq_len, norm1_w, w_in, conv_w, a_log_fwd, a_log_bwd, dt_bias_fwd, dt_bias_bwd, out_norm_w,
           w_branch_a, attn_sink, w_branch_b, w_out, norm2_w, w_router, w_gate, w_up, w_down):
    lp, d = hp.shape
    main_w = 2 * DN_QK + 2 * DN_VW
    gate_w = 4 * DN_HEADS
    rest_w = SWA_QW + 2 * SWA_KVW + 2 * d
    n = _rmsnorm(hp, norm1_w, BF16)
    w_t = jnp.swapaxes(w_in, 0, 1)
    proj_main = _matmul(n, w_t, main_w, 0, BF16, w_rows=True, name="in_proj_main")
    gates = _matmul(n, w_t, LANES, main_w, F32, w_rows=True, name="in_proj_gates")
    rest = _matmul(n, w_t, rest_w, main_w + gate_w, BF16, w_rows=True, name="in_proj_rest")

    prm = jnp.zeros((8, LANES), F32)
    prm = prm.at[0, 2 * DN_HEADS:4 * DN_HEADS].set(jnp.concatenate([a_log_fwd, a_log_bwd]).astype(F32))
    prm = prm.at[1, 2 * DN_HEADS:4 * DN_HEADS].set(jnp.concatenate([dt_bias_fwd, dt_bias_bwd]).astype(F32))
    conv_f = conv_w.astype(F32)
    o_f = _deltanet(proj_main, conv_f, gates, prm, seq_len, backward=False)
    o_r = _deltanet(proj_main, conv_f, gates, prm, seq_len, backward=True)
    o_a = _gated_norm(o_f, o_r, proj_main, out_norm_w)
    o_b = _window_attention(rest, attn_sink, seq_len)

    mixed = _branch_merge(o_a, o_b, w_branch_a, w_branch_b, rest, d)
    h2 = _matmul(mixed, w_out, d, 0, F32, res=hp, name="out_proj")

    cap = EC_CAPACITY * seq_len // N_EXPERTS
    cap_pad = -(-(cap + COMBINE_CHUNK) // 16) * 16
    xg, afft = _router(h2, norm2_w, w_router, seq_len)
    n_real = seq_len - N_META
    tt = _pick(n_real, TOKEN_TILE, LANES)
    slot_pad = -(-cap_pad // LANES) * LANES
    idx, lo = _select(afft, cap, slot_pad, tt, n_real // tt)
    xe, aff = _dispatch(xg.reshape(-1, LANES), idx, cap, cap_pad, d)
    hid = _ffn_a(xe, w_gate, w_up, cap_pad, d)
    ye = _ffn_b(hid, w_down, aff, cap_pad, d)
    return functools.partial(_combine, idx, lo, h2, ye.reshape(-1, LANES), n_real=n_real, tt=tt, cap_pad=cap_pad)


def kernel(x, meta_tokens, norm1_w, w_in, conv_w, a_log_fwd, a_log_bwd, dt_bias_fwd, dt_bias_bwd, out_norm_w,
           w_branch_a, attn_sink, w_branch_b, w_out, norm2_w, w_router, w_gate, w_up, w_down, norm_f_w):
    batch, seq, d = x.shape
    depth = norm1_w.shape[0]
    assert depth == 1, "the final norm is fused into the last layer's expert combine"
    seq_len = N_META + seq
    lp = -(-seq_len // TOKEN_TILE) * TOKEN_TILE
    outs = []
    for b in range(batch):
        h = jnp.concatenate([meta_tokens.astype(x.dtype), x[b]], axis=0)
        hp = jnp.pad(h, ((0, lp - seq_len), (0, 0)))
        combine = _layer(hp, seq_len, norm1_w[0], w_in[0], conv_w[0], a_log_fwd[0], a_log_bwd[0],
                         dt_bias_fwd[0], dt_bias_bwd[0], out_norm_w[0], w_branch_a[0], attn_sink[0],
                         w_branch_b[0], w_out[0], norm2_w[0], w_router[0], w_gate[0], w_up[0], w_down[0])
        outs.append(combine(norm_f_w=norm_f_w))
    return jnp.stack(outs, axis=0)
```

```python
import functools

import jax
import jax.numpy as jnp
from jax import lax
from jax.experimental import pallas as pl
from jax.experimental.pallas import tpu as pltpu

F32 = jnp.float32
BF16 = jnp.bfloat16

N_META = 16
NORM_EPS = 1e-6
DN_HEADS = 16
DN_DK = 128
DN_DV = 128
SWA_HQ = 16
SWA_HKV = 4
SWA_D = 128
SWA_WINDOW = 128
N_EXPERTS = 16
EC_CAPACITY = 2

LANES = 128
TOKEN_TILE = 256
DN_CHUNK = 128
DN_HEADS_PER_STEP = 8
ATTN_TILE = 128
ATTN_TILES_PER_STEP = 1
ROW_PAD = 8
DMA_ISSUE_UNROLL = 6
COMBINE_CHUNK = 8
COMBINE_REGION = 64
VMEM_LIMIT = 56 * 1024 * 1024

DN_QK = DN_HEADS * DN_DK
DN_VW = DN_HEADS * DN_DV
SWA_QW = SWA_HQ * SWA_D
SWA_KVW = SWA_HKV * SWA_D


def _params(n_grid):
    return pltpu.CompilerParams(dimension_semantics=("arbitrary",) * n_grid, vmem_limit_bytes=VMEM_LIMIT)


def _pick(n, target, mult):
    best = None
    for t in range(mult, min(n, target) + 1, mult):
        if n % t == 0:
            best = t
    assert best is not None, (n, target, mult)
    return best


def _sigmoid(x):
    return 1.0 / (1.0 + jnp.exp(-x))


def _dot(a, b):
    return jnp.dot(a, b, preferred_element_type=F32)


def _dot_nt(a, b):
    return lax.dot_general(a, b, (((1,), (1,)), ((), ())), preferred_element_type=F32)


def _rms_kernel(x_ref, w_ref, o_ref):
    x = x_ref[...]
    ms = jnp.mean(x * x, axis=-1, keepdims=True)
    o_ref[...] = (x * lax.rsqrt(ms + NORM_EPS) * w_ref[...]).astype(o_ref.dtype)


def _rmsnorm(x, w, out_dtype):
    m, d = x.shape
    tr = _pick(m, 256, 16)
    return pl.pallas_call(
        _rms_kernel,
        grid=(m // tr,),
        in_specs=[pl.BlockSpec((tr, d), lambda i: (i, 0)), pl.BlockSpec((1, d), lambda i: (0, 0))],
        out_specs=pl.BlockSpec((tr, d), lambda i: (i, 0)),
        out_shape=jax.ShapeDtypeStruct((m, d), out_dtype),
        compiler_params=_params(1),
        name="rmsnorm",
    )(x, w.reshape(1, d).astype(F32))


def _mm_kernel(*refs, has_res, w_rows):
    if has_res:
        a_ref, w_ref, r_ref, o_ref, wb_ref = refs
    else:
        a_ref, w_ref, o_ref, wb_ref = refs

    @pl.when(pl.program_id(1) == 0)
    def _():
        w = w_ref[...]
        wb_ref[...] = (jnp.transpose(w) if w_rows else w).astype(BF16)

    acc = _dot(a_ref[...], wb_ref[...])
    if has_res:
        acc = acc + r_ref[...]
    o_ref[...] = acc.astype(o_ref.dtype)


def _matmul(a, w, n_cols, off, out_dtype, res=None, w_rows=False, name="matmul"):
    m, k = a.shape
    tn = _pick(n_cols, 512, LANES)
    tm = _pick(m, 1408 if res is None else 1056, 16)
    if w_rows:
        if off % tn == 0:
            w_spec = pl.BlockSpec((tn, k), lambda n, i: (off // tn + n, 0))
        else:
            assert off % 8 == 0
            w_spec = pl.BlockSpec((pl.Element(tn), pl.Element(k)), lambda n, i: (pl.multiple_of(off + n * tn, 8), 0))
    else:
        assert off % tn == 0
        w_spec = pl.BlockSpec((k, tn), lambda n, i: (0, off // tn + n))
    in_specs = [pl.BlockSpec((tm, k), lambda n, i: (i, 0)), w_spec]
    args = [a, w]
    if res is not None:
        in_specs.append(pl.BlockSpec((tm, tn), lambda n, i: (i, n)))
        args.append(res)
    return pl.pallas_call(
        functools.partial(_mm_kernel, has_res=res is not None, w_rows=w_rows),
        grid=(n_cols // tn, m // tm),
        in_specs=in_specs,
        out_specs=pl.BlockSpec((tm, tn), lambda n, i: (i, n)),
        out_shape=jax.ShapeDtypeStruct((m, n_cols), out_dtype),
        scratch_shapes=[pltpu.VMEM((k, tn), BF16)],
        compiler_params=_params(2),
        name=name,
    )(*args)


def _dn_prep_kernel(x_ref, p_ref, n_ref, c_ref, o_ref, *, seq_len, tl, nh, q_blocks, k_blocks):
    i = pl.program_id(0)
    j = pl.program_id(1)
    nt = pl.num_programs(0)
    row = lax.broadcasted_iota(jnp.int32, (tl, 1), 0)
    validf = ((i * tl + row) < seq_len).astype(F32)
    x = x_ref[...].astype(F32)
    prev = p_ref[...].astype(F32)[15:16, :] * (i > 0).astype(F32)
    nxt = n_ref[...].astype(F32)[0:1, :] * (i < nt - 1).astype(F32)
    xm1 = jnp.where(row == 0, prev, pltpu.roll(x, 1, 0))
    xp1 = jnp.where(row == tl - 1, nxt, pltpu.roll(x, tl - 1, 0))
    c = c_ref[...]
    y = xm1 * c[0:1] + x * c[1:2] + xp1 * c[2:3]
    y = y * _sigmoid(y)
    is_qk = (j < q_blocks + k_blocks).astype(F32)
    q_scale = jnp.where(j < q_blocks, DN_DK ** -0.5, 1.0)
    for h in range(nh):
        hs = slice(DN_DK * h, DN_DK * (h + 1))
        yh = y[:, hs]
        norm = lax.rsqrt(jnp.sum(yh * yh, axis=-1, keepdims=True) + NORM_EPS) * q_scale
        o_ref[:, hs] = (yh * (is_qk * norm + (1.0 - is_qk)) * validf).astype(o_ref.dtype)


def _dn_prep(proj_main, conv_w, seq_len):
    lp = proj_main.shape[0]
    tl = TOKEN_TILE
    nh = 8
    wb = nh * DN_DK
    width = 2 * DN_QK + DN_VW
    hb = tl // 16
    nhb = lp // 16
    return pl.pallas_call(
        functools.partial(_dn_prep_kernel, seq_len=seq_len, tl=tl, nh=nh, q_blocks=DN_QK // wb, k_blocks=DN_QK // wb),
        grid=(lp // tl, width // wb),
        in_specs=[pl.BlockSpec((tl, wb), lambda i, j: (i, j)),
                  pl.BlockSpec((16, wb), lambda i, j: (jnp.maximum(i * hb - 1, 0), j)),
                  pl.BlockSpec((16, wb), lambda i, j: (jnp.minimum((i + 1) * hb, nhb - 1), j)),
                  pl.BlockSpec((3, wb), lambda i, j: (0, j))],
        out_specs=pl.BlockSpec((tl, wb), lambda i, j: (i, j)),
        out_shape=jax.ShapeDtypeStruct((lp, width), BF16),
        compiler_params=_params(2),
        name="deltanet_prep",
    )(proj_main, proj_main, proj_main, conv_w)


def _dn_kernel(q_ref, k_ref, v_ref, sm_ref, prm_ref, o_ref, s_ref, *, seq_len, tl, backward):
    hg = pl.program_id(0)
    i = pl.program_id(1)
    nt = pl.num_programs(1)
    ti = nt - 1 - i if backward else i
    cs = DN_CHUNK
    cs_shift = cs.bit_length() - 1
    hpb = DN_HEADS_PER_STEP

    @pl.when(i == 0)
    def _():
        s_ref[...] = jnp.zeros_like(s_ref)

    row = lax.broadcasted_iota(jnp.int32, (tl, 1), 0)
    validf = ((ti * tl + row) < seq_len).astype(F32)

    lane = lax.broadcasted_iota(jnp.int32, (1, LANES), 1)
    sm = sm_ref[...]
    prm = prm_ref[...]
    z = sm + prm[1:2]
    softplus = jnp.maximum(z, 0.0) + jnp.log1p(jnp.exp(-jnp.abs(z)))
    beta_all = _sigmoid(sm) * validf
    g_all = -jnp.exp(prm[0:1]) * softplus * validf

    r2 = lax.broadcasted_iota(jnp.int32, (tl, tl), 0)
    c2 = lax.broadcasted_iota(jnp.int32, (tl, tl), 1)
    incl2 = ((r2 >> cs_shift) == (c2 >> cs_shift)) & ((r2 <= c2) if backward else (r2 >= c2))
    tri = jnp.where(incl2, 1.0, 0.0).astype(BF16)
    g1 = g_all.astype(BF16)
    rem = g_all - g1.astype(F32)
    g2 = rem.astype(BF16)
    g3 = (rem - g2.astype(F32)).astype(BF16)
    gc_all = _dot(tri, g1) + _dot(tri, g2) + _dot(tri, g3)

    r = lax.broadcasted_iota(jnp.int32, (cs, cs), 0)
    c = lax.broadcasted_iota(jnp.int32, (cs, cs), 1)
    incl = (r <= c) if backward else (r >= c)
    strict = (r < c) if backward else (r > c)
    levels = []
    b = 1
    while b < cs:
        sh = b.bit_length() - 1
        levels.append(((r >> (sh + 1)) == (c >> (sh + 1))) & ((r >> sh) != (c >> sh)))
        b *= 2

    chunk_order = range(tl // cs - 1, -1, -1) if backward else range(tl // cs)
    streams = []
    for j in range(hpb):
        hs = slice(DN_DK * j, DN_DK * (j + 1))
        q = q_ref[:, hs].astype(F32)
        k = k_ref[:, hs].astype(F32)
        v = v_ref[:, hs].astype(F32)
        col = (DN_HEADS if backward else 0) + hg * hpb + j
        beta = jnp.sum(jnp.where(lane == col, beta_all, 0.0), axis=-1, keepdims=True)
        g = jnp.sum(jnp.where(lane == 2 * DN_HEADS + col, g_all, 0.0), axis=-1, keepdims=True)
        gcum = jnp.sum(jnp.where(lane == 2 * DN_HEADS + col, gc_all, 0.0), axis=-1, keepdims=True)
        for ci in chunk_order:
            rs = slice(ci * cs, (ci + 1) * cs)
            qc, kc, vc, bc = q[rs], k[rs], v[rs], beta[rs]
            gtot = jnp.sum(g[rs], axis=0, keepdims=True)
            gb = jnp.broadcast_to(gcum[rs], (cs, LANES))
            gj = jnp.transpose(gb)[0:1, :]
            decay = jnp.where(incl, jnp.exp(jnp.where(incl, gb[:, 0:1] - gj, 0.0)), 0.0)
            kb = kc * bc
            k16 = kc.astype(BF16)
            egc = jnp.exp(gb)
            streams.append(dict(
                j=j, rs=rs, hs=hs, gtot=gtot,
                m=jnp.where(strict, _dot_nt(kb.astype(BF16), k16) * decay, 0.0),
                qk=(_dot_nt(qc.astype(BF16), k16) * decay).astype(BF16),
                rhs=jnp.concatenate([vc * bc, kb * egc], axis=1),
                q_dec=qc * egc,
                k_dec_t=jnp.transpose(kc * jnp.exp(gtot - gb)).astype(BF16)))

    for st in streams:
        st["e"] = -jnp.where(levels[0], st["m"], 0.0)
    for mask in levels[1:]:
        for st in streams:
            lb = jnp.where(mask, st["m"], 0.0)
            st["lb"] = lb
            st["y"] = lb + _dot(lb.astype(BF16), st["e"].astype(BF16))
        for st in streams:
            st["e"] = st["e"] - st["y"] - _dot(st["e"].astype(BF16), st["y"].astype(BF16))
    for st in streams:
        st["uw"] = st["rhs"] + _dot(st["e"].astype(BF16), st["rhs"].astype(BF16))

    state = [s_ref[j] for j in range(hpb)]
    for step in range(tl // cs):
        cur = [streams[j * (tl // cs) + step] for j in range(hpb)]
        for st in cur:
            wq = jnp.concatenate([st["uw"][:, DN_DV:], st["q_dec"]], axis=0).astype(BF16)
            st["wqs"] = _dot(wq, state[st["j"]].astype(BF16))
        for st in cur:
            v16 = (st["uw"][:, :DN_DV] - st["wqs"][:cs]).astype(BF16)
            o_ref[st["rs"], st["hs"]] = st["wqs"][cs:] + _dot(st["qk"], v16)
            state[st["j"]] = state[st["j"]] * jnp.exp(st["gtot"]) + _dot(st["k_dec_t"], v16)
    for j in range(hpb):
        s_ref[j] = state[j]


def _deltanet(qkv, gates, prm, seq_len, backward):
    lp = qkv.shape[0]
    tl = TOKEN_TILE
    nt = lp // tl
    hpb = DN_HEADS_PER_STEP
    wb = hpb * DN_DK
    ngroups = DN_HEADS // hpb

    def tile(i):
        return nt - 1 - i if backward else i

    def main_spec(off):
        return pl.BlockSpec((tl, wb), lambda h, i: (tile(i), off + h))

    in_specs = [main_spec(0), main_spec(ngroups), main_spec(2 * ngroups),
                pl.BlockSpec((tl, LANES), lambda h, i: (tile(i), 0)),
                pl.BlockSpec((8, LANES), lambda h, i: (0, 0))]
    args = [qkv, qkv, qkv, gates, prm]
    return pl.pallas_call(
        functools.partial(_dn_kernel, seq_len=seq_len, tl=tl, backward=backward),
        grid=(ngroups, nt),
        in_specs=in_specs,
        out_specs=pl.BlockSpec((tl, wb), lambda h, i: (tile(i), h)),
        out_shape=jax.ShapeDtypeStruct((lp, DN_VW), F32),
        scratch_shapes=[pltpu.VMEM((hpb, DN_DK, DN_DV), F32)],
        compiler_params=_params(2),
        name="deltanet_bwd" if backward else "deltanet_fwd",
    )(*args)


def _gnorm_kernel(of_ref, ob_ref, z_ref, w_ref, o_ref, *, nh):
    for j in range(nh):
        sl = slice(DN_DV * j, DN_DV * (j + 1))
        o = of_ref[:, sl] + ob_ref[:, sl]
        o = o * lax.rsqrt(jnp.mean(o * o, axis=-1, keepdims=True) + NORM_EPS) * w_ref[...]
        z = z_ref[:, sl].astype(F32)
        o_ref[:, sl] = (o * (z * _sigmoid(z))).astype(o_ref.dtype)


def _gated_norm(o_f, o_b, proj_main, out_norm_w):
    lp = proj_main.shape[0]
    nh = 4
    wb = nh * DN_DV
    tr = _pick(lp, 768, 16)
    zoff = (2 * DN_QK + DN_VW) // wb
    return pl.pallas_call(
        functools.partial(_gnorm_kernel, nh=nh),
        grid=(lp // tr, DN_VW // wb),
        in_specs=[pl.BlockSpec((tr, wb), lambda i, j: (i, j)),
                  pl.BlockSpec((tr, wb), lambda i, j: (i, j)),
                  pl.BlockSpec((tr, wb), lambda i, j: (i, zoff + j)),
                  pl.BlockSpec((1, DN_DV), lambda i, j: (0, 0))],
        out_specs=pl.BlockSpec((tr, wb), lambda i, j: (i, j)),
        out_shape=jax.ShapeDtypeStruct((lp, DN_VW), BF16),
        compiler_params=_params(2),
        name="gated_norm",
    )(o_f, o_b, proj_main, out_norm_w.reshape(1, DN_DV).astype(F32))


def _attn_kernel(q_ref, k_ref, v_ref, prm_ref, o_ref, *, seq_len, tq, lp):
    groups = SWA_HQ // SWA_HKV
    heads = range(groups)
    prm = prm_ref[0]
    slopes = [prm[g:g + 1, 0:1] for g in heads]
    sinks = [prm[groups + g:groups + g + 1, 0:1] for g in heads]
    scale = SWA_D ** -0.5
    k_meta = k_ref[0:tq, :]
    v_meta = v_ref[0:tq, :]
    rq = lax.broadcasted_iota(jnp.int32, (tq, 1), 0)
    ck = lax.broadcasted_iota(jnp.int32, (1, 4 * tq), 1)
    meta_part = ck < tq
    win_part = jnp.logical_not(meta_part)

    def tile_inputs(i):
        q0 = pl.multiple_of(i * tq, tq)
        k0 = pl.multiple_of(jnp.clip(i * tq - tq, 0, lp - 3 * tq), tq)
        kall = jnp.concatenate([k_meta, k_ref[pl.ds(k0, 3 * tq), :]], axis=0)
        vall = jnp.concatenate([v_meta, v_ref[pl.ds(k0, 3 * tq), :]], axis=0)
        pq = q0 + rq
        pk = k0 + (ck - tq)
        real_q = pq >= N_META
        dist = pq - pk
        adist = jnp.abs(dist)
        in_win = (real_q & (adist <= SWA_WINDOW)) | (jnp.logical_not(real_q) & (-dist <= SWA_WINDOW))
        win_ok = win_part & (pk >= N_META) & (pk < seq_len) & in_win
        ok = (meta_part & (ck < N_META)) | win_ok
        bias = jnp.where(ok, jnp.where(real_q & win_part, -adist.astype(F32), 0.0), -jnp.inf)
        return q0, kall, vall, bias

    per_step = ATTN_TILES_PER_STEP if (lp // tq) % ATTN_TILES_PER_STEP == 0 else 1

    def step(it, carry):
        tiles = [tile_inputs(it * per_step + t) for t in range(per_step)]
        streams = [(q0, kall, vall, bias, g) for (q0, kall, vall, bias) in tiles for g in heads]
        s = [_dot_nt(q_ref[pl.ds(q0, tq), SWA_D * g:SWA_D * (g + 1)], kall) * scale + slopes[g] * bias
             for (q0, kall, _, bias, g) in streams]
        m = [jnp.maximum(jnp.max(sv, axis=-1, keepdims=True), sinks[st[4]]) for sv, st in zip(s, streams)]
        p = [jnp.exp(sv - mv) for sv, mv in zip(s, m)]
        den = [jnp.sum(pv, axis=-1, keepdims=True) + jnp.exp(sinks[st[4]] - mv) for pv, mv, st in zip(p, m, streams)]
        o = [_dot(pv.astype(BF16), st[2]) for pv, st in zip(p, streams)]
        for ov, dv, (q0, _, _, _, g) in zip(o, den, streams):
            o_ref[pl.ds(q0, tq), SWA_D * g:SWA_D * (g + 1)] = (ov / dv).astype(o_ref.dtype)
        return carry

    n_steps = lp // tq // per_step
    lax.fori_loop(0, n_steps, step, 0, unroll=2 if n_steps % 2 == 0 else 1)


def _window_attention(rest, attn_sink, seq_len):
    lp = rest.shape[0]
    tq = ATTN_TILE
    assert lp >= 3 * tq
    groups = SWA_HQ // SWA_HKV
    koff = SWA_QW // SWA_D
    voff = koff + SWA_HKV
    slopes = 2.0 ** (-8.0 * jnp.arange(1, SWA_HQ + 1, dtype=F32) / SWA_HQ)
    prm = jnp.concatenate([slopes.reshape(SWA_HKV, groups), attn_sink.astype(F32).reshape(SWA_HKV, groups)], axis=1)
    prm = jnp.broadcast_to(prm[:, :, None], (SWA_HKV, 2 * groups, LANES))
    qw = groups * SWA_D
    return pl.pallas_call(
        functools.partial(_attn_kernel, seq_len=seq_len, tq=tq, lp=lp),
        grid=(SWA_HKV,),
        in_specs=[pl.BlockSpec((lp, qw), lambda h: (0, h)),
                  pl.BlockSpec((lp, SWA_D), lambda h: (0, koff + h)),
                  pl.BlockSpec((lp, SWA_D), lambda h: (0, voff + h)),
                  pl.BlockSpec((1, 2 * groups, LANES), lambda h: (h, 0, 0))],
        out_specs=pl.BlockSpec((lp, qw), lambda h: (0, h)),
        out_shape=jax.ShapeDtypeStruct((lp, SWA_QW), BF16),
        compiler_params=_params(1),
        name="window_attention",
    )(rest, rest, rest, prm)


def _branch_kernel(oa_ref, ob_ref, wa_ref, wb_ref, ga_ref, gb_ref, o_ref, wa16_ref, wb16_ref):
    @pl.when(pl.program_id(1) == 0)
    def _():
        wa16_ref[...] = wa_ref[...].astype(BF16)
        wb16_ref[...] = wb_ref[...].astype(BF16)

    ya = _dot(oa_ref[...], wa16_ref[...])
    yb = _dot(ob_ref[...], wb16_ref[...])
    o = _sigmoid(ga_ref[...].astype(F32)) * ya + _sigmoid(gb_ref[...].astype(F32)) * yb
    o_ref[...] = o.astype(o_ref.dtype)


def _branch_merge(o_a, o_b, w_a, w_b, rest, d_model):
    lp = o_a.shape[0]
    tn = _pick(d_model, 512, LANES)
    tm = _pick(lp, 1056, 16)
    ga_off = (SWA_QW + 2 * SWA_KVW) // tn
    gb_off = (SWA_QW + 2 * SWA_KVW + d_model) // tn
    assert (SWA_QW + 2 * SWA_KVW) % tn == 0 and d_model % tn == 0
    return pl.pallas_call(
        _branch_kernel,
        grid=(d_model // tn, lp // tm),
        in_specs=[pl.BlockSpec((tm, DN_VW), lambda n, i: (i, 0)),
                  pl.BlockSpec((tm, SWA_QW), lambda n, i: (i, 0)),
                  pl.BlockSpec((DN_VW, tn), lambda n, i: (0, n)),
                  pl.BlockSpec((SWA_QW, tn), lambda n, i: (0, n)),
                  pl.BlockSpec((tm, tn), lambda n, i: (i, ga_off + n)),
                  pl.BlockSpec((tm, tn), lambda n, i: (i, gb_off + n))],
        out_specs=pl.BlockSpec((tm, tn), lambda n, i: (i, n)),
        out_shape=jax.ShapeDtypeStruct((lp, d_model), BF16),
        scratch_shapes=[pltpu.VMEM((DN_VW, tn), BF16), pltpu.VMEM((SWA_QW, tn), BF16)],
        compiler_params=_params(2),
        name="branch_merge",
    )(o_a, o_b, w_a, w_b, rest, rest)


def _router_kernel(h_ref, w_ref, wr_ref, xg_ref, afft_ref, *, seq_len, tr, d_model):
    i = pl.program_id(0)
    x = h_ref[...]
    xn = x * lax.rsqrt(jnp.mean(x * x, axis=-1, keepdims=True) + NORM_EPS) * w_ref[...]
    logits = _dot_nt(xn.astype(BF16), wr_ref[...].astype(BF16))
    lane = lax.broadcasted_iota(jnp.int32, (1, LANES), 1)
    lm = jnp.where(lane < N_EXPERTS, logits, -jnp.inf)
    ex = jnp.exp(lm - jnp.max(lm, axis=-1, keepdims=True))
    aff = ex / jnp.sum(ex, axis=-1, keepdims=True)
    row = i * tr + lax.broadcasted_iota(jnp.int32, (tr, 1), 0)
    aff = jnp.where((row < seq_len) & (lane < N_EXPERTS), aff, -1.0)
    nx = d_model // LANES
    for c in range(nx):
        xg_ref[:, c, :] = xn[:, LANES * c:LANES * (c + 1)]
    xg_ref[:, nx, :] = aff
    xg_ref[:, nx + 1:, :] = jnp.zeros((tr, ROW_PAD - 1, LANES), F32)
    afft_ref[...] = jnp.transpose(aff)[:N_EXPERTS, :]


def _router(h2, norm2_w, w_router, seq_len):
    lp, d = h2.shape
    tr = TOKEN_TILE
    wr = jnp.zeros((LANES, d), F32).at[:N_EXPERTS].set(jnp.swapaxes(w_router, 0, 1).astype(F32))
    nc = d // LANES + ROW_PAD
    return pl.pallas_call(
        functools.partial(_router_kernel, seq_len=seq_len, tr=tr, d_model=d),
        grid=(lp // tr,),
        in_specs=[pl.BlockSpec((tr, d), lambda i: (i, 0)),
                  pl.BlockSpec((1, d), lambda i: (0, 0)),
                  pl.BlockSpec((LANES, d), lambda i: (0, 0))],
        out_specs=[pl.BlockSpec((tr, nc, LANES), lambda i: (i, 0, 0)),
                   pl.BlockSpec((N_EXPERTS, tr), lambda i: (0, i))],
        out_shape=[jax.ShapeDtypeStruct((lp, nc, LANES), F32), jax.ShapeDtypeStruct((N_EXPERTS, lp), F32)],
        compiler_params=_params(1),
        name="router",
    )(h2, norm2_w.reshape(1, d).astype(F32), wr)


def _select_kernel(afft_ref, idx_ref, lo_ref, sel_ref, csum_ref, cols_ref, *, cap, slot_pad, lp, tile, n_tiles):
    aff = afft_ref[...]
    capf = float(cap)

    def count_ge(thr):
        return jnp.sum(jnp.where(aff >= thr, 1.0, 0.0), axis=1, keepdims=True)

    def bisect(_, carry):
        lo, hi = carry
        mid = 0.5 * (lo + hi)
        ge = count_ge(mid) >= capf
        return jnp.where(ge, mid, lo), jnp.where(ge, hi, mid)

    lo0 = jnp.zeros((N_EXPERTS, 1), F32)
    hi0 = jnp.full((N_EXPERTS, 1), 2.0, F32)
    _, hi = lax.fori_loop(0, 40, bisect, (lo0, hi0))

    def refine(st):
        hi, tau, done, _ = st
        cand = jnp.max(jnp.where(aff < hi, aff, -2.0), axis=1, keepdims=True)
        found = jnp.where(count_ge(cand) >= capf, 1.0, 0.0)
        tau = jnp.where(done > 0.0, tau, cand)
        hi = jnp.where(done + found > 0.0, hi, cand)
        done = jnp.maximum(done, found)
        return hi, tau, done, jnp.sum(1.0 - done)

    zero = jnp.zeros((N_EXPERTS, 1), F32)
    _, tau, _, _ = lax.while_loop(lambda st: st[3] > 0.0, refine, (hi, zero, zero, jnp.float32(N_EXPERTS)))
    need = capf - jnp.sum(jnp.where(aff > tau, 1.0, 0.0), axis=1, keepdims=True)

    ra = lax.broadcasted_iota(jnp.int32, (LANES, LANES), 0)
    ca = lax.broadcasted_iota(jnp.int32, (LANES, LANES), 1)
    ut = jnp.where(ra <= ca, 1.0, 0.0).astype(BF16)
    carry_eq = jnp.zeros((N_EXPERTS, 1), F32)
    carry_sel = jnp.zeros((N_EXPERTS, 1), F32)
    for j in range(lp // LANES):
        sl = slice(LANES * j, LANES * (j + 1))
        bj = aff[:, sl]
        eqf = jnp.where(bj == tau, 1.0, 0.0)
        tie_rank = _dot(eqf.astype(BF16), ut) + carry_eq - eqf
        carry_eq = carry_eq + jnp.sum(eqf, axis=1, keepdims=True)
        self = jnp.where((bj > tau) | ((bj == tau) & (tie_rank < need)), 1.0, 0.0)
        csum_ref[j] = _dot(self.astype(BF16), ut) + carry_sel
        carry_sel = carry_sel + jnp.sum(self, axis=1, keepdims=True)
        sel_ref[:, sl] = self

    lane = lax.broadcasted_iota(jnp.int32, (1, LANES), 1)
    sblk = 64
    cbase = lax.broadcasted_iota(jnp.int32, (sblk, 1), 0)
    cols_ref[...] = jnp.zeros_like(cols_ref)

    def per_expert(e, carry):
        def per_block(sb, carry2):
            cvals = (sb * sblk + cbase).astype(F32)

            def per_tile(j, acc):
                return acc + jnp.where(csum_ref[j, pl.ds(e, 1), :] <= cvals, 1.0, 0.0)

            n_lane_tiles = lp // LANES
            acc = lax.fori_loop(0, n_lane_tiles, per_tile, jnp.zeros((sblk, LANES), F32),
                                unroll=6 if n_lane_tiles % 6 == 0 else 1)
            rows = pl.ds(pl.multiple_of(sb * sblk, sblk), sblk)
            cols_ref[rows, :] = jnp.where(lane == e, jnp.sum(acc, axis=1, keepdims=True), cols_ref[rows, :])
            return carry2

        return lax.fori_loop(0, slot_pad // sblk, per_block, carry)

    lax.fori_loop(0, N_EXPERTS, per_expert, 0)
    idx_ref[...] = jnp.transpose(cols_ref[...])[:N_EXPERTS, :].astype(jnp.int32)

    sel = sel_ref[...]
    tok = lax.broadcasted_iota(jnp.int32, (1, lp), 1)
    table = jnp.zeros((N_EXPERTS, LANES), F32)
    for i in range(n_tiles + 1):
        below = jnp.sum(jnp.where(tok < N_META + tile * i, sel, 0.0), axis=1, keepdims=True)
        table = jnp.where(lane == i, below, table)
    lo_ref[...] = table.astype(jnp.int32)


def _select(afft, cap, slot_pad, tile, n_tiles):
    lp = afft.shape[1]
    assert n_tiles < LANES
    return pl.pallas_call(
        functools.partial(_select_kernel, cap=cap, slot_pad=slot_pad, lp=lp, tile=tile, n_tiles=n_tiles),
        out_shape=[jax.ShapeDtypeStruct((N_EXPERTS, slot_pad), jnp.int32),
                   jax.ShapeDtypeStruct((N_EXPERTS, LANES), jnp.int32)],
        scratch_shapes=[pltpu.VMEM((N_EXPERTS, lp), F32), pltpu.VMEM((lp // LANES, N_EXPERTS, LANES), F32),
                        pltpu.VMEM((slot_pad, LANES), F32)],
        compiler_params=pltpu.CompilerParams(vmem_limit_bytes=VMEM_LIMIT),
        name="expert_select",
    )(afft)


def _slab_copy(src, s_tok, dst, d_tok, sem, *, nc):
    s0 = pl.multiple_of(s_tok * nc, 8)
    d0 = pl.multiple_of(d_tok * nc, 8)
    return pltpu.make_async_copy(src.at[pl.ds(s0, nc)], dst.at[pl.ds(d0, nc)], sem)


def _dispatch_kernel(idx_ref, xg_ref, x_ref, aff_ref, slab_ref, sem, *, cap, cap_pad, nc, nx):
    copy = functools.partial(_slab_copy, nc=nc)
    e = pl.program_id(0)
    h0 = -(-(cap_pad // 2) // 16) * 16
    halves = ((0, h0, min(cap, h0)), (h0, cap_pad - h0, max(cap - h0, 0)))

    def fetch(ex, hf):
        first, _, used = halves[hf]

        def issue(s, c):
            copy(xg_ref, idx_ref[ex, first + s], slab_ref.at[hf], s, sem.at[hf]).start()
            return c

        lax.fori_loop(0, used, issue, 0, unroll=DMA_ISSUE_UNROLL if used % DMA_ISSUE_UNROLL == 0 else 1)

    def finish(hf):
        first, size, used = halves[hf]

        def drain(_, c):
            copy(xg_ref, 0, slab_ref.at[hf], 0, sem.at[hf]).wait()
            return c

        lax.fori_loop(0, used, drain, 0)
        rows = slice(first, first + size)
        for c in range(nx):
            x_ref[rows, LANES * c:LANES * (c + 1)] = slab_ref[hf, pl.ds(c, size, stride=nc), :].astype(BF16)
        aff_ref[rows, :] = slab_ref[hf, pl.ds(nx, size, stride=nc), :]

    @pl.when(e == 0)
    def _():
        for hf, (_, size, used) in enumerate(halves):
            if used < size:
                slab_ref[hf, used * nc:size * nc, :] = jnp.zeros(((size - used) * nc, LANES), F32)
        fetch(0, 0)

    fetch(e, 1)
    finish(0)

    @pl.when(e + 1 < N_EXPERTS)
    def _():
        fetch(e + 1, 0)

    finish(1)


def _dispatch(xg, idx, cap, cap_pad, d_model):
    nx = d_model // LANES
    nc = nx + ROW_PAD
    h0 = -(-(cap_pad // 2) // 16) * 16
    return pl.pallas_call(
        functools.partial(_dispatch_kernel, cap=cap, cap_pad=cap_pad, nc=nc, nx=nx),
        grid_spec=pltpu.PrefetchScalarGridSpec(
            num_scalar_prefetch=1,
            grid=(N_EXPERTS,),
            in_specs=[pl.BlockSpec(memory_space=pl.ANY)],
            out_specs=[pl.BlockSpec((cap_pad, d_model), lambda e, *_: (e, 0)),
                       pl.BlockSpec((cap_pad, LANES), lambda e, *_: (e, 0))],
            scratch_shapes=[pltpu.VMEM((2, h0 * nc, LANES), F32), pltpu.SemaphoreType.DMA((2,))]),
        out_shape=[jax.ShapeDtypeStruct((N_EXPERTS * cap_pad, d_model), BF16),
                   jax.ShapeDtypeStruct((N_EXPERTS * cap_pad, LANES), F32)],
        compiler_params=_params(1),
        name="expert_dispatch",
    )(idx, xg)


def _ffn_a_kernel(x_ref, wg_ref, wu_ref, o_ref):
    x16 = x_ref[...]
    g = _dot(x16, wg_ref[0].astype(BF16))
    u = _dot(x16, wu_ref[0].astype(BF16))
    o_ref[...] = (g * _sigmoid(g) * u).astype(o_ref.dtype)


def _ffn_a(xe, w_gate, w_up, cap_pad, d_model):
    _, _, ff = w_gate.shape
    tf = _pick(ff, 256, LANES)
    return pl.pallas_call(
        _ffn_a_kernel,
        grid=(N_EXPERTS, ff // tf),
        in_specs=[pl.BlockSpec((cap_pad, d_model), lambda e, f: (e, 0)),
                  pl.BlockSpec((1, d_model, tf), lambda e, f: (e, 0, f)),
                  pl.BlockSpec((1, d_model, tf), lambda e, f: (e, 0, f))],
        out_specs=pl.BlockSpec((cap_pad, tf), lambda e, f: (e, f)),
        out_shape=jax.ShapeDtypeStruct((N_EXPERTS * cap_pad, ff), BF16),
        compiler_params=_params(2),
        name="expert_ffn_in",
    )(xe, w_gate, w_up)


def _ffn_b_kernel(h_ref, wd_ref, aff_ref, o_ref):
    e = pl.program_id(0)
    y = _dot(h_ref[...], wd_ref[0].astype(BF16))
    lane = lax.broadcasted_iota(jnp.int32, (1, LANES), 1)
    gate = jnp.sum(jnp.where(lane == e, aff_ref[...], 0.0), axis=-1, keepdims=True)
    y = y * gate
    for c in range(y.shape[1] // LANES):
        o_ref[:, c, :] = y[:, LANES * c:LANES * (c + 1)]


def _ffn_b(hid, w_down, aff, cap_pad, d_model):
    _, ff, _ = w_down.shape
    td = _pick(d_model, 1024, 8 * LANES) if d_model % (8 * LANES) == 0 else d_model
    return pl.pallas_call(
        _ffn_b_kernel,
        grid=(N_EXPERTS, d_model // td),
        in_specs=[pl.BlockSpec((cap_pad, ff), lambda e, n: (e, 0)),
                  pl.BlockSpec((1, ff, td), lambda e, n: (e, 0, n)),
                  pl.BlockSpec((cap_pad, LANES), lambda e, n: (e, 0))],
        out_specs=pl.BlockSpec((cap_pad, td // LANES, LANES), lambda e, n: (e, n, 0)),
        out_shape=jax.ShapeDtypeStruct((N_EXPERTS * cap_pad, d_model // LANES, LANES), F32),
        compiler_params=_params(2),
        name="expert_ffn_out",
    )(hid, w_down, aff)


def _combine_kernel(idx_ref, lo_ref, h_ref, ye_ref, nw_ref, o_ref, hbuf_ref, acc_ref, stage_ref, sem, hsem, *,
                    tt, nx, cap_pad):
    i = pl.program_id(0)
    t0 = N_META + i * tt
    ch = COMBINE_CHUNK
    reg = COMBINE_REGION
    pitch = nx + ROW_PAD
    hc = pltpu.make_async_copy(h_ref.at[pl.ds(pl.multiple_of(t0, 8), tt)], hbuf_ref, hsem)
    hc.start()

    def round_chunks(e, r):
        left = lo_ref[e, i + 1] - lo_ref[e, i] - r * reg
        return (jnp.minimum(left, reg) + ch - 1) // ch

    def chunk_copy(e, r, j):
        src = pl.multiple_of((e * cap_pad + lo_ref[e, i] + r * reg + j * ch) * nx, 8)
        dst = pl.multiple_of(j * ch * nx, 8)
        return pltpu.make_async_copy(ye_ref.at[pl.ds(src, ch * nx)], stage_ref.at[e, pl.ds(dst, ch * nx)],
                                     sem.at[e])

    def fetch(e, r):
        def body(j, c):
            chunk_copy(e, r, j).start()
            return c

        lax.fori_loop(0, round_chunks(e, r), body, 0)

    def add_round(e, r):
        def drain(j, c):
            chunk_copy(e, r, j).wait()
            return c

        lax.fori_loop(0, round_chunks(e, r), drain, 0)
        first = lo_ref[e, i] + r * reg

        def per_slot(s, c):
            src = pl.ds(pl.multiple_of((s - first) * nx, 8), nx)
            dst = pl.ds(pl.multiple_of((idx_ref[e, s] - t0) * pitch, 8), nx)
            acc_ref[dst, :] = acc_ref[dst, :] + stage_ref[e, src, :]
            return c

        lax.fori_loop(first, jnp.minimum(first + reg, lo_ref[e, i + 1]), per_slot, 0)

    def prefetch(e, c):
        fetch(e, 0)
        return c

    lax.fori_loop(0, N_EXPERTS, prefetch, 0)
    hc.wait()
    for c in range(nx):
        acc_ref[pl.ds(c, tt, stride=pitch), :] = hbuf_ref[:, LANES * c:LANES * (c + 1)]

    def per_expert(e, carry):
        add_round(e, 0)

        def later(r, c):
            fetch(e, r)
            add_round(e, r)
            return c

        n_rounds = (lo_ref[e, i + 1] - lo_ref[e, i] + reg - 1) // reg
        lax.fori_loop(1, n_rounds, later, 0)
        return carry

    lax.fori_loop(0, N_EXPERTS, per_expert, 0)
    for c in range(nx):
        hbuf_ref[:, LANES * c:LANES * (c + 1)] = acc_ref[pl.ds(c, tt, stride=pitch), :]
    x = hbuf_ref[...]
    o_ref[...] = x * lax.rsqrt(jnp.mean(x * x, axis=-1, keepdims=True) + NORM_EPS) * nw_ref[...]


def _combine(idx, lo, h2, ye, norm_f_w, n_real, tt, cap_pad):
    d = h2.shape[1]
    nx = d // LANES
    assert COMBINE_REGION % COMBINE_CHUNK == 0
    return pl.pallas_call(
        functools.partial(_combine_kernel, tt=tt, nx=nx, cap_pad=cap_pad),
        grid_spec=pltpu.PrefetchScalarGridSpec(
            num_scalar_prefetch=2,
            grid=(n_real // tt,),
            in_specs=[pl.BlockSpec(memory_space=pl.ANY),
                      pl.BlockSpec(memory_space=pl.ANY),
                      pl.BlockSpec((1, d), lambda i, *_: (0, 0))],
            out_specs=pl.BlockSpec((tt, d), lambda i, *_: (i, 0)),
            scratch_shapes=[pltpu.VMEM((tt, d), F32), pltpu.VMEM((tt * (nx + ROW_PAD), LANES), F32),
                            pltpu.VMEM((N_EXPERTS, COMBINE_REGION * nx, LANES), F32),
                            pltpu.SemaphoreType.DMA((N_EXPERTS,)), pltpu.SemaphoreType.DMA]),
        out_shape=jax.ShapeDtypeStruct((n_real, d), F32),
        compiler_params=_params(1),
        name="expert_combine",
    )(idx, lo, h2, ye, norm_f_w.reshape(1, d).astype(F32))


def _layer(hp, seq_len, norm1_w, w_in, conv_w, a_log_fwd, a_log_bwd, dt_bias_fwd, dt_bias_bwd, out_norm_w,
           w_branch_a, attn_sink, w_branch_b, w_out, norm2_w, w_router, w_gate, w_up, w_down):
    lp, d = hp.shape
    main_w = 2 * DN_QK + 2 * DN_VW
    gate_w = 4 * DN_HEADS
    rest_w = SWA_QW + 2 * SWA_KVW + 2 * d
    n = _rmsnorm(hp, norm1_w, BF16)
    w_t = jnp.swapaxes(w_in, 0, 1)
    proj_main = _matmul(n, w_t, main_w, 0, BF16, w_rows=True, name="in_proj_main")
    gates = _matmul(n, w_t, LANES, main_w, F32, w_rows=True, name="in_proj_gates")
    rest = _matmul(n, w_t, rest_w, main_w + gate_w, BF16, w_rows=True, name="in_proj_rest")

    prm = jnp.zeros((8, LANES), F32)
    prm = prm.at[0, 2 * DN_HEADS:4 * DN_HEADS].set(jnp.concatenate([a_log_fwd, a_log_bwd]).astype(F32))
    prm = prm.at[1, 2 * DN_HEADS:4 * DN_HEADS].set(jnp.concatenate([dt_bias_fwd, dt_bias_bwd]).astype(F32))
    qkv = _dn_prep(proj_main, conv_w.astype(F32), seq_len)
    o_f = _deltanet(qkv, gates, prm, seq_len, backward=False)
    o_r = _deltanet(qkv, gates, prm, seq_len, backward=True)
    o_a = _gated_norm(o_f, o_r, proj_main, out_norm_w)
    o_b = _window_attention(rest, attn_sink, seq_len)

    mixed = _branch_merge(o_a, o_b, w_branch_a, w_branch_b, rest, d)
    h2 = _matmul(mixed, w_out, d, 0, F32, res=hp, name="out_proj")

    cap = EC_CAPACITY * seq_len // N_EXPERTS
    cap_pad = -(-(cap + COMBINE_CHUNK) // 16) * 16
    xg, afft = _router(h2, norm2_w, w_router, seq_len)
    n_real = seq_len - N_META
    tt = _pick(n_real, TOKEN_TILE, LANES)
    slot_pad = -(-cap_pad // LANES) * LANES
    idx, lo = _select(afft, cap, slot_pad, tt, n_real // tt)
    xe, aff = _dispatch(xg.reshape(-1, LANES), idx, cap, cap_pad, d)
    hid = _ffn_a(xe, w_gate, w_up, cap_pad, d)
    ye = _ffn_b(hid, w_down, aff, cap_pad, d)
    return functools.partial(_combine, idx, lo, h2, ye.reshape(-1, LANES), n_real=n_real, tt=tt, cap_pad=cap_pad)


def kernel(x, meta_tokens, norm1_w, w_in, conv_w, a_log_fwd, a_log_bwd, dt_bias_fwd, dt_bias_bwd, out_norm_w,
           w_branch_a, attn_sink, w_branch_b, w_out, norm2_w, w_router, w_gate, w_up, w_down, norm_f_w):
    batch, seq, d = x.shape
    depth = norm1_w.shape[0]
    assert depth == 1, "the final norm is fused into the last layer's expert combine"
    seq_len = N_META + seq
    lp = -(-seq_len // TOKEN_TILE) * TOKEN_TILE
    outs = []
    for b in range(batch):
        h = jnp.concatenate([meta_tokens.astype(x.dtype), x[b]], axis=0)
        hp = jnp.pad(h, ((0, lp - seq_len), (0, 0)))
        combine = _layer(hp, seq_len, norm1_w[0], w_in[0], conv_w[0], a_log_fwd[0], a_log_bwd[0],
                         dt_bias_fwd[0], dt_bias_bwd[0], out_norm_w[0], w_branch_a[0], attn_sink[0],
                         w_branch_b[0], w_out[0], norm2_w[0], w_router[0], w_gate[0], w_up[0], w_down[0])
        outs.append(combine(norm_f_w=norm_f_w))
    return jnp.stack(outs, axis=0)
```

```python
import functools

import jax
import jax.numpy as jnp
from jax import lax
from jax.experimental import pallas as pl
from jax.experimental.pallas import tpu as pltpu

F32 = jnp.float32
BF16 = jnp.bfloat16

N_META = 16
NORM_EPS = 1e-6
DN_HEADS = 16
DN_DK = 128
DN_DV = 128
SWA_HQ = 16
SWA_HKV = 4
SWA_D = 128
SWA_WINDOW = 128
N_EXPERTS = 16
EC_CAPACITY = 2

LANES = 128
TOKEN_TILE = 256
DN_CHUNK = 128
DN_HEADS_PER_STEP = 8
ATTN_TILE = 128
ROW_PAD = 8
DMA_ISSUE_UNROLL = 6
COMBINE_CHUNK = 8
COMBINE_REGION = 64
VMEM_LIMIT = 56 * 1024 * 1024

DN_QK = DN_HEADS * DN_DK
DN_VW = DN_HEADS * DN_DV
SWA_QW = SWA_HQ * SWA_D
SWA_KVW = SWA_HKV * SWA_D


def _params(n_grid):
    return pltpu.CompilerParams(dimension_semantics=("arbitrary",) * n_grid, vmem_limit_bytes=VMEM_LIMIT)


def _pick(n, target, mult):
    best = None
    for t in range(mult, min(n, target) + 1, mult):
        if n % t == 0:
            best = t
    assert best is not None, (n, target, mult)
    return best


def _sigmoid(x):
    return 1.0 / (1.0 + jnp.exp(-x))


def _dot(a, b):
    return jnp.dot(a, b, preferred_element_type=F32)


def _dot_nt(a, b):
    return lax.dot_general(a, b, (((1,), (1,)), ((), ())), preferred_element_type=F32)


def _rms_kernel(x_ref, w_ref, o_ref):
    x = x_ref[...]
    ms = jnp.mean(x * x, axis=-1, keepdims=True)
    o_ref[...] = (x * lax.rsqrt(ms + NORM_EPS) * w_ref[...]).astype(o_ref.dtype)


def _rmsnorm(x, w, out_dtype):
    m, d = x.shape
    tr = _pick(m, 256, 16)
    return pl.pallas_call(
        _rms_kernel,
        grid=(m // tr,),
        in_specs=[pl.BlockSpec((tr, d), lambda i: (i, 0)), pl.BlockSpec((1, d), lambda i: (0, 0))],
        out_specs=pl.BlockSpec((tr, d), lambda i: (i, 0)),
        out_shape=jax.ShapeDtypeStruct((m, d), out_dtype),
        compiler_params=_params(1),
        name="rmsnorm",
    )(x, w.reshape(1, d).astype(F32))


def _mm_kernel(*refs, has_res, w_rows):
    if has_res:
        a_ref, w_ref, r_ref, o_ref, wb_ref = refs
    else:
        a_ref, w_ref, o_ref, wb_ref = refs

    @pl.when(pl.program_id(1) == 0)
    def _():
        w = w_ref[...]
        wb_ref[...] = (jnp.transpose(w) if w_rows else w).astype(BF16)

    acc = _dot(a_ref[...], wb_ref[...])
    if has_res:
        acc = acc + r_ref[...]
    o_ref[...] = acc.astype(o_ref.dtype)


def _matmul(a, w, n_cols, off, out_dtype, res=None, w_rows=False, name="matmul"):
    m, k = a.shape
    tn = _pick(n_cols, 512, LANES)
    tm = _pick(m, 1408 if res is None else 1056, 16)
    if w_rows:
        if off % tn == 0:
            w_spec = pl.BlockSpec((tn, k), lambda n, i: (off // tn + n, 0))
        else:
            assert off % 8 == 0
            w_spec = pl.BlockSpec((pl.Element(tn), pl.Element(k)), lambda n, i: (pl.multiple_of(off + n * tn, 8), 0))
    else:
        assert off % tn == 0
        w_spec = pl.BlockSpec((k, tn), lambda n, i: (0, off // tn + n))
    in_specs = [pl.BlockSpec((tm, k), lambda n, i: (i, 0)), w_spec]
    args = [a, w]
    if res is not None:
        in_specs.append(pl.BlockSpec((tm, tn), lambda n, i: (i, n)))
        args.append(res)
    return pl.pallas_call(
        functools.partial(_mm_kernel, has_res=res is not None, w_rows=w_rows),
        grid=(n_cols // tn, m // tm),
        in_specs=in_specs,
        out_specs=pl.BlockSpec((tm, tn), lambda n, i: (i, n)),
        out_shape=jax.ShapeDtypeStruct((m, n_cols), out_dtype),
        scratch_shapes=[pltpu.VMEM((k, tn), BF16)],
        compiler_params=_params(2),
        name=name,
    )(*args)


def _dn_prep_kernel(x_ref, p_ref, n_ref, c_ref, o_ref, *, seq_len, tl, nh, q_blocks, k_blocks):
    i = pl.program_id(0)
    j = pl.program_id(1)
    nt = pl.num_programs(0)
    row = lax.broadcasted_iota(jnp.int32, (tl, 1), 0)
    validf = ((i * tl + row) < seq_len).astype(F32)
    x = x_ref[...].astype(F32)
    prev = p_ref[...].astype(F32)[15:16, :] * (i > 0).astype(F32)
    nxt = n_ref[...].astype(F32)[0:1, :] * (i < nt - 1).astype(F32)
    xm1 = jnp.where(row == 0, prev, pltpu.roll(x, 1, 0))
    xp1 = jnp.where(row == tl - 1, nxt, pltpu.roll(x, tl - 1, 0))
    c = c_ref[...]
    y = xm1 * c[0:1] + x * c[1:2] + xp1 * c[2:3]
    y = y * _sigmoid(y)
    is_qk = (j < q_blocks + k_blocks).astype(F32)
    q_scale = jnp.where(j < q_blocks, DN_DK ** -0.5, 1.0)
    for h in range(nh):
        hs = slice(DN_DK * h, DN_DK * (h + 1))
        yh = y[:, hs]
        norm = lax.rsqrt(jnp.sum(yh * yh, axis=-1, keepdims=True) + NORM_EPS) * q_scale
        o_ref[:, hs] = (yh * (is_qk * norm + (1.0 - is_qk)) * validf).astype(o_ref.dtype)


def _dn_prep(proj_main, conv_w, seq_len):
    lp = proj_main.shape[0]
    tl = _pick(lp, 3 * TOKEN_TILE, TOKEN_TILE)
    nh = 8
    wb = nh * DN_DK
    width = 2 * DN_QK + DN_VW
    hb = tl // 16
    nhb = lp // 16
    return pl.pallas_call(
        functools.partial(_dn_prep_kernel, seq_len=seq_len, tl=tl, nh=nh, q_blocks=DN_QK // wb, k_blocks=DN_QK // wb),
        grid=(lp // tl, width // wb),
        in_specs=[pl.BlockSpec((tl, wb), lambda i, j: (i, j)),
                  pl.BlockSpec((16, wb), lambda i, j: (jnp.maximum(i * hb - 1, 0), j)),
                  pl.BlockSpec((16, wb), lambda i, j: (jnp.minimum((i + 1) * hb, nhb - 1), j)),
                  pl.BlockSpec((3, wb), lambda i, j: (0, j))],
        out_specs=pl.BlockSpec((tl, wb), lambda i, j: (i, j)),
        out_shape=jax.ShapeDtypeStruct((lp, width), BF16),
        compiler_params=_params(2),
        name="deltanet_prep",
    )(proj_main, proj_main, proj_main, conv_w)


def _dn_kernel(q_ref, k_ref, v_ref, sm_ref, prm_ref, o_ref, s_ref, *, seq_len, tl, backward):
    hg = pl.program_id(0)
    i = pl.program_id(1)
    nt = pl.num_programs(1)
    ti = nt - 1 - i if backward else i
    cs = DN_CHUNK
    cs_shift = cs.bit_length() - 1
    hpb = DN_HEADS_PER_STEP

    @pl.when(i == 0)
    def _():
        s_ref[...] = jnp.zeros_like(s_ref)

    row = lax.broadcasted_iota(jnp.int32, (tl, 1), 0)
    validf = ((ti * tl + row) < seq_len).astype(F32)

    lane = lax.broadcasted_iota(jnp.int32, (1, LANES), 1)
    sm = sm_ref[...]
    prm = prm_ref[...]
    z = sm + prm[1:2]
    softplus = jnp.maximum(z, 0.0) + jnp.log1p(jnp.exp(-jnp.abs(z)))
    beta_all = _sigmoid(sm) * validf
    g_all = -jnp.exp(prm[0:1]) * softplus * validf

    r2 = lax.broadcasted_iota(jnp.int32, (tl, tl), 0)
    c2 = lax.broadcasted_iota(jnp.int32, (tl, tl), 1)
    incl2 = ((r2 >> cs_shift) == (c2 >> cs_shift)) & ((r2 <= c2) if backward else (r2 >= c2))
    tri = jnp.where(incl2, 1.0, 0.0).astype(BF16)
    g1 = g_all.astype(BF16)
    rem = g_all - g1.astype(F32)
    g2 = rem.astype(BF16)
    g3 = (rem - g2.astype(F32)).astype(BF16)
    gc_all = _dot(tri, g1) + _dot(tri, g2) + _dot(tri, g3)

    r = lax.broadcasted_iota(jnp.int32, (cs, cs), 0)
    c = lax.broadcasted_iota(jnp.int32, (cs, cs), 1)
    incl = (r <= c) if backward else (r >= c)
    strict = (r < c) if backward else (r > c)
    levels = []
    b = 1
    while b < cs:
        sh = b.bit_length() - 1
        levels.append(((r >> (sh + 1)) == (c >> (sh + 1))) & ((r >> sh) != (c >> sh)))
        b *= 2

    chunk_order = range(tl // cs - 1, -1, -1) if backward else range(tl // cs)
    streams = []
    for j in range(hpb):
        hs = slice(DN_DK * j, DN_DK * (j + 1))
        q = q_ref[:, hs].astype(F32)
        k = k_ref[:, hs].astype(F32)
        v = v_ref[:, hs].astype(F32)
        col = (DN_HEADS if backward else 0) + hg * hpb + j
        beta = jnp.sum(jnp.where(lane == col, beta_all, 0.0), axis=-1, keepdims=True)
        g = jnp.sum(jnp.where(lane == 2 * DN_HEADS + col, g_all, 0.0), axis=-1, keepdims=True)
        gcum = jnp.sum(jnp.where(lane == 2 * DN_HEADS + col, gc_all, 0.0), axis=-1, keepdims=True)
        for ci in chunk_order:
            rs = slice(ci * cs, (ci + 1) * cs)
            qc, kc, vc, bc = q[rs], k[rs], v[rs], beta[rs]
            gtot = jnp.sum(g[rs], axis=0, keepdims=True)
            gb = jnp.broadcast_to(gcum[rs], (cs, LANES))
            gj = jnp.transpose(gb)[0:1, :]
            decay = jnp.where(incl, jnp.exp(jnp.where(incl, gb[:, 0:1] - gj, 0.0)), 0.0)
            kb = kc * bc
            k16 = kc.astype(BF16)
            egc = jnp.exp(gb)
            streams.append(dict(
                j=j, rs=rs, hs=hs, gtot=gtot,
                m=jnp.where(strict, _dot_nt(kb.astype(BF16), k16) * decay, 0.0),
                qk=(_dot_nt(qc.astype(BF16), k16) * decay).astype(BF16),
                rhs=jnp.concatenate([vc * bc, kb * egc], axis=1),
                q_dec=qc * egc,
                k_dec_t=jnp.transpose(kc * jnp.exp(gtot - gb)).astype(BF16)))

    for st in streams:
        st["e"] = -jnp.where(levels[0], st["m"], 0.0)
    for mask in levels[1:]:
        for st in streams:
            lb = jnp.where(mask, st["m"], 0.0)
            st["lb"] = lb
            st["y"] = lb + _dot(lb.astype(BF16), st["e"].astype(BF16))
        for st in streams:
            st["e"] = st["e"] - st["y"] - _dot(st["e"].astype(BF16), st["y"].astype(BF16))
    for st in streams:
        st["uw"] = st["rhs"] + _dot(st["e"].astype(BF16), st["rhs"].astype(BF16))

    state = [s_ref[j] for j in range(hpb)]
    for step in range(tl // cs):
        cur = [streams[j * (tl // cs) + step] for j in range(hpb)]
        for st in cur:
            wq = jnp.concatenate([st["uw"][:, DN_DV:], st["q_dec"]], axis=0).astype(BF16)
            st["wqs"] = _dot(wq, state[st["j"]].astype(BF16))
        for st in cur:
            v16 = (st["uw"][:, :DN_DV] - st["wqs"][:cs]).astype(BF16)
            o_ref[st["rs"], st["hs"]] = st["wqs"][cs:] + _dot(st["qk"], v16)
            state[st["j"]] = state[st["j"]] * jnp.exp(st["gtot"]) + _dot(st["k_dec_t"], v16)
    for j in range(hpb):
        s_ref[j] = state[j]


def _deltanet(qkv, gates, prm, seq_len, backward):
    lp = qkv.shape[0]
    tl = TOKEN_TILE
    nt = lp // tl
    hpb = DN_HEADS_PER_STEP
    wb = hpb * DN_DK
    ngroups = DN_HEADS // hpb

    def tile(i):
        return nt - 1 - i if backward else i

    def main_spec(off):
        return pl.BlockSpec((tl, wb), lambda h, i: (tile(i), off + h))

    in_specs = [main_spec(0), main_spec(ngroups), main_spec(2 * ngroups),
                pl.BlockSpec((tl, LANES), lambda h, i: (tile(i), 0)),
                pl.BlockSpec((8, LANES), lambda h, i: (0, 0))]
    args = [qkv, qkv, qkv, gates, prm]
    return pl.pallas_call(
        functools.partial(_dn_kernel, seq_len=seq_len, tl=tl, backward=backward),
        grid=(ngroups, nt),
        in_specs=in_specs,
        out_specs=pl.BlockSpec((tl, wb), lambda h, i: (tile(i), h)),
        out_shape=jax.ShapeDtypeStruct((lp, DN_VW), F32),
        scratch_shapes=[pltpu.VMEM((hpb, DN_DK, DN_DV), F32)],
        compiler_params=_params(2),
        name="deltanet_bwd" if backward else "deltanet_fwd",
    )(*args)


def _gnorm_kernel(of_ref, ob_ref, z_ref, w_ref, o_ref, *, nh):
    for j in range(nh):
        sl = slice(DN_DV * j, DN_DV * (j + 1))
        o = of_ref[:, sl] + ob_ref[:, sl]
        o = o * lax.rsqrt(jnp.mean(o * o, axis=-1, keepdims=True) + NORM_EPS) * w_ref[...]
        z = z_ref[:, sl].astype(F32)
        o_ref[:, sl] = (o * (z * _sigmoid(z))).astype(o_ref.dtype)


def _gated_norm(o_f, o_b, proj_main, out_norm_w):
    lp = proj_main.shape[0]
    nh = 4
    wb = nh * DN_DV
    tr = _pick(lp, 768, 16)
    zoff = (2 * DN_QK + DN_VW) // wb
    return pl.pallas_call(
        functools.partial(_gnorm_kernel, nh=nh),
        grid=(lp // tr, DN_VW // wb),
        in_specs=[pl.BlockSpec((tr, wb), lambda i, j: (i, j)),
                  pl.BlockSpec((tr, wb), lambda i, j: (i, j)),
                  pl.BlockSpec((tr, wb), lambda i, j: (i, zoff + j)),
                  pl.BlockSpec((1, DN_DV), lambda i, j: (0, 0))],
        out_specs=pl.BlockSpec((tr, wb), lambda i, j: (i, j)),
        out_shape=jax.ShapeDtypeStruct((lp, DN_VW), BF16),
        compiler_params=_params(2),
        name="gated_norm",
    )(o_f, o_b, proj_main, out_norm_w.reshape(1, DN_DV).astype(F32))


def _attn_kernel(q_ref, k_ref, v_ref, prm_ref, o_ref, *, seq_len, tq, lp):
    groups = SWA_HQ // SWA_HKV
    heads = range(groups)
    prm = prm_ref[0]
    slopes = [prm[g:g + 1, 0:1] for g in heads]
    sinks = [prm[groups + g:groups + g + 1, 0:1] for g in heads]
    scale = SWA_D ** -0.5
    k_meta = k_ref[0:tq, :]
    v_meta = v_ref[0:tq, :]
    rq = lax.broadcasted_iota(jnp.int32, (tq, 1), 0)
    ck = lax.broadcasted_iota(jnp.int32, (1, 4 * tq), 1)
    meta_part = ck < tq
    win_part = jnp.logical_not(meta_part)

    def tile_bias(q0, k0):
        pq = q0 + rq
        pk = k0 + (ck - tq)
        real_q = pq >= N_META
        dist = pq - pk
        adist = jnp.abs(dist)
        in_win = (real_q & (adist <= SWA_WINDOW)) | (jnp.logical_not(real_q) & (-dist <= SWA_WINDOW))
        win_ok = win_part & (pk >= N_META) & (pk < seq_len) & in_win
        ok = (meta_part & (ck < N_META)) | win_ok
        return jnp.where(ok, jnp.where(real_q & win_part, -adist.astype(F32), 0.0), -jnp.inf)

    def tile(i, bias):
        q0 = pl.multiple_of(i * tq, tq)
        k0 = pl.multiple_of(jnp.clip(i * tq - tq, 0, lp - 3 * tq), tq)
        kall = jnp.concatenate([k_meta, k_ref[pl.ds(k0, 3 * tq), :]], axis=0)
        vall = jnp.concatenate([v_meta, v_ref[pl.ds(k0, 3 * tq), :]], axis=0)
        if bias is None:
            bias = tile_bias(q0, k0)
        s = [_dot_nt(q_ref[pl.ds(q0, tq), SWA_D * g:SWA_D * (g + 1)], kall) * scale + slopes[g] * bias
             for g in heads]
        m = [jnp.maximum(jnp.max(s[g], axis=-1, keepdims=True), sinks[g]) for g in heads]
        p = [jnp.exp(s[g] - m[g]) for g in heads]
        den = [jnp.sum(p[g], axis=-1, keepdims=True) + jnp.exp(sinks[g] - m[g]) for g in heads]
        o = [_dot(p[g].astype(BF16), vall) for g in heads]
        for g in heads:
            o_ref[pl.ds(q0, tq), SWA_D * g:SWA_D * (g + 1)] = (o[g] / den[g]).astype(o_ref.dtype)

    def loop(first, last, bias):
        def body(i, carry):
            tile(i, bias)
            return carry

        lax.fori_loop(first, last, body, 0, unroll=2 if (last - first) % 2 == 0 else 1)

    nq = lp // tq
    first_inner = 2
    last_inner = max(first_inner, seq_len // tq - 1)
    loop(0, first_inner, None)
    loop(first_inner, last_inner, tile_bias(first_inner * tq, (first_inner - 1) * tq))
    loop(last_inner, nq, None)


def _window_attention(rest, attn_sink, seq_len):
    lp = rest.shape[0]
    tq = ATTN_TILE
    assert lp >= 3 * tq
    groups = SWA_HQ // SWA_HKV
    koff = SWA_QW // SWA_D
    voff = koff + SWA_HKV
    slopes = 2.0 ** (-8.0 * jnp.arange(1, SWA_HQ + 1, dtype=F32) / SWA_HQ)
    prm = jnp.concatenate([slopes.reshape(SWA_HKV, groups), attn_sink.astype(F32).reshape(SWA_HKV, groups)], axis=1)
    prm = jnp.broadcast_to(prm[:, :, None], (SWA_HKV, 2 * groups, LANES))
    qw = groups * SWA_D
    return pl.pallas_call(
        functools.partial(_attn_kernel, seq_len=seq_len, tq=tq, lp=lp),
        grid=(SWA_HKV,),
        in_specs=[pl.BlockSpec((lp, qw), lambda h: (0, h)),
                  pl.BlockSpec((lp, SWA_D), lambda h: (0, koff + h)),
                  pl.BlockSpec((lp, SWA_D), lambda h: (0, voff + h)),
                  pl.BlockSpec((1, 2 * groups, LANES), lambda h: (h, 0, 0))],
        out_specs=pl.BlockSpec((lp, qw), lambda h: (0, h)),
        out_shape=jax.ShapeDtypeStruct((lp, SWA_QW), BF16),
        compiler_params=_params(1),
        name="window_attention",
    )(rest, rest, rest, prm)


def _branch_kernel(oa_ref, ob_ref, wa_ref, wb_ref, ga_ref, gb_ref, o_ref, wa16_ref, wb16_ref):
    @pl.when(pl.program_id(1) == 0)
    def _():
        wa16_ref[...] = wa_ref[...].astype(BF16)
        wb16_ref[...] = wb_ref[...].astype(BF16)

    ya = _dot(oa_ref[...], wa16_ref[...])
    yb = _dot(ob_ref[...], wb16_ref[...])
    o = _sigmoid(ga_ref[...].astype(F32)) * ya + _sigmoid(gb_ref[...].astype(F32)) * yb
    o_ref[...] = o.astype(o_ref.dtype)


def _branch_merge(o_a, o_b, w_a, w_b, rest, d_model):
    lp = o_a.shape[0]
    tn = _pick(d_model, 512, LANES)
    tm = _pick(lp, 1056, 16)
    ga_off = (SWA_QW + 2 * SWA_KVW) // tn
    gb_off = (SWA_QW + 2 * SWA_KVW + d_model) // tn
    assert (SWA_QW + 2 * SWA_KVW) % tn == 0 and d_model % tn == 0
    return pl.pallas_call(
        _branch_kernel,
        grid=(d_model // tn, lp // tm),
        in_specs=[pl.BlockSpec((tm, DN_VW), lambda n, i: (i, 0)),
                  pl.BlockSpec((tm, SWA_QW), lambda n, i: (i, 0)),
                  pl.BlockSpec((DN_VW, tn), lambda n, i: (0, n)),
                  pl.BlockSpec((SWA_QW, tn), lambda n, i: (0, n)),
                  pl.BlockSpec((tm, tn), lambda n, i: (i, ga_off + n)),
                  pl.BlockSpec((tm, tn), lambda n, i: (i, gb_off + n))],
        out_specs=pl.BlockSpec((tm, tn), lambda n, i: (i, n)),
        out_shape=jax.ShapeDtypeStruct((lp, d_model), BF16),
        scratch_shapes=[pltpu.VMEM((DN_VW, tn), BF16), pltpu.VMEM((SWA_QW, tn), BF16)],
        compiler_params=_params(2),
        name="branch_merge",
    )(o_a, o_b, w_a, w_b, rest, rest)


def _router_kernel(h_ref, w_ref, wr_ref, xg_ref, afft_ref, *, seq_len, tr, d_model):
    i = pl.program_id(0)
    x = h_ref[...]
    xn = x * lax.rsqrt(jnp.mean(x * x, axis=-1, keepdims=True) + NORM_EPS) * w_ref[...]
    logits = _dot_nt(xn.astype(BF16), wr_ref[...].astype(BF16))
    lane = lax.broadcasted_iota(jnp.int32, (1, LANES), 1)
    lm = jnp.where(lane < N_EXPERTS, logits, -jnp.inf)
    ex = jnp.exp(lm - jnp.max(lm, axis=-1, keepdims=True))
    aff = ex / jnp.sum(ex, axis=-1, keepdims=True)
    row = i * tr + lax.broadcasted_iota(jnp.int32, (tr, 1), 0)
    aff = jnp.where((row < seq_len) & (lane < N_EXPERTS), aff, -1.0)
    nx = d_model // LANES
    for c in range(nx):
        xg_ref[:, c, :] = xn[:, LANES * c:LANES * (c + 1)]
    xg_ref[:, nx, :] = aff
    xg_ref[:, nx + 1:, :] = jnp.zeros((tr, ROW_PAD - 1, LANES), F32)
    afft_ref[...] = jnp.transpose(aff)[:N_EXPERTS, :]


def _router(h2, norm2_w, w_router, seq_len):
    lp, d = h2.shape
    tr = TOKEN_TILE
    wr = jnp.zeros((LANES, d), F32).at[:N_EXPERTS].set(jnp.swapaxes(w_router, 0, 1).astype(F32))
    nc = d // LANES + ROW_PAD
    return pl.pallas_call(
        functools.partial(_router_kernel, seq_len=seq_len, tr=tr, d_model=d),
        grid=(lp // tr,),
        in_specs=[pl.BlockSpec((tr, d), lambda i: (i, 0)),
                  pl.BlockSpec((1, d), lambda i: (0, 0)),
                  pl.BlockSpec((LANES, d), lambda i: (0, 0))],
        out_specs=[pl.BlockSpec((tr, nc, LANES), lambda i: (i, 0, 0)),
                   pl.BlockSpec((N_EXPERTS, tr), lambda i: (0, i))],
        out_shape=[jax.ShapeDtypeStruct((lp, nc, LANES), F32), jax.ShapeDtypeStruct((N_EXPERTS, lp), F32)],
        compiler_params=_params(1),
        name="router",
    )(h2, norm2_w.reshape(1, d).astype(F32), wr)


def _select_kernel(afft_ref, idx_ref, lo_ref, sel_ref, csum_ref, cols_ref, *, cap, slot_pad, lp, tile, n_tiles):
    aff = afft_ref[...]
    capf = float(cap)

    def count_ge(thr):
        return jnp.sum(jnp.where(aff >= thr, 1.0, 0.0), axis=1, keepdims=True)

    def bisect(_, carry):
        lo, hi = carry
        mid = 0.5 * (lo + hi)
        ge = count_ge(mid) >= capf
        return jnp.where(ge, mid, lo), jnp.where(ge, hi, mid)

    lo0 = jnp.zeros((N_EXPERTS, 1), F32)
    hi0 = jnp.full((N_EXPERTS, 1), 2.0, F32)
    _, hi = lax.fori_loop(0, 40, bisect, (lo0, hi0))

    def refine(st):
        hi, tau, done, _ = st
        cand = jnp.max(jnp.where(aff < hi, aff, -2.0), axis=1, keepdims=True)
        found = jnp.where(count_ge(cand) >= capf, 1.0, 0.0)
        tau = jnp.where(done > 0.0, tau, cand)
        hi = jnp.where(done + found > 0.0, hi, cand)
        done = jnp.maximum(done, found)
        return hi, tau, done, jnp.sum(1.0 - done)

    zero = jnp.zeros((N_EXPERTS, 1), F32)
    _, tau, _, _ = lax.while_loop(lambda st: st[3] > 0.0, refine, (hi, zero, zero, jnp.float32(N_EXPERTS)))
    need = capf - jnp.sum(jnp.where(aff > tau, 1.0, 0.0), axis=1, keepdims=True)

    ra = lax.broadcasted_iota(jnp.int32, (LANES, LANES), 0)
    ca = lax.broadcasted_iota(jnp.int32, (LANES, LANES), 1)
    ut = jnp.where(ra <= ca, 1.0, 0.0).astype(BF16)
    carry_eq = jnp.zeros((N_EXPERTS, 1), F32)
    carry_sel = jnp.zeros((N_EXPERTS, 1), F32)
    for j in range(lp // LANES):
        sl = slice(LANES * j, LANES * (j + 1))
        bj = aff[:, sl]
        eqf = jnp.where(bj == tau, 1.0, 0.0)
        tie_rank = _dot(eqf.astype(BF16), ut) + carry_eq - eqf
        carry_eq = carry_eq + jnp.sum(eqf, axis=1, keepdims=True)
        self = jnp.where((bj > tau) | ((bj == tau) & (tie_rank < need)), 1.0, 0.0)
        csum_ref[j] = _dot(self.astype(BF16), ut) + carry_sel
        carry_sel = carry_sel + jnp.sum(self, axis=1, keepdims=True)
        sel_ref[:, sl] = self

    lane = lax.broadcasted_iota(jnp.int32, (1, LANES), 1)
    sblk = 64
    cbase = lax.broadcasted_iota(jnp.int32, (sblk, 1), 0)
    cols_ref[...] = jnp.zeros_like(cols_ref)

    def per_expert(e, carry):
        def per_block(sb, carry2):
            cvals = (sb * sblk + cbase).astype(F32)

            def per_tile(j, acc):
                return acc + jnp.where(csum_ref[j, pl.ds(e, 1), :] <= cvals, 1.0, 0.0)

            n_lane_tiles = lp // LANES
            acc = lax.fori_loop(0, n_lane_tiles, per_tile, jnp.zeros((sblk, LANES), F32),
                                unroll=6 if n_lane_tiles % 6 == 0 else 1)
            rows = pl.ds(pl.multiple_of(sb * sblk, sblk), sblk)
            cols_ref[rows, :] = jnp.where(lane == e, jnp.sum(acc, axis=1, keepdims=True), cols_ref[rows, :])
            return carry2

        return lax.fori_loop(0, slot_pad // sblk, per_block, carry)

    lax.fori_loop(0, N_EXPERTS, per_expert, 0)
    idx_ref[...] = jnp.transpose(cols_ref[...])[:N_EXPERTS, :].astype(jnp.int32)

    sel = sel_ref[...]
    tok = lax.broadcasted_iota(jnp.int32, (1, lp), 1)
    table = jnp.zeros((N_EXPERTS, LANES), F32)
    for i in range(n_tiles + 1):
        below = jnp.sum(jnp.where(tok < N_META + tile * i, sel, 0.0), axis=1, keepdims=True)
        table = jnp.where(lane == i, below, table)
    lo_ref[...] = table.astype(jnp.int32)


def _select(afft, cap, slot_pad, tile, n_tiles):
    lp = afft.shape[1]
    assert n_tiles < LANES
    return pl.pallas_call(
        functools.partial(_select_kernel, cap=cap, slot_pad=slot_pad, lp=lp, tile=tile, n_tiles=n_tiles),
        out_shape=[jax.ShapeDtypeStruct((N_EXPERTS, slot_pad), jnp.int32),
                   jax.ShapeDtypeStruct((N_EXPERTS, LANES), jnp.int32)],
        scratch_shapes=[pltpu.VMEM((N_EXPERTS, lp), F32), pltpu.VMEM((lp // LANES, N_EXPERTS, LANES), F32),
                        pltpu.VMEM((slot_pad, LANES), F32)],
        compiler_params=pltpu.CompilerParams(vmem_limit_bytes=VMEM_LIMIT),
        name="expert_select",
    )(afft)


def _slab_copy(src, s_tok, dst, d_tok, sem, *, nc):
    s0 = pl.multiple_of(s_tok * nc, 8)
    d0 = pl.multiple_of(d_tok * nc, 8)
    return pltpu.make_async_copy(src.at[pl.ds(s0, nc)], dst.at[pl.ds(d0, nc)], sem)


def _dispatch_kernel(idx_ref, xg_ref, x_ref, aff_ref, slab_ref, sem, *, cap, cap_pad, nc, nx):
    copy = functools.partial(_slab_copy, nc=nc)
    e = pl.program_id(0)
    h0 = -(-(cap_pad // 2) // 16) * 16
    halves = ((0, h0, min(cap, h0)), (h0, cap_pad - h0, max(cap - h0, 0)))

    def fetch(ex, hf):
        first, _, used = halves[hf]

        def issue(s, c):
            copy(xg_ref, idx_ref[ex, first + s], slab_ref.at[hf], s, sem.at[hf]).start()
            return c

        lax.fori_loop(0, used, issue, 0, unroll=DMA_ISSUE_UNROLL if used % DMA_ISSUE_UNROLL == 0 else 1)

    def finish(hf):
        first, size, used = halves[hf]

        def drain(_, c):
            copy(xg_ref, 0, slab_ref.at[hf], 0, sem.at[hf]).wait()
            return c

        lax.fori_loop(0, used, drain, 0)
        rows = slice(first, first + size)
        for c in range(nx):
            x_ref[rows, LANES * c:LANES * (c + 1)] = slab_ref[hf, pl.ds(c, size, stride=nc), :].astype(BF16)
        aff_ref[rows, :] = slab_ref[hf, pl.ds(nx, size, stride=nc), :]

    @pl.when(e == 0)
    def _():
        for hf, (_, size, used) in enumerate(halves):
            if used < size:
                slab_ref[hf, used * nc:size * nc, :] = jnp.zeros(((size - used) * nc, LANES), F32)
        fetch(0, 0)

    fetch(e, 1)
    finish(0)

    @pl.when(e + 1 < N_EXPERTS)
    def _():
        fetch(e + 1, 0)

    finish(1)


def _dispatch(xg, idx, cap, cap_pad, d_model):
    nx = d_model // LANES
    nc = nx + ROW_PAD
    h0 = -(-(cap_pad // 2) // 16) * 16
    return pl.pallas_call(
        functools.partial(_dispatch_kernel, cap=cap, cap_pad=cap_pad, nc=nc, nx=nx),
        grid_spec=pltpu.PrefetchScalarGridSpec(
            num_scalar_prefetch=1,
            grid=(N_EXPERTS,),
            in_specs=[pl.BlockSpec(memory_space=pl.ANY)],
            out_specs=[pl.BlockSpec((cap_pad, d_model), lambda e, *_: (e, 0)),
                       pl.BlockSpec((cap_pad, LANES), lambda e, *_: (e, 0))],
            scratch_shapes=[pltpu.VMEM((2, h0 * nc, LANES), F32), pltpu.SemaphoreType.DMA((2,))]),
        out_shape=[jax.ShapeDtypeStruct((N_EXPERTS * cap_pad, d_model), BF16),
                   jax.ShapeDtypeStruct((N_EXPERTS * cap_pad, LANES), F32)],
        compiler_params=_params(1),
        name="expert_dispatch",
    )(idx, xg)


def _ffn_a_kernel(x_ref, wg_ref, wu_ref, o_ref):
    x16 = x_ref[...]
    g = _dot(x16, wg_ref[0].astype(BF16))
    u = _dot(x16, wu_ref[0].astype(BF16))
    o_ref[...] = (g * _sigmoid(g) * u).astype(o_ref.dtype)


def _ffn_a(xe, w_gate, w_up, cap_pad, d_model):
    _, _, ff = w_gate.shape
    tf = _pick(ff, 256, LANES)
    return pl.pallas_call(
        _ffn_a_kernel,
        grid=(N_EXPERTS, ff // tf),
        in_specs=[pl.BlockSpec((cap_pad, d_model), lambda e, f: (e, 0)),
                  pl.BlockSpec((1, d_model, tf), lambda e, f: (e, 0, f)),
                  pl.BlockSpec((1, d_model, tf), lambda e, f: (e, 0, f))],
        out_specs=pl.BlockSpec((cap_pad, tf), lambda e, f: (e, f)),
        out_shape=jax.ShapeDtypeStruct((N_EXPERTS * cap_pad, ff), BF16),
        compiler_params=_params(2),
        name="expert_ffn_in",
    )(xe, w_gate, w_up)


def _ffn_b_kernel(h_ref, wd_ref, aff_ref, o_ref):
    e = pl.program_id(0)
    y = _dot(h_ref[...], wd_ref[0].astype(BF16))
    lane = lax.broadcasted_iota(jnp.int32, (1, LANES), 1)
    gate = jnp.sum(jnp.where(lane == e, aff_ref[...], 0.0), axis=-1, keepdims=True)
    y = y * gate
    for c in range(y.shape[1] // LANES):
        o_ref[:, c, :] = y[:, LANES * c:LANES * (c + 1)]


def _ffn_b(hid, w_down, aff, cap_pad, d_model):
    _, ff, _ = w_down.shape
    td = _pick(d_model, 1024, 8 * LANES) if d_model % (8 * LANES) == 0 else d_model
    return pl.pallas_call(
        _ffn_b_kernel,
        grid=(N_EXPERTS, d_model // td),
        in_specs=[pl.BlockSpec((cap_pad, ff), lambda e, n: (e, 0)),
                  pl.BlockSpec((1, ff, td), lambda e, n: (e, 0, n)),
                  pl.BlockSpec((cap_pad, LANES), lambda e, n: (e, 0))],
        out_specs=pl.BlockSpec((cap_pad, td // LANES, LANES), lambda e, n: (e, n, 0)),
        out_shape=jax.ShapeDtypeStruct((N_EXPERTS * cap_pad, d_model // LANES, LANES), F32),
        compiler_params=_params(2),
        name="expert_ffn_out",
    )(hid, w_down, aff)


def _combine_kernel(idx_ref, lo_ref, h_ref, ye_ref, nw_ref, o_ref, hbuf_ref, acc_ref, stage_ref, sem, hsem, *,
                    tt, nx, cap_pad):
    i = pl.program_id(0)
    t0 = N_META + i * tt
    ch = COMBINE_CHUNK
    reg = COMBINE_REGION
    pitch = nx + ROW_PAD
    slot = i % 2

    def residual_copy(step, buf):
        rows = pl.ds(pl.multiple_of(N_META + step * tt, 8), tt)
        return pltpu.make_async_copy(h_ref.at[rows], hbuf_ref.at[buf], hsem.at[buf])

    @pl.when(i == 0)
    def _():
        residual_copy(0, 0).start()

    @pl.when(i + 1 < pl.num_programs(0))
    def _():
        residual_copy(i + 1, 1 - slot).start()

    def round_chunks(e, r):
        left = lo_ref[e, i + 1] - lo_ref[e, i] - r * reg
        return (jnp.minimum(left, reg) + ch - 1) // ch

    def chunk_copy(e, r, j):
        src = pl.multiple_of((e * cap_pad + lo_ref[e, i] + r * reg + j * ch) * nx, 8)
        dst = pl.multiple_of(j * ch * nx, 8)
        return pltpu.make_async_copy(ye_ref.at[pl.ds(src, ch * nx)], stage_ref.at[e, pl.ds(dst, ch * nx)],
                                     sem.at[e])

    def fetch(e, r):
        def body(j, c):
            chunk_copy(e, r, j).start()
            return c

        lax.fori_loop(0, round_chunks(e, r), body, 0)

    def add_round(e, r):
        def drain(j, c):
            chunk_copy(e, r, j).wait()
            return c

        lax.fori_loop(0, round_chunks(e, r), drain, 0)
        first = lo_ref[e, i] + r * reg

        def per_slot(s, c):
            src = pl.ds(pl.multiple_of((s - first) * nx, 8), nx)
            dst = pl.ds(pl.multiple_of((idx_ref[e, s] - t0) * pitch, 8), nx)
            acc_ref[dst, :] = acc_ref[dst, :] + stage_ref[e, src, :]
            return c

        lax.fori_loop(first, jnp.minimum(first + reg, lo_ref[e, i + 1]), per_slot, 0)

    def prefetch(e, c):
        fetch(e, 0)
        return c

    lax.fori_loop(0, N_EXPERTS, prefetch, 0)
    residual_copy(i, slot).wait()
    for c in range(nx):
        acc_ref[pl.ds(c, tt, stride=pitch), :] = hbuf_ref[slot, :, LANES * c:LANES * (c + 1)]

    def per_expert(e, carry):
        add_round(e, 0)

        def later(r, c):
            fetch(e, r)
            add_round(e, r)
            return c

        n_rounds = (lo_ref[e, i + 1] - lo_ref[e, i] + reg - 1) // reg
        lax.fori_loop(1, n_rounds, later, 0)
        return carry

    lax.fori_loop(0, N_EXPERTS, per_expert, 0)
    for c in range(nx):
        hbuf_ref[slot, :, LANES * c:LANES * (c + 1)] = acc_ref[pl.ds(c, tt, stride=pitch), :]
    x = hbuf_ref[slot]
    o_ref[...] = x * lax.rsqrt(jnp.mean(x * x, axis=-1, keepdims=True) + NORM_EPS) * nw_ref[...]


def _combine(idx, lo, h2, ye, norm_f_w, n_real, tt, cap_pad):
    d = h2.shape[1]
    nx = d // LANES
    assert COMBINE_REGION % COMBINE_CHUNK == 0
    return pl.pallas_call(
        functools.partial(_combine_kernel, tt=tt, nx=nx, cap_pad=cap_pad),
        grid_spec=pltpu.PrefetchScalarGridSpec(
            num_scalar_prefetch=2,
            grid=(n_real // tt,),
            in_specs=[pl.BlockSpec(memory_space=pl.ANY),
                      pl.BlockSpec(memory_space=pl.ANY),
                      pl.BlockSpec((1, d), lambda i, *_: (0, 0))],
            out_specs=pl.BlockSpec((tt, d), lambda i, *_: (i, 0)),
            scratch_shapes=[pltpu.VMEM((2, tt, d), F32), pltpu.VMEM((tt * (nx + ROW_PAD), LANES), F32),
                            pltpu.VMEM((N_EXPERTS, COMBINE_REGION * nx, LANES), F32),
                            pltpu.SemaphoreType.DMA((N_EXPERTS,)), pltpu.SemaphoreType.DMA((2,))]),
        out_shape=jax.ShapeDtypeStruct((n_real, d), F32),
        compiler_params=_params(1),
        name="expert_combine",
    )(idx, lo, h2, ye, norm_f_w.reshape(1, d).astype(F32))


def _layer(hp, seq_len, norm1_w, w_in, conv_w, a_log_fwd, a_log_bwd, dt_bias_fwd, dt_bias_bwd, out_norm_w,
           w_branch_a, attn_sink, w_branch_b, w_out, norm2_w, w_router, w_gate, w_up, w_down):
    lp, d = hp.shape
    main_w = 2 * DN_QK + 2 * DN_VW
    gate_w = 4 * DN_HEADS
    rest_w = SWA_QW + 2 * SWA_KVW + 2 * d
    n = _rmsnorm(hp, norm1_w, BF16)
    w_t = jnp.swapaxes(w_in, 0, 1)
    proj_main = _matmul(n, w_t, main_w, 0, BF16, w_rows=True, name="in_proj_main")
    gates = _matmul(n, w_t, LANES, main_w, F32, w_rows=True, name="in_proj_gates")
    rest = _matmul(n, w_t, rest_w, main_w + gate_w, BF16, w_rows=True, name="in_proj_rest")

    prm = jnp.zeros((8, LANES), F32)
    prm = prm.at[0, 2 * DN_HEADS:4 * DN_HEADS].set(jnp.concatenate([a_log_fwd, a_log_bwd]).astype(F32))
    prm = prm.at[1, 2 * DN_HEADS:4 * DN_HEADS].set(jnp.concatenate([dt_bias_fwd, dt_bias_bwd]).astype(F32))
    qkv = _dn_prep(proj_main, conv_w.astype(F32), seq_len)
    o_f = _deltanet(qkv, gates, prm, seq_len, backward=False)
    o_r = _deltanet(qkv, gates, prm, seq_len, backward=True)
    o_a = _gated_norm(o_f, o_r, proj_main, out_norm_w)
    o_b = _window_attention(rest, attn_sink, seq_len)

    mixed = _branch_merge(o_a, o_b, w_branch_a, w_branch_b, rest, d)
    h2 = _matmul(mixed, w_out, d, 0, F32, res=hp, name="out_proj")

    cap = EC_CAPACITY * seq_len // N_EXPERTS
    cap_pad = -(-(cap + COMBINE_CHUNK) // 16) * 16
    xg, afft = _router(h2, norm2_w, w_router, seq_len)
    n_real = seq_len - N_META
    tt = _pick(n_real, TOKEN_TILE, LANES)
    slot_pad = -(-cap_pad // LANES) * LANES
    idx, lo = _select(afft, cap, slot_pad, tt, n_real // tt)
    xe, aff = _dispatch(xg.reshape(-1, LANES), idx, cap, cap_pad, d)
    hid = _ffn_a(xe, w_gate, w_up, cap_pad, d)
    ye = _ffn_b(hid, w_down, aff, cap_pad, d)
    return functools.partial(_combine, idx, lo, h2, ye.reshape(-1, LANES), n_real=n_real, tt=tt, cap_pad=cap_pad)


def kernel(x, meta_tokens, norm1_w, w_in, conv_w, a_log_fwd, a_log_bwd, dt_bias_fwd, dt_bias_bwd, out_norm_w,
           w_branch_a, attn_sink, w_branch_b, w_out, norm2_w, w_router, w_gate, w_up, w_down, norm_f_w):
    batch, seq, d = x.shape
    depth = norm1_w.shape[0]
    assert depth == 1, "the final norm is fused into the last layer's expert combine"
    seq_len = N_META + seq
    lp = -(-seq_len // TOKEN_TILE) * TOKEN_TILE
    outs = []
    for b in range(batch):
        h = jnp.concatenate([meta_tokens.astype(x.dtype), x[b]], axis=0)
        hp = jnp.pad(h, ((0, lp - seq_len), (0, 0)))
        combine = _layer(hp, seq_len, norm1_w[0], w_in[0], conv_w[0], a_log_fwd[0], a_log_bwd[0],
                         dt_bias_fwd[0], dt_bias_bwd[0], out_norm_w[0], w_branch_a[0], attn_sink[0],
                         w_branch_b[0], w_out[0], norm2_w[0], w_router[0], w_gate[0], w_up[0], w_down[0])
        outs.append(combine(norm_f_w=norm_f_w))
    return jnp.stack(outs, axis=0)
```

```python
import functools

import jax
import jax.numpy as jnp
from jax import lax
from jax.experimental import pallas as pl
from jax.experimental.pallas import tpu as pltpu

F32 = jnp.float32
BF16 = jnp.bfloat16

N_META = 16
NORM_EPS = 1e-6
DN_HEADS = 16
DN_DK = 128
DN_DV = 128
SWA_HQ = 16
SWA_HKV = 4
SWA_D = 128
SWA_WINDOW = 128
N_EXPERTS = 16
EC_CAPACITY = 2

LANES = 128
TOKEN_TILE = 256
DN_CHUNK = 128
DN_HEADS_PER_STEP = 8
ATTN_TILE = 128
ROW_PAD = 8
DMA_ISSUE_UNROLL = 6
COMBINE_CHUNK = 8
COMBINE_REGION = 64
VMEM_LIMIT = 56 * 1024 * 1024

DN_QK = DN_HEADS * DN_DK
DN_VW = DN_HEADS * DN_DV
SWA_QW = SWA_HQ * SWA_D
SWA_KVW = SWA_HKV * SWA_D


def _params(n_grid):
    return pltpu.CompilerParams(dimension_semantics=("arbitrary",) * n_grid, vmem_limit_bytes=VMEM_LIMIT)


def _pick(n, target, mult):
    best = None
    for t in range(mult, min(n, target) + 1, mult):
        if n % t == 0:
            best = t
    assert best is not None, (n, target, mult)
    return best


def _sigmoid(x):
    return 1.0 / (1.0 + jnp.exp(-x))


def _dot(a, b):
    return jnp.dot(a, b, preferred_element_type=F32)


def _dot_nt(a, b):
    return lax.dot_general(a, b, (((1,), (1,)), ((), ())), preferred_element_type=F32)


def _rms_kernel(x_ref, w_ref, o_ref):
    x = x_ref[...]
    ms = jnp.mean(x * x, axis=-1, keepdims=True)
    o_ref[...] = (x * lax.rsqrt(ms + NORM_EPS) * w_ref[...]).astype(o_ref.dtype)


def _rmsnorm(x, w, out_dtype):
    m, d = x.shape
    tr = _pick(m, 256, 16)
    return pl.pallas_call(
        _rms_kernel,
        grid=(m // tr,),
        in_specs=[pl.BlockSpec((tr, d), lambda i: (i, 0)), pl.BlockSpec((1, d), lambda i: (0, 0))],
        out_specs=pl.BlockSpec((tr, d), lambda i: (i, 0)),
        out_shape=jax.ShapeDtypeStruct((m, d), out_dtype),
        compiler_params=_params(1),
        name="rmsnorm",
    )(x, w.reshape(1, d).astype(F32))


def _mm_kernel(*refs, has_res, w_rows):
    if has_res:
        a_ref, w_ref, r_ref, o_ref, wb_ref = refs
    else:
        a_ref, w_ref, o_ref, wb_ref = refs

    @pl.when(pl.program_id(1) == 0)
    def _():
        w = w_ref[...]
        wb_ref[...] = (jnp.transpose(w) if w_rows else w).astype(BF16)

    acc = _dot(a_ref[...], wb_ref[...])
    if has_res:
        acc = acc + r_ref[...]
    o_ref[...] = acc.astype(o_ref.dtype)


def _matmul(a, w, n_cols, off, out_dtype, res=None, w_rows=False, name="matmul"):
    m, k = a.shape
    tn = _pick(n_cols, 512, LANES)
    tm = _pick(m, 1408 if res is None else 1056, 16)
    if w_rows:
        if off % tn == 0:
            w_spec = pl.BlockSpec((tn, k), lambda n, i: (off // tn + n, 0))
        else:
            assert off % 8 == 0
            w_spec = pl.BlockSpec((pl.Element(tn), pl.Element(k)), lambda n, i: (pl.multiple_of(off + n * tn, 8), 0))
    else:
        assert off % tn == 0
        w_spec = pl.BlockSpec((k, tn), lambda n, i: (0, off // tn + n))
    in_specs = [pl.BlockSpec((tm, k), lambda n, i: (i, 0)), w_spec]
    args = [a, w]
    if res is not None:
        in_specs.append(pl.BlockSpec((tm, tn), lambda n, i: (i, n)))
        args.append(res)
    return pl.pallas_call(
        functools.partial(_mm_kernel, has_res=res is not None, w_rows=w_rows),
        grid=(n_cols // tn, m // tm),
        in_specs=in_specs,
        out_specs=pl.BlockSpec((tm, tn), lambda n, i: (i, n)),
        out_shape=jax.ShapeDtypeStruct((m, n_cols), out_dtype),
        scratch_shapes=[pltpu.VMEM((k, tn), BF16)],
        compiler_params=_params(2),
        name=name,
    )(*args)


def _dn_prep_kernel(x_ref, p_ref, n_ref, c_ref, o_ref, *, seq_len, tl, nh, q_blocks, k_blocks):
    i = pl.program_id(0)
    j = pl.program_id(1)
    nt = pl.num_programs(0)
    row = lax.broadcasted_iota(jnp.int32, (tl, 1), 0)
    validf = ((i * tl + row) < seq_len).astype(F32)
    x = x_ref[...].astype(F32)
    prev = p_ref[...].astype(F32)[15:16, :] * (i > 0).astype(F32)
    nxt = n_ref[...].astype(F32)[0:1, :] * (i < nt - 1).astype(F32)
    xm1 = jnp.where(row == 0, prev, pltpu.roll(x, 1, 0))
    xp1 = jnp.where(row == tl - 1, nxt, pltpu.roll(x, tl - 1, 0))
    c = c_ref[...]
    y = xm1 * c[0:1] + x * c[1:2] + xp1 * c[2:3]
    y = y * _sigmoid(y)
    is_qk = (j < q_blocks + k_blocks).astype(F32)
    q_scale = jnp.where(j < q_blocks, DN_DK ** -0.5, 1.0)
    for h in range(nh):
        hs = slice(DN_DK * h, DN_DK * (h + 1))
        yh = y[:, hs]
        norm = lax.rsqrt(jnp.sum(yh * yh, axis=-1, keepdims=True) + NORM_EPS) * q_scale
        o_ref[:, hs] = (yh * (is_qk * norm + (1.0 - is_qk)) * validf).astype(o_ref.dtype)


def _dn_prep(proj_main, conv_w, seq_len):
    lp = proj_main.shape[0]
    tl = _pick(lp, 3 * TOKEN_TILE, TOKEN_TILE)
    nh = 8
    wb = nh * DN_DK
    width = 2 * DN_QK + DN_VW
    hb = tl // 16
    nhb = lp // 16
    return pl.pallas_call(
        functools.partial(_dn_prep_kernel, seq_len=seq_len, tl=tl, nh=nh, q_blocks=DN_QK // wb, k_blocks=DN_QK // wb),
        grid=(lp // tl, width // wb),
        in_specs=[pl.BlockSpec((tl, wb), lambda i, j: (i, j)),
                  pl.BlockSpec((16, wb), lambda i, j: (jnp.maximum(i * hb - 1, 0), j)),
                  pl.BlockSpec((16, wb), lambda i, j: (jnp.minimum((i + 1) * hb, nhb - 1), j)),
                  pl.BlockSpec((3, wb), lambda i, j: (0, j))],
        out_specs=pl.BlockSpec((tl, wb), lambda i, j: (i, j)),
        out_shape=jax.ShapeDtypeStruct((lp, width), BF16),
        compiler_params=_params(2),
        name="deltanet_prep",
    )(proj_main, proj_main, proj_main, conv_w)


def _dn_kernel(q_ref, k_ref, v_ref, sm_ref, prm_ref, o_ref, s_ref, *, seq_len, tl, backward):
    hg = pl.program_id(0)
    i = pl.program_id(1)
    nt = pl.num_programs(1)
    ti = nt - 1 - i if backward else i
    cs = DN_CHUNK
    cs_shift = cs.bit_length() - 1
    hpb = DN_HEADS_PER_STEP

    @pl.when(i == 0)
    def _():
        s_ref[...] = jnp.zeros_like(s_ref)

    row = lax.broadcasted_iota(jnp.int32, (tl, 1), 0)
    validf = ((ti * tl + row) < seq_len).astype(F32)

    lane = lax.broadcasted_iota(jnp.int32, (1, LANES), 1)
    sm = sm_ref[...]
    prm = prm_ref[...]
    z = sm + prm[1:2]
    softplus = jnp.maximum(z, 0.0) + jnp.log1p(jnp.exp(-jnp.abs(z)))
    beta_all = _sigmoid(sm) * validf
    g_all = -jnp.exp(prm[0:1]) * softplus * validf

    r2 = lax.broadcasted_iota(jnp.int32, (tl, tl), 0)
    c2 = lax.broadcasted_iota(jnp.int32, (tl, tl), 1)
    incl2 = ((r2 >> cs_shift) == (c2 >> cs_shift)) & ((r2 <= c2) if backward else (r2 >= c2))
    tri = jnp.where(incl2, 1.0, 0.0).astype(BF16)
    g1 = g_all.astype(BF16)
    rem = g_all - g1.astype(F32)
    g2 = rem.astype(BF16)
    g3 = (rem - g2.astype(F32)).astype(BF16)
    gc_all = _dot(tri, g1) + _dot(tri, g2) + _dot(tri, g3)

    r = lax.broadcasted_iota(jnp.int32, (cs, cs), 0)
    c = lax.broadcasted_iota(jnp.int32, (cs, cs), 1)
    incl = (r <= c) if backward else (r >= c)
    strict = (r < c) if backward else (r > c)
    levels = []
    b = 1
    while b < cs:
        sh = b.bit_length() - 1
        levels.append(((r >> (sh + 1)) == (c >> (sh + 1))) & ((r >> sh) != (c >> sh)))
        b *= 2

    chunk_order = range(tl // cs - 1, -1, -1) if backward else range(tl // cs)
    streams = []
    for j in range(hpb):
        hs = slice(DN_DK * j, DN_DK * (j + 1))
        q = q_ref[:, hs].astype(F32)
        k = k_ref[:, hs].astype(F32)
        v = v_ref[:, hs].astype(F32)
        col = (DN_HEADS if backward else 0) + hg * hpb + j
        beta = jnp.sum(jnp.where(lane == col, beta_all, 0.0), axis=-1, keepdims=True)
        g = jnp.sum(jnp.where(lane == 2 * DN_HEADS + col, g_all, 0.0), axis=-1, keepdims=True)
        gcum = jnp.sum(jnp.where(lane == 2 * DN_HEADS + col, gc_all, 0.0), axis=-1, keepdims=True)
        for ci in chunk_order:
            rs = slice(ci * cs, (ci + 1) * cs)
            qc, kc, vc, bc = q[rs], k[rs], v[rs], beta[rs]
            gtot = jnp.sum(g[rs], axis=0, keepdims=True)
            gb = jnp.broadcast_to(gcum[rs], (cs, LANES))
            gj = jnp.transpose(gb)[0:1, :]
            decay = jnp.where(incl, jnp.exp(jnp.where(incl, gb[:, 0:1] - gj, 0.0)), 0.0)
            kb = kc * bc
            k16 = kc.astype(BF16)
            egc = jnp.exp(gb)
            streams.append(dict(
                j=j, rs=rs, hs=hs, gtot=gtot,
                m=jnp.where(strict, _dot_nt(kb.astype(BF16), k16) * decay, 0.0),
                qk=(_dot_nt(qc.astype(BF16), k16) * decay).astype(BF16),
                rhs=jnp.concatenate([vc * bc, kb * egc], axis=1),
                q_dec=qc * egc,
                k_dec_t=jnp.transpose(kc * jnp.exp(gtot - gb)).astype(BF16)))

    for st in streams:
        st["e"] = -jnp.where(levels[0], st["m"], 0.0)
    for mask in levels[1:]:
        for st in streams:
            lb = jnp.where(mask, st["m"], 0.0)
            st["lb"] = lb
            st["y"] = lb + _dot(lb.astype(BF16), st["e"].astype(BF16))
        for st in streams:
            st["e"] = st["e"] - st["y"] - _dot(st["e"].astype(BF16), st["y"].astype(BF16))
    for st in streams:
        st["uw"] = st["rhs"] + _dot(st["e"].astype(BF16), st["rhs"].astype(BF16))

    state = [s_ref[j] for j in range(hpb)]
    for step in range(tl // cs):
        cur = [streams[j * (tl // cs) + step] for j in range(hpb)]
        for st in cur:
            wq = jnp.concatenate([st["uw"][:, DN_DV:], st["q_dec"]], axis=0).astype(BF16)
            st["wqs"] = _dot(wq, state[st["j"]].astype(BF16))
        for st in cur:
            v16 = (st["uw"][:, :DN_DV] - st["wqs"][:cs]).astype(BF16)
            o_ref[st["rs"], st["hs"]] = st["wqs"][cs:] + _dot(st["qk"], v16)
            state[st["j"]] = state[st["j"]] * jnp.exp(st["gtot"]) + _dot(st["k_dec_t"], v16)
    for j in range(hpb):
        s_ref[j] = state[j]


def _deltanet(qkv, gates, prm, seq_len, backward):
    lp = qkv.shape[0]
    tl = TOKEN_TILE
    nt = lp // tl
    hpb = DN_HEADS_PER_STEP
    wb = hpb * DN_DK
    ngroups = DN_HEADS // hpb

    def tile(i):
        return nt - 1 - i if backward else i

    def main_spec(off):
        return pl.BlockSpec((tl, wb), lambda h, i: (tile(i), off + h))

    in_specs = [main_spec(0), main_spec(ngroups), main_spec(2 * ngroups),
                pl.BlockSpec((tl, LANES), lambda h, i: (tile(i), 0)),
                pl.BlockSpec((8, LANES), lambda h, i: (0, 0))]
    args = [qkv, qkv, qkv, gates, prm]
    return pl.pallas_call(
        functools.partial(_dn_kernel, seq_len=seq_len, tl=tl, backward=backward),
        grid=(ngroups, nt),
        in_specs=in_specs,
        out_specs=pl.BlockSpec((tl, wb), lambda h, i: (tile(i), h)),
        out_shape=jax.ShapeDtypeStruct((lp, DN_VW), F32),
        scratch_shapes=[pltpu.VMEM((hpb, DN_DK, DN_DV), F32)],
        compiler_params=_params(2),
        name="deltanet_bwd" if backward else "deltanet_fwd",
    )(*args)


def _gnorm_kernel(of_ref, ob_ref, z_ref, w_ref, o_ref, *, nh):
    for j in range(nh):
        sl = slice(DN_DV * j, DN_DV * (j + 1))
        o = of_ref[:, sl] + ob_ref[:, sl]
        o = o * lax.rsqrt(jnp.mean(o * o, axis=-1, keepdims=True) + NORM_EPS) * w_ref[...]
        z = z_ref[:, sl].astype(F32)
        o_ref[:, sl] = (o * (z * _sigmoid(z))).astype(o_ref.dtype)


def _gated_norm(o_f, o_b, proj_main, out_norm_w):
    lp = proj_main.shape[0]
    nh = 4
    wb = nh * DN_DV
    tr = _pick(lp, 768, 16)
    zoff = (2 * DN_QK + DN_VW) // wb
    return pl.pallas_call(
        functools.partial(_gnorm_kernel, nh=nh),
        grid=(lp // tr, DN_VW // wb),
        in_specs=[pl.BlockSpec((tr, wb), lambda i, j: (i, j)),
                  pl.BlockSpec((tr, wb), lambda i, j: (i, j)),
                  pl.BlockSpec((tr, wb), lambda i, j: (i, zoff + j)),
                  pl.BlockSpec((1, DN_DV), lambda i, j: (0, 0))],
        out_specs=pl.BlockSpec((tr, wb), lambda i, j: (i, j)),
        out_shape=jax.ShapeDtypeStruct((lp, DN_VW), BF16),
        compiler_params=_params(2),
        name="gated_norm",
    )(o_f, o_b, proj_main, out_norm_w.reshape(1, DN_DV).astype(F32))


def _attn_kernel(q_ref, k_ref, v_ref, prm_ref, o_ref, *, seq_len, tq, lp):
    groups = SWA_HQ // SWA_HKV
    heads = range(groups)
    prm = prm_ref[0]
    slopes = [prm[g:g + 1, 0:1] for g in heads]
    sinks = [prm[groups + g:groups + g + 1, 0:1] for g in heads]
    scale = SWA_D ** -0.5
    k_meta = k_ref[0:tq, :]
    v_meta = v_ref[0:tq, :]
    rq = lax.broadcasted_iota(jnp.int32, (tq, 1), 0)
    ck = lax.broadcasted_iota(jnp.int32, (1, 4 * tq), 1)
    meta_part = ck < tq
    win_part = jnp.logical_not(meta_part)

    def tile_bias(q0, k0):
        pq = q0 + rq
        pk = k0 + (ck - tq)
        real_q = pq >= N_META
        dist = pq - pk
        adist = jnp.abs(dist)
        in_win = (real_q & (adist <= SWA_WINDOW)) | (jnp.logical_not(real_q) & (-dist <= SWA_WINDOW))
        win_ok = win_part & (pk >= N_META) & (pk < seq_len) & in_win
        ok = (meta_part & (ck < N_META)) | win_ok
        return jnp.where(ok, jnp.where(real_q & win_part, -adist.astype(F32), 0.0), -jnp.inf)

    def tile(i, carry):
        q0 = pl.multiple_of(i * tq, tq)
        k0 = pl.multiple_of(jnp.clip(i * tq - tq, 0, lp - 3 * tq), tq)
        kall = jnp.concatenate([k_meta, k_ref[pl.ds(k0, 3 * tq), :]], axis=0)
        vall = jnp.concatenate([v_meta, v_ref[pl.ds(k0, 3 * tq), :]], axis=0)
        bias = tile_bias(q0, k0)
        s = [_dot_nt(q_ref[pl.ds(q0, tq), SWA_D * g:SWA_D * (g + 1)], kall) * scale + slopes[g] * bias
             for g in heads]
        m = [jnp.maximum(jnp.max(s[g], axis=-1, keepdims=True), sinks[g]) for g in heads]
        p = [jnp.exp(s[g] - m[g]) for g in heads]
        den = [jnp.sum(p[g], axis=-1, keepdims=True) + jnp.exp(sinks[g] - m[g]) for g in heads]
        o = [_dot(p[g].astype(BF16), vall) for g in heads]
        for g in heads:
            o_ref[pl.ds(q0, tq), SWA_D * g:SWA_D * (g + 1)] = (o[g] / den[g]).astype(o_ref.dtype)
        return carry

    nq = lp // tq
    lax.fori_loop(0, nq, tile, 0, unroll=2 if nq % 2 == 0 else 1)


def _window_attention(rest, attn_sink, seq_len):
    lp = rest.shape[0]
    tq = ATTN_TILE
    assert lp >= 3 * tq
    groups = SWA_HQ // SWA_HKV
    koff = SWA_QW // SWA_D
    voff = koff + SWA_HKV
    slopes = 2.0 ** (-8.0 * jnp.arange(1, SWA_HQ + 1, dtype=F32) / SWA_HQ)
    prm = jnp.concatenate([slopes.reshape(SWA_HKV, groups), attn_sink.astype(F32).reshape(SWA_HKV, groups)], axis=1)
    prm = jnp.broadcast_to(prm[:, :, None], (SWA_HKV, 2 * groups, LANES))
    qw = groups * SWA_D
    return pl.pallas_call(
        functools.partial(_attn_kernel, seq_len=seq_len, tq=tq, lp=lp),
        grid=(SWA_HKV,),
        in_specs=[pl.BlockSpec((lp, qw), lambda h: (0, h)),
                  pl.BlockSpec((lp, SWA_D), lambda h: (0, koff + h)),
                  pl.BlockSpec((lp, SWA_D), lambda h: (0, voff + h)),
                  pl.BlockSpec((1, 2 * groups, LANES), lambda h: (h, 0, 0))],
        out_specs=pl.BlockSpec((lp, qw), lambda h: (0, h)),
        out_shape=jax.ShapeDtypeStruct((lp, SWA_QW), BF16),
        compiler_params=_params(1),
        name="window_attention",
    )(rest, rest, rest, prm)


def _branch_kernel(oa_ref, ob_ref, wa_ref, wb_ref, ga_ref, gb_ref, o_ref, wa16_ref, wb16_ref):
    @pl.when(pl.program_id(1) == 0)
    def _():
        wa16_ref[...] = wa_ref[...].astype(BF16)
        wb16_ref[...] = wb_ref[...].astype(BF16)

    ya = _dot(oa_ref[...], wa16_ref[...])
    yb = _dot(ob_ref[...], wb16_ref[...])
    o = _sigmoid(ga_ref[...].astype(F32)) * ya + _sigmoid(gb_ref[...].astype(F32)) * yb
    o_ref[...] = o.astype(o_ref.dtype)


def _branch_merge(o_a, o_b, w_a, w_b, rest, d_model):
    lp = o_a.shape[0]
    tn = _pick(d_model, 512, LANES)
    tm = _pick(lp, 1056, 16)
    ga_off = (SWA_QW + 2 * SWA_KVW) // tn
    gb_off = (SWA_QW + 2 * SWA_KVW + d_model) // tn
    assert (SWA_QW + 2 * SWA_KVW) % tn == 0 and d_model % tn == 0
    return pl.pallas_call(
        _branch_kernel,
        grid=(d_model // tn, lp // tm),
        in_specs=[pl.BlockSpec((tm, DN_VW), lambda n, i: (i, 0)),
                  pl.BlockSpec((tm, SWA_QW), lambda n, i: (i, 0)),
                  pl.BlockSpec((DN_VW, tn), lambda n, i: (0, n)),
                  pl.BlockSpec((SWA_QW, tn), lambda n, i: (0, n)),
                  pl.BlockSpec((tm, tn), lambda n, i: (i, ga_off + n)),
                  pl.BlockSpec((tm, tn), lambda n, i: (i, gb_off + n))],
        out_specs=pl.BlockSpec((tm, tn), lambda n, i: (i, n)),
        out_shape=jax.ShapeDtypeStruct((lp, d_model), BF16),
        scratch_shapes=[pltpu.VMEM((DN_VW, tn), BF16), pltpu.VMEM((SWA_QW, tn), BF16)],
        compiler_params=_params(2),
        name="branch_merge",
    )(o_a, o_b, w_a, w_b, rest, rest)


def _router_kernel(h_ref, w_ref, wr_ref, xg_ref, afft_ref, *, seq_len, tr, d_model):
    i = pl.program_id(0)
    x = h_ref[...]
    xn = x * lax.rsqrt(jnp.mean(x * x, axis=-1, keepdims=True) + NORM_EPS) * w_ref[...]
    logits = _dot_nt(xn.astype(BF16), wr_ref[...].astype(BF16))
    lane = lax.broadcasted_iota(jnp.int32, (1, LANES), 1)
    lm = jnp.where(lane < N_EXPERTS, logits, -jnp.inf)
    ex = jnp.exp(lm - jnp.max(lm, axis=-1, keepdims=True))
    aff = ex / jnp.sum(ex, axis=-1, keepdims=True)
    row = i * tr + lax.broadcasted_iota(jnp.int32, (tr, 1), 0)
    aff = jnp.where((row < seq_len) & (lane < N_EXPERTS), aff, -1.0)
    nx = d_model // LANES
    for c in range(nx):
        xg_ref[:, c, :] = xn[:, LANES * c:LANES * (c + 1)]
    xg_ref[:, nx, :] = aff
    xg_ref[:, nx + 1:, :] = jnp.zeros((tr, ROW_PAD - 1, LANES), F32)
    afft_ref[...] = jnp.transpose(aff)[:N_EXPERTS, :]


def _router(h2, norm2_w, w_router, seq_len):
    lp, d = h2.shape
    tr = TOKEN_TILE
    wr = jnp.zeros((LANES, d), F32).at[:N_EXPERTS].set(jnp.swapaxes(w_router, 0, 1).astype(F32))
    nc = d // LANES + ROW_PAD
    return pl.pallas_call(
        functools.partial(_router_kernel, seq_len=seq_len, tr=tr, d_model=d),
        grid=(lp // tr,),
        in_specs=[pl.BlockSpec((tr, d), lambda i: (i, 0)),
                  pl.BlockSpec((1, d), lambda i: (0, 0)),
                  pl.BlockSpec((LANES, d), lambda i: (0, 0))],
        out_specs=[pl.BlockSpec((tr, nc, LANES), lambda i: (i, 0, 0)),
                   pl.BlockSpec((N_EXPERTS, tr), lambda i: (0, i))],
        out_shape=[jax.ShapeDtypeStruct((lp, nc, LANES), F32), jax.ShapeDtypeStruct((N_EXPERTS, lp), F32)],
        compiler_params=_params(1),
        name="router",
    )(h2, norm2_w.reshape(1, d).astype(F32), wr)


def _select_kernel(afft_ref, idx_ref, lo_ref, sel_ref, csum_ref, cols_ref, *, cap, slot_pad, lp, tile, n_tiles):
    aff = afft_ref[...]
    capf = float(cap)

    def count_ge(thr):
        return jnp.sum(jnp.where(aff >= thr, 1.0, 0.0), axis=1, keepdims=True)

    def bisect(_, carry):
        lo, hi = carry
        mid = 0.5 * (lo + hi)
        ge = count_ge(mid) >= capf
        return jnp.where(ge, mid, lo), jnp.where(ge, hi, mid)

    lo0 = jnp.zeros((N_EXPERTS, 1), F32)
    hi0 = jnp.full((N_EXPERTS, 1), 2.0, F32)
    _, hi = lax.fori_loop(0, 40, bisect, (lo0, hi0))

    def refine(st):
        hi, tau, done, _ = st
        cand = jnp.max(jnp.where(aff < hi, aff, -2.0), axis=1, keepdims=True)
        found = jnp.where(count_ge(cand) >= capf, 1.0, 0.0)
        tau = jnp.where(done > 0.0, tau, cand)
        hi = jnp.where(done + found > 0.0, hi, cand)
        done = jnp.maximum(done, found)
        return hi, tau, done, jnp.sum(1.0 - done)

    zero = jnp.zeros((N_EXPERTS, 1), F32)
    _, tau, _, _ = lax.while_loop(lambda st: st[3] > 0.0, refine, (hi, zero, zero, jnp.float32(N_EXPERTS)))
    need = capf - jnp.sum(jnp.where(aff > tau, 1.0, 0.0), axis=1, keepdims=True)

    ra = lax.broadcasted_iota(jnp.int32, (LANES, LANES), 0)
    ca = lax.broadcasted_iota(jnp.int32, (LANES, LANES), 1)
    ut = jnp.where(ra <= ca, 1.0, 0.0).astype(BF16)
    carry_eq = jnp.zeros((N_EXPERTS, 1), F32)
    carry_sel = jnp.zeros((N_EXPERTS, 1), F32)
    for j in range(lp // LANES):
        sl = slice(LANES * j, LANES * (j + 1))
        bj = aff[:, sl]
        eqf = jnp.where(bj == tau, 1.0, 0.0)
        tie_rank = _dot(eqf.astype(BF16), ut) + carry_eq - eqf
        carry_eq = carry_eq + jnp.sum(eqf, axis=1, keepdims=True)
        self = jnp.where((bj > tau) | ((bj == tau) & (tie_rank < need)), 1.0, 0.0)
        csum_ref[j] = _dot(self.astype(BF16), ut) + carry_sel
        carry_sel = carry_sel + jnp.sum(self, axis=1, keepdims=True)
        sel_ref[:, sl] = self

    lane = lax.broadcasted_iota(jnp.int32, (1, LANES), 1)
    sblk = 64
    cbase = lax.broadcasted_iota(jnp.int32, (sblk, 1), 0)
    cols_ref[...] = jnp.zeros_like(cols_ref)

    def per_expert(e, carry):
        def per_block(sb, carry2):
            cvals = (sb * sblk + cbase).astype(F32)

            def per_tile(j, acc):
                return acc + jnp.where(csum_ref[j, pl.ds(e, 1), :] <= cvals, 1.0, 0.0)

            n_lane_tiles = lp // LANES
            acc = lax.fori_loop(0, n_lane_tiles, per_tile, jnp.zeros((sblk, LANES), F32),
                                unroll=6 if n_lane_tiles % 6 == 0 else 1)
            rows = pl.ds(pl.multiple_of(sb * sblk, sblk), sblk)
            cols_ref[rows, :] = jnp.where(lane == e, jnp.sum(acc, axis=1, keepdims=True), cols_ref[rows, :])
            return carry2

        return lax.fori_loop(0, slot_pad // sblk, per_block, carry)

    lax.fori_loop(0, N_EXPERTS, per_expert, 0)
    idx_ref[...] = jnp.transpose(cols_ref[...])[:N_EXPERTS, :].astype(jnp.int32)

    sel = sel_ref[...]
    tok = lax.broadcasted_iota(jnp.int32, (1, lp), 1)
    table = jnp.zeros((N_EXPERTS, LANES), F32)
    for i in range(n_tiles + 1):
        below = jnp.sum(jnp.where(tok < N_META + tile * i, sel, 0.0), axis=1, keepdims=True)
        table = jnp.where(lane == i, below, table)
    lo_ref[...] = table.astype(jnp.int32)


def _select(afft, cap, slot_pad, tile, n_tiles):
    lp = afft.shape[1]
    assert n_tiles < LANES
    return pl.pallas_call(
        functools.partial(_select_kernel, cap=cap, slot_pad=slot_pad, lp=lp, tile=tile, n_tiles=n_tiles),
        out_shape=[jax.ShapeDtypeStruct((N_EXPERTS, slot_pad), jnp.int32),
                   jax.ShapeDtypeStruct((N_EXPERTS, LANES), jnp.int32)],
        scratch_shapes=[pltpu.VMEM((N_EXPERTS, lp), F32), pltpu.VMEM((lp // LANES, N_EXPERTS, LANES), F32),
                        pltpu.VMEM((slot_pad, LANES), F32)],
        compiler_params=pltpu.CompilerParams(vmem_limit_bytes=VMEM_LIMIT),
        name="expert_select",
    )(afft)


def _slab_copy(src, s_tok, dst, d_tok, sem, *, nc):
    s0 = pl.multiple_of(s_tok * nc, 8)
    d0 = pl.multiple_of(d_tok * nc, 8)
    return pltpu.make_async_copy(src.at[pl.ds(s0, nc)], dst.at[pl.ds(d0, nc)], sem)


def _dispatch_kernel(idx_ref, xg_ref, x_ref, aff_ref, slab_ref, sem, *, cap, cap_pad, nc, nx):
    copy = functools.partial(_slab_copy, nc=nc)
    e = pl.program_id(0)
    h0 = -(-(cap_pad // 2) // 16) * 16
    halves = ((0, h0, min(cap, h0)), (h0, cap_pad - h0, max(cap - h0, 0)))

    def fetch(ex, hf):
        first, _, used = halves[hf]

        def issue(s, c):
            copy(xg_ref, idx_ref[ex, first + s], slab_ref.at[hf], s, sem.at[hf]).start()
            return c

        lax.fori_loop(0, used, issue, 0, unroll=DMA_ISSUE_UNROLL if used % DMA_ISSUE_UNROLL == 0 else 1)

    def finish(hf):
        first, size, used = halves[hf]

        def drain(_, c):
            copy(xg_ref, 0, slab_ref.at[hf], 0, sem.at[hf]).wait()
            return c

        lax.fori_loop(0, used, drain, 0)
        rows = slice(first, first + size)
        for c in range(nx):
            x_ref[rows, LANES * c:LANES * (c + 1)] = slab_ref[hf, pl.ds(c, size, stride=nc), :].astype(BF16)
        aff_ref[rows, :] = slab_ref[hf, pl.ds(nx, size, stride=nc), :]

    @pl.when(e == 0)
    def _():
        for hf, (_, size, used) in enumerate(halves):
            if used < size:
                slab_ref[hf, used * nc:size * nc, :] = jnp.zeros(((size - used) * nc, LANES), F32)
        fetch(0, 0)

    fetch(e, 1)
    finish(0)

    @pl.when(e + 1 < N_EXPERTS)
    def _():
        fetch(e + 1, 0)

    finish(1)


def _dispatch(xg, idx, cap, cap_pad, d_model):
    nx = d_model // LANES
    nc = nx + ROW_PAD
    h0 = -(-(cap_pad // 2) // 16) * 16
    return pl.pallas_call(
        functools.partial(_dispatch_kernel, cap=cap, cap_pad=cap_pad, nc=nc, nx=nx),
        grid_spec=pltpu.PrefetchScalarGridSpec(
            num_scalar_prefetch=1,
            grid=(N_EXPERTS,),
            in_specs=[pl.BlockSpec(memory_space=pl.ANY)],
            out_specs=[pl.BlockSpec((cap_pad, d_model), lambda e, *_: (e, 0)),
                       pl.BlockSpec((cap_pad, LANES), lambda e, *_: (e, 0))],
            scratch_shapes=[pltpu.VMEM((2, h0 * nc, LANES), F32), pltpu.SemaphoreType.DMA((2,))]),
        out_shape=[jax.ShapeDtypeStruct((N_EXPERTS * cap_pad, d_model), BF16),
                   jax.ShapeDtypeStruct((N_EXPERTS * cap_pad, LANES), F32)],
        compiler_params=_params(1),
        name="expert_dispatch",
    )(idx, xg)


def _ffn_a_kernel(x_ref, wg_ref, wu_ref, o_ref):
    x16 = x_ref[...]
    g = _dot(x16, wg_ref[0].astype(BF16))
    u = _dot(x16, wu_ref[0].astype(BF16))
    o_ref[...] = (g * _sigmoid(g) * u).astype(o_ref.dtype)


def _ffn_a(xe, w_gate, w_up, cap_pad, d_model):
    _, _, ff = w_gate.shape
    tf = _pick(ff, 256, LANES)
    return pl.pallas_call(
        _ffn_a_kernel,
        grid=(N_EXPERTS, ff // tf),
        in_specs=[pl.BlockSpec((cap_pad, d_model), lambda e, f: (e, 0)),
                  pl.BlockSpec((1, d_model, tf), lambda e, f: (e, 0, f)),
                  pl.BlockSpec((1, d_model, tf), lambda e, f: (e, 0, f))],
        out_specs=pl.BlockSpec((cap_pad, tf), lambda e, f: (e, f)),
        out_shape=jax.ShapeDtypeStruct((N_EXPERTS * cap_pad, ff), BF16),
        compiler_params=_params(2),
        name="expert_ffn_in",
    )(xe, w_gate, w_up)


def _ffn_b_kernel(h_ref, wd_ref, aff_ref, o_ref):
    e = pl.program_id(0)
    y = _dot(h_ref[...], wd_ref[0].astype(BF16))
    lane = lax.broadcasted_iota(jnp.int32, (1, LANES), 1)
    gate = jnp.sum(jnp.where(lane == e, aff_ref[...], 0.0), axis=-1, keepdims=True)
    y = y * gate
    for c in range(y.shape[1] // LANES):
        o_ref[:, c, :] = y[:, LANES * c:LANES * (c + 1)]


def _ffn_b(hid, w_down, aff, cap_pad, d_model):
    _, ff, _ = w_down.shape
    td = _pick(d_model, 1024, 8 * LANES) if d_model % (8 * LANES) == 0 else d_model
    return pl.pallas_call(
        _ffn_b_kernel,
        grid=(N_EXPERTS, d_model // td),
        in_specs=[pl.BlockSpec((cap_pad, ff), lambda e, n: (e, 0)),
                  pl.BlockSpec((1, ff, td), lambda e, n: (e, 0, n)),
                  pl.BlockSpec((cap_pad, LANES), lambda e, n: (e, 0))],
        out_specs=pl.BlockSpec((cap_pad, td // LANES, LANES), lambda e, n: (e, n, 0)),
        out_shape=jax.ShapeDtypeStruct((N_EXPERTS * cap_pad, d_model // LANES, LANES), F32),
        compiler_params=_params(2),
        name="expert_ffn_out",
    )(hid, w_down, aff)


def _combine_kernel(idx_ref, lo_ref, h_ref, ye_ref, nw_ref, o_ref, hbuf_ref, acc_ref, stage_ref, sem, hsem, *,
                    tt, nx, cap_pad):
    i = pl.program_id(0)
    t0 = N_META + i * tt
    ch = COMBINE_CHUNK
    reg = COMBINE_REGION
    pitch = nx + ROW_PAD
    slot = i % 2

    def residual_copy(step, buf):
        rows = pl.ds(pl.multiple_of(N_META + step * tt, 8), tt)
        return pltpu.make_async_copy(h_ref.at[rows], hbuf_ref.at[buf], hsem.at[buf])

    @pl.when(i == 0)
    def _():
        residual_copy(0, 0).start()

    @pl.when(i + 1 < pl.num_programs(0))
    def _():
        residual_copy(i + 1, 1 - slot).start()

    def round_chunks(e, r):
        left = lo_ref[e, i + 1] - lo_ref[e, i] - r * reg
        return (jnp.minimum(left, reg) + ch - 1) // ch

    def chunk_copy(e, r, j):
        src = pl.multiple_of((e * cap_pad + lo_ref[e, i] + r * reg + j * ch) * nx, 8)
        dst = pl.multiple_of(j * ch * nx, 8)
        return pltpu.make_async_copy(ye_ref.at[pl.ds(src, ch * nx)], stage_ref.at[e, pl.ds(dst, ch * nx)],
                                     sem.at[e])

    def fetch(e, r):
        def body(j, c):
            chunk_copy(e, r, j).start()
            return c

        lax.fori_loop(0, round_chunks(e, r), body, 0)

    def add_round(e, r):
        def drain(j, c):
            chunk_copy(e, r, j).wait()
            return c

        lax.fori_loop(0, round_chunks(e, r), drain, 0)
        first = lo_ref[e, i] + r * reg

        def per_slot(s, c):
            src = pl.ds(pl.multiple_of((s - first) * nx, 8), nx)
            dst = pl.ds(pl.multiple_of((idx_ref[e, s] - t0) * pitch, 8), nx)
            acc_ref[dst, :] = acc_ref[dst, :] + stage_ref[e, src, :]
            return c

        lax.fori_loop(first, jnp.minimum(first + reg, lo_ref[e, i + 1]), per_slot, 0)

    def prefetch(e, c):
        fetch(e, 0)
        return c

    lax.fori_loop(0, N_EXPERTS, prefetch, 0)
    residual_copy(i, slot).wait()
    for c in range(nx):
        acc_ref[pl.ds(c, tt, stride=pitch), :] = hbuf_ref[slot, :, LANES * c:LANES * (c + 1)]

    def per_expert(e, carry):
        add_round(e, 0)

        def later(r, c):
            fetch(e, r)
            add_round(e, r)
            return c

        n_rounds = (lo_ref[e, i + 1] - lo_ref[e, i] + reg - 1) // reg
        lax.fori_loop(1, n_rounds, later, 0)
        return carry

    lax.fori_loop(0, N_EXPERTS, per_expert, 0)
    for c in range(nx):
        hbuf_ref[slot, :, LANES * c:LANES * (c + 1)] = acc_ref[pl.ds(c, tt, stride=pitch), :]
    x = hbuf_ref[slot]
    o_ref[...] = x * lax.rsqrt(jnp.mean(x * x, axis=-1, keepdims=True) + NORM_EPS) * nw_ref[...]


def _combine(idx, lo, h2, ye, norm_f_w, n_real, tt, cap_pad):
    d = h2.shape[1]
    nx = d // LANES
    assert COMBINE_REGION % COMBINE_CHUNK == 0
    return pl.pallas_call(
        functools.partial(_combine_kernel, tt=tt, nx=nx, cap_pad=cap_pad),
        grid_spec=pltpu.PrefetchScalarGridSpec(
            num_scalar_prefetch=2,
            grid=(n_real // tt,),
            in_specs=[pl.BlockSpec(memory_space=pl.ANY),
                      pl.BlockSpec(memory_space=pl.ANY),
                      pl.BlockSpec((1, d), lambda i, *_: (0, 0))],
            out_specs=pl.BlockSpec((tt, d), lambda i, *_: (i, 0)),
            scratch_shapes=[pltpu.VMEM((2, tt, d), F32), pltpu.VMEM((tt * (nx + ROW_PAD), LANES), F32),
                            pltpu.VMEM((N_EXPERTS, COMBINE_REGION * nx, LANES), F32),
                            pltpu.SemaphoreType.DMA((N_EXPERTS,)), pltpu.SemaphoreType.DMA((2,))]),
        out_shape=jax.ShapeDtypeStruct((n_real, d), F32),
        compiler_params=_params(1),
        name="expert_combine",
    )(idx, lo, h2, ye, norm_f_w.reshape(1, d).astype(F32))


def _layer(hp, seq_len, norm1_w, w_in, conv_w, a_log_fwd, a_log_bwd, dt_bias_fwd, dt_bias_bwd, out_norm_w,
           w_branch_a, attn_sink, w_branch_b, w_out, norm2_w, w_router, w_gate, w_up, w_down):
    lp, d = hp.shape
    main_w = 2 * DN_QK + 2 * DN_VW
    gate_w = 4 * DN_HEADS
    rest_w = SWA_QW + 2 * SWA_KVW + 2 * d
    n = _rmsnorm(hp, norm1_w, BF16)
    w_t = jnp.swapaxes(w_in, 0, 1)
    proj_main = _matmul(n, w_t, main_w, 0, BF16, w_rows=True, name="in_proj_main")
    gates = _matmul(n, w_t, LANES, main_w, F32, w_rows=True, name="in_proj_gates")
    rest = _matmul(n, w_t, rest_w, main_w + gate_w, BF16, w_rows=True, name="in_proj_rest")

    prm = jnp.zeros((8, LANES), F32)
    prm = prm.at[0, 2 * DN_HEADS:4 * DN_HEADS].set(jnp.concatenate([a_log_fwd, a_log_bwd]).astype(F32))
    prm = prm.at[1, 2 * DN_HEADS:4 * DN_HEADS].set(jnp.concatenate([dt_bias_fwd, dt_bias_bwd]).astype(F32))
    qkv = _dn_prep(proj_main, conv_w.astype(F32), seq_len)
    o_f = _deltanet(qkv, gates, prm, seq_len, backward=False)
    o_r = _deltanet(qkv, gates, prm, seq_len, backward=True)
    o_a = _gated_norm(o_f, o_r, proj_main, out_norm_w)
    o_b = _window_attention(rest, attn_sink, seq_len)

    mixed = _branch_merge(o_a, o_b, w_branch_a, w_branch_b, rest, d)
    h2 = _matmul(mixed, w_out, d, 0, F32, res=hp, name="out_proj")

    cap = EC_CAPACITY * seq_len // N_EXPERTS
    cap_pad = -(-(cap + COMBINE_CHUNK) // 16) * 16
    xg, afft = _router(h2, norm2_w, w_router, seq_len)
    n_real = seq_len - N_META
    tt = _pick(n_real, TOKEN_TILE, LANES)
    slot_pad = -(-cap_pad // LANES) * LANES
    idx, lo = _select(afft, cap, slot_pad, tt, n_real // tt)
    xe, aff = _dispatch(xg.reshape(-1, LANES), idx, cap, cap_pad, d)
    hid = _ffn_a(xe, w_gate, w_up, cap_pad, d)
    ye = _ffn_b(hid, w_down, aff, cap_pad, d)
    return functools.partial(_combine, idx, lo, h2, ye.reshape(-1, LANES), n_real=n_real, tt=tt, cap_pad=cap_pad)


def kernel(x, meta_tokens, norm1_w, w_in, conv_w, a_log_fwd, a_log_bwd, dt_bias_fwd, dt_bias_bwd, out_norm_w,
           w_branch_a, attn_sink, w_branch_b, w_out, norm2_w, w_router, w_gate, w_up, w_down, norm_f_w):
    batch, seq, d = x.shape
    depth = norm1_w.shape[0]
    assert depth == 1, "the final norm is fused into the last layer's expert combine"
    seq_len = N_META + seq
    lp = -(-seq_len // TOKEN_TILE) * TOKEN_TILE
    outs = []
    for b in range(batch):
        h = jnp.concatenate([meta_tokens.astype(x.dtype), x[b]], axis=0)
        hp = jnp.pad(h, ((0, lp - seq_len), (0, 0)))
        combine = _layer(hp, seq_len, norm1_w[0], w_in[0], conv_w[0], a_log_fwd[0], a_log_bwd[0],
                         dt_bias_fwd[0], dt_bias_bwd[0], out_norm_w[0], w_branch_a[0], attn_sink[0],
                         w_branch_b[0], w_out[0], norm2_w[0], w_router[0], w_gate[0], w_up[0], w_down[0])
        outs.append(combine(norm_f_w=norm_f_w))
    return jnp.stack(outs, axis=0)
```

```python
import functools

import jax
import jax.numpy as jnp
from jax import lax
from jax.experimental import pallas as pl
from jax.experimental.pallas import tpu as pltpu

F32 = jnp.float32
BF16 = jnp.bfloat16

N_META = 16
NORM_EPS = 1e-6
DN_HEADS = 16
DN_DK = 128
DN_DV = 128
SWA_HQ = 16
SWA_HKV = 4
SWA_D = 128
SWA_WINDOW = 128
N_EXPERTS = 16
EC_CAPACITY = 2

LANES = 128
TOKEN_TILE = 256
DN_CHUNK = 128
DN_HEADS_PER_STEP = 8
ATTN_TILE = 128
ROW_PAD = 8
DMA_ISSUE_UNROLL = 6
COMBINE_CHUNK = 8
COMBINE_REGION = 64
VMEM_LIMIT = 56 * 1024 * 1024

DN_QK = DN_HEADS * DN_DK
DN_VW = DN_HEADS * DN_DV
SWA_QW = SWA_HQ * SWA_D
SWA_KVW = SWA_HKV * SWA_D


def _params(n_grid):
    return pltpu.CompilerParams(dimension_semantics=("arbitrary",) * n_grid, vmem_limit_bytes=VMEM_LIMIT)


def _pick(n, target, mult):
    best = None
    for t in range(mult, min(n, target) + 1, mult):
        if n % t == 0:
            best = t
    assert best is not None, (n, target, mult)
    return best


def _sigmoid(x):
    return 1.0 / (1.0 + jnp.exp(-x))


def _dot(a, b):
    return jnp.dot(a, b, preferred_element_type=F32)


def _dot_nt(a, b):
    return lax.dot_general(a, b, (((1,), (1,)), ((), ())), preferred_element_type=F32)


def _rms_kernel(x_ref, w_ref, o_ref):
    x = x_ref[...]
    ms = jnp.mean(x * x, axis=-1, keepdims=True)
    o_ref[...] = (x * lax.rsqrt(ms + NORM_EPS) * w_ref[...]).astype(o_ref.dtype)


def _rmsnorm(x, w, out_dtype):
    m, d = x.shape
    tr = _pick(m, 256, 16)
    return pl.pallas_call(
        _rms_kernel,
        grid=(m // tr,),
        in_specs=[pl.BlockSpec((tr, d), lambda i: (i, 0)), pl.BlockSpec((1, d), lambda i: (0, 0))],
        out_specs=pl.BlockSpec((tr, d), lambda i: (i, 0)),
        out_shape=jax.ShapeDtypeStruct((m, d), out_dtype),
        compiler_params=_params(1),
        name="rmsnorm",
    )(x, w.reshape(1, d).astype(F32))


def _mm_kernel(*refs, has_res, w_rows):
    if has_res:
        a_ref, w_ref, r_ref, o_ref, wb_ref = refs
    else:
        a_ref, w_ref, o_ref, wb_ref = refs

    @pl.when(pl.program_id(1) == 0)
    def _():
        w = w_ref[...]
        wb_ref[...] = (jnp.transpose(w) if w_rows else w).astype(BF16)

    acc = _dot(a_ref[...], wb_ref[...])
    if has_res:
        acc = acc + r_ref[...]
    o_ref[...] = acc.astype(o_ref.dtype)


def _matmul(a, w, n_cols, off, out_dtype, res=None, w_rows=False, name="matmul"):
    m, k = a.shape
    tn = _pick(n_cols, 512, LANES)
    tm = _pick(m, 1408 if res is None else 1056, 16)
    if w_rows:
        if off % tn == 0:
            w_spec = pl.BlockSpec((tn, k), lambda n, i: (off // tn + n, 0))
        else:
            assert off % 8 == 0
            w_spec = pl.BlockSpec((pl.Element(tn), pl.Element(k)), lambda n, i: (pl.multiple_of(off + n * tn, 8), 0))
    else:
        assert off % tn == 0
        w_spec = pl.BlockSpec((k, tn), lambda n, i: (0, off // tn + n))
    in_specs = [pl.BlockSpec((tm, k), lambda n, i: (i, 0)), w_spec]
    args = [a, w]
    if res is not None:
        in_specs.append(pl.BlockSpec((tm, tn), lambda n, i: (i, n)))
        args.append(res)
    return pl.pallas_call(
        functools.partial(_mm_kernel, has_res=res is not None, w_rows=w_rows),
        grid=(n_cols // tn, m // tm),
        in_specs=in_specs,
        out_specs=pl.BlockSpec((tm, tn), lambda n, i: (i, n)),
        out_shape=jax.ShapeDtypeStruct((m, n_cols), out_dtype),
        scratch_shapes=[pltpu.VMEM((k, tn), BF16)],
        compiler_params=_params(2),
        name=name,
    )(*args)


def _dn_prep_kernel(x_ref, p_ref, n_ref, c_ref, o_ref, *, seq_len, tl, nh, q_blocks, k_blocks):
    i = pl.program_id(0)
    j = pl.program_id(1)
    nt = pl.num_programs(0)
    row = lax.broadcasted_iota(jnp.int32, (tl, 1), 0)
    validf = ((i * tl + row) < seq_len).astype(F32)
    x = x_ref[...].astype(F32)
    prev = p_ref[...].astype(F32)[15:16, :] * (i > 0).astype(F32)
    nxt = n_ref[...].astype(F32)[0:1, :] * (i < nt - 1).astype(F32)
    xm1 = jnp.where(row == 0, prev, pltpu.roll(x, 1, 0))
    xp1 = jnp.where(row == tl - 1, nxt, pltpu.roll(x, tl - 1, 0))
    c = c_ref[...]
    y = xm1 * c[0:1] + x * c[1:2] + xp1 * c[2:3]
    y = y * _sigmoid(y)
    is_qk = (j < q_blocks + k_blocks).astype(F32)
    q_scale = jnp.where(j < q_blocks, DN_DK ** -0.5, 1.0)
    for h in range(nh):
        hs = slice(DN_DK * h, DN_DK * (h + 1))
        yh = y[:, hs]
        norm = lax.rsqrt(jnp.sum(yh * yh, axis=-1, keepdims=True) + NORM_EPS) * q_scale
        o_ref[:, hs] = (yh * (is_qk * norm + (1.0 - is_qk)) * validf).astype(o_ref.dtype)


def _dn_prep(proj_main, conv_w, seq_len):
    lp = proj_main.shape[0]
    tl = _pick(lp, 3 * TOKEN_TILE, TOKEN_TILE)
    nh = 8
    wb = nh * DN_DK
    width = 2 * DN_QK + DN_VW
    hb = tl // 16
    nhb = lp // 16
    return pl.pallas_call(
        functools.partial(_dn_prep_kernel, seq_len=seq_len, tl=tl, nh=nh, q_blocks=DN_QK // wb, k_blocks=DN_QK // wb),
        grid=(lp // tl, width // wb),
        in_specs=[pl.BlockSpec((tl, wb), lambda i, j: (i, j)),
                  pl.BlockSpec((16, wb), lambda i, j: (jnp.maximum(i * hb - 1, 0), j)),
                  pl.BlockSpec((16, wb), lambda i, j: (jnp.minimum((i + 1) * hb, nhb - 1), j)),
                  pl.BlockSpec((3, wb), lambda i, j: (0, j))],
        out_specs=pl.BlockSpec((tl, wb), lambda i, j: (i, j)),
        out_shape=jax.ShapeDtypeStruct((lp, width), BF16),
        compiler_params=_params(2),
        name="deltanet_prep",
    )(proj_main, proj_main, proj_main, conv_w)


def _dn_kernel(q_ref, k_ref, v_ref, sm_ref, prm_ref, o_ref, s_ref, *, seq_len, tl, backward):
    hg = pl.program_id(0)
    i = pl.program_id(1)
    nt = pl.num_programs(1)
    ti = nt - 1 - i if backward else i
    cs = DN_CHUNK
    cs_shift = cs.bit_length() - 1
    hpb = DN_HEADS_PER_STEP

    @pl.when(i == 0)
    def _():
        s_ref[...] = jnp.zeros_like(s_ref)

    row = lax.broadcasted_iota(jnp.int32, (tl, 1), 0)
    validf = ((ti * tl + row) < seq_len).astype(F32)

    lane = lax.broadcasted_iota(jnp.int32, (1, LANES), 1)
    sm = sm_ref[...]
    prm = prm_ref[...]
    z = sm + prm[1:2]
    softplus = jnp.maximum(z, 0.0) + jnp.log1p(jnp.exp(-jnp.abs(z)))
    beta_all = _sigmoid(sm) * validf
    g_all = -jnp.exp(prm[0:1]) * softplus * validf

    r2 = lax.broadcasted_iota(jnp.int32, (tl, tl), 0)
    c2 = lax.broadcasted_iota(jnp.int32, (tl, tl), 1)
    incl2 = ((r2 >> cs_shift) == (c2 >> cs_shift)) & ((r2 <= c2) if backward else (r2 >= c2))
    tri = jnp.where(incl2, 1.0, 0.0).astype(BF16)
    g1 = g_all.astype(BF16)
    rem = g_all - g1.astype(F32)
    g2 = rem.astype(BF16)
    g3 = (rem - g2.astype(F32)).astype(BF16)
    gc_all = _dot(tri, g1) + _dot(tri, g2) + _dot(tri, g3)

    r = lax.broadcasted_iota(jnp.int32, (cs, cs), 0)
    c = lax.broadcasted_iota(jnp.int32, (cs, cs), 1)
    incl = (r <= c) if backward else (r >= c)
    strict = (r < c) if backward else (r > c)
    levels = []
    b = 1
    while b < cs:
        sh = b.bit_length() - 1
        levels.append(((r >> (sh + 1)) == (c >> (sh + 1))) & ((r >> sh) != (c >> sh)))
        b *= 2

    chunk_order = range(tl // cs - 1, -1, -1) if backward else range(tl // cs)
    streams = []
    for j in range(hpb):
        hs = slice(DN_DK * j, DN_DK * (j + 1))
        q = q_ref[:, hs].astype(F32)
        k = k_ref[:, hs].astype(F32)
        v = v_ref[:, hs].astype(F32)
        col = (DN_HEADS if backward else 0) + hg * hpb + j
        beta = jnp.sum(jnp.where(lane == col, beta_all, 0.0), axis=-1, keepdims=True)
        g = jnp.sum(jnp.where(lane == 2 * DN_HEADS + col, g_all, 0.0), axis=-1, keepdims=True)
        gcum = jnp.sum(jnp.where(lane == 2 * DN_HEADS + col, gc_all, 0.0), axis=-1, keepdims=True)
        for ci in chunk_order:
            rs = slice(ci * cs, (ci + 1) * cs)
            qc, kc, vc, bc = q[rs], k[rs], v[rs], beta[rs]
            gtot = jnp.sum(g[rs], axis=0, keepdims=True)
            gb = jnp.broadcast_to(gcum[rs], (cs, LANES))
            gj = jnp.transpose(gb)[0:1, :]
            decay = jnp.where(incl, jnp.exp(jnp.where(incl, gb[:, 0:1] - gj, 0.0)), 0.0)
            kb = kc * bc
            k16 = kc.astype(BF16)
            egc = jnp.exp(gb)
            streams.append(dict(
                j=j, rs=rs, hs=hs, gtot=gtot,
                m=jnp.where(strict, _dot_nt(kb.astype(BF16), k16) * decay, 0.0),
                qk=(_dot_nt(qc.astype(BF16), k16) * decay).astype(BF16),
                rhs=jnp.concatenate([vc * bc, kb * egc], axis=1),
                q_dec=qc * egc,
                k_dec_t=jnp.transpose(kc * jnp.exp(gtot - gb)).astype(BF16)))

    for st in streams:
        st["e"] = -jnp.where(levels[0], st["m"], 0.0)
    for mask in levels[1:]:
        for st in streams:
            lb = jnp.where(mask, st["m"], 0.0)
            st["lb"] = lb
            st["y"] = lb + _dot(lb.astype(BF16), st["e"].astype(BF16))
        for st in streams:
            st["e"] = st["e"] - st["y"] - _dot(st["e"].astype(BF16), st["y"].astype(BF16))
    for st in streams:
        st["uw"] = st["rhs"] + _dot(st["e"].astype(BF16), st["rhs"].astype(BF16))

    state = [s_ref[j] for j in range(hpb)]
    for step in range(tl // cs):
        cur = [streams[j * (tl // cs) + step] for j in range(hpb)]
        for st in cur:
            wq = jnp.concatenate([st["uw"][:, DN_DV:], st["q_dec"]], axis=0).astype(BF16)
            st["wqs"] = _dot(wq, state[st["j"]].astype(BF16))
        for st in cur:
            v16 = (st["uw"][:, :DN_DV] - st["wqs"][:cs]).astype(BF16)
            o_ref[st["rs"], st["hs"]] = (st["wqs"][cs:] + _dot(st["qk"], v16)).astype(o_ref.dtype)
            state[st["j"]] = state[st["j"]] * jnp.exp(st["gtot"]) + _dot(st["k_dec_t"], v16)
    for j in range(hpb):
        s_ref[j] = state[j]


def _deltanet(qkv, gates, prm, seq_len, backward):
    lp = qkv.shape[0]
    tl = TOKEN_TILE
    nt = lp // tl
    hpb = DN_HEADS_PER_STEP
    wb = hpb * DN_DK
    ngroups = DN_HEADS // hpb

    def tile(i):
        return nt - 1 - i if backward else i

    def main_spec(off):
        return pl.BlockSpec((tl, wb), lambda h, i: (tile(i), off + h))

    in_specs = [main_spec(0), main_spec(ngroups), main_spec(2 * ngroups),
                pl.BlockSpec((tl, LANES), lambda h, i: (tile(i), 0)),
                pl.BlockSpec((8, LANES), lambda h, i: (0, 0))]
    args = [qkv, qkv, qkv, gates, prm]
    return pl.pallas_call(
        functools.partial(_dn_kernel, seq_len=seq_len, tl=tl, backward=backward),
        grid=(ngroups, nt),
        in_specs=in_specs,
        out_specs=pl.BlockSpec((tl, wb), lambda h, i: (tile(i), h)),
        out_shape=jax.ShapeDtypeStruct((lp, DN_VW), BF16),
        scratch_shapes=[pltpu.VMEM((hpb, DN_DK, DN_DV), F32)],
        compiler_params=_params(2),
        name="deltanet_bwd" if backward else "deltanet_fwd",
    )(*args)


def _gnorm_kernel(of_ref, ob_ref, z_ref, w_ref, o_ref, *, nh):
    for j in range(nh):
        sl = slice(DN_DV * j, DN_DV * (j + 1))
        o = of_ref[:, sl].astype(F32) + ob_ref[:, sl].astype(F32)
        o = o * lax.rsqrt(jnp.mean(o * o, axis=-1, keepdims=True) + NORM_EPS) * w_ref[...]
        z = z_ref[:, sl].astype(F32)
        o_ref[:, sl] = (o * (z * _sigmoid(z))).astype(o_ref.dtype)


def _gated_norm(o_f, o_b, proj_main, out_norm_w):
    lp = proj_main.shape[0]
    nh = 4
    wb = nh * DN_DV
    tr = _pick(lp, 768, 16)
    zoff = (2 * DN_QK + DN_VW) // wb
    return pl.pallas_call(
        functools.partial(_gnorm_kernel, nh=nh),
        grid=(lp // tr, DN_VW // wb),
        in_specs=[pl.BlockSpec((tr, wb), lambda i, j: (i, j)),
                  pl.BlockSpec((tr, wb), lambda i, j: (i, j)),
                  pl.BlockSpec((tr, wb), lambda i, j: (i, zoff + j)),
                  pl.BlockSpec((1, DN_DV), lambda i, j: (0, 0))],
        out_specs=pl.BlockSpec((tr, wb), lambda i, j: (i, j)),
        out_shape=jax.ShapeDtypeStruct((lp, DN_VW), BF16),
        compiler_params=_params(2),
        name="gated_norm",
    )(o_f, o_b, proj_main, out_norm_w.reshape(1, DN_DV).astype(F32))


def _attn_kernel(q_ref, k_ref, v_ref, prm_ref, o_ref, *, seq_len, tq, lp):
    groups = SWA_HQ // SWA_HKV
    heads = range(groups)
    prm = prm_ref[0]
    slopes = [prm[g:g + 1, 0:1] for g in heads]
    sinks = [prm[groups + g:groups + g + 1, 0:1] for g in heads]
    scale = SWA_D ** -0.5
    k_meta = k_ref[0:tq, :]
    v_meta = v_ref[0:tq, :]
    rq = lax.broadcasted_iota(jnp.int32, (tq, 1), 0)
    ck = lax.broadcasted_iota(jnp.int32, (1, 4 * tq), 1)
    meta_part = ck < tq
    win_part = jnp.logical_not(meta_part)

    def tile_bias(q0, k0):
        pq = q0 + rq
        pk = k0 + (ck - tq)
        real_q = pq >= N_META
        dist = pq - pk
        adist = jnp.abs(dist)
        in_win = (real_q & (adist <= SWA_WINDOW)) | (jnp.logical_not(real_q) & (-dist <= SWA_WINDOW))
        win_ok = win_part & (pk >= N_META) & (pk < seq_len) & in_win
        ok = (meta_part & (ck < N_META)) | win_ok
        return jnp.where(ok, jnp.where(real_q & win_part, -adist.astype(F32), 0.0), -jnp.inf)

    def tile(i, carry):
        q0 = pl.multiple_of(i * tq, tq)
        k0 = pl.multiple_of(jnp.clip(i * tq - tq, 0, lp - 3 * tq), tq)
        kall = jnp.concatenate([k_meta, k_ref[pl.ds(k0, 3 * tq), :]], axis=0)
        vall = jnp.concatenate([v_meta, v_ref[pl.ds(k0, 3 * tq), :]], axis=0)
        bias = tile_bias(q0, k0)
        s = [_dot_nt(q_ref[pl.ds(q0, tq), SWA_D * g:SWA_D * (g + 1)], kall) * scale + slopes[g] * bias
             for g in heads]
        m = [jnp.maximum(jnp.max(s[g], axis=-1, keepdims=True), sinks[g]) for g in heads]
        p = [jnp.exp(s[g] - m[g]) for g in heads]
        den = [jnp.sum(p[g], axis=-1, keepdims=True) + jnp.exp(sinks[g] - m[g]) for g in heads]
        o = [_dot(p[g].astype(BF16), vall) for g in heads]
        for g in heads:
            o_ref[pl.ds(q0, tq), SWA_D * g:SWA_D * (g + 1)] = (o[g] / den[g]).astype(o_ref.dtype)
        return carry

    nq = lp // tq
    lax.fori_loop(0, nq, tile, 0, unroll=2 if nq % 2 == 0 else 1)


def _window_attention(rest, attn_sink, seq_len):
    lp = rest.shape[0]
    tq = ATTN_TILE
    assert lp >= 3 * tq
    groups = SWA_HQ // SWA_HKV
    koff = SWA_QW // SWA_D
    voff = koff + SWA_HKV
    slopes = 2.0 ** (-8.0 * jnp.arange(1, SWA_HQ + 1, dtype=F32) / SWA_HQ)
    prm = jnp.concatenate([slopes.reshape(SWA_HKV, groups), attn_sink.astype(F32).reshape(SWA_HKV, groups)], axis=1)
    prm = jnp.broadcast_to(prm[:, :, None], (SWA_HKV, 2 * groups, LANES))
    qw = groups * SWA_D
    return pl.pallas_call(
        functools.partial(_attn_kernel, seq_len=seq_len, tq=tq, lp=lp),
        grid=(SWA_HKV,),
        in_specs=[pl.BlockSpec((lp, qw), lambda h: (0, h)),
                  pl.BlockSpec((lp, SWA_D), lambda h: (0, koff + h)),
                  pl.BlockSpec((lp, SWA_D), lambda h: (0, voff + h)),
                  pl.BlockSpec((1, 2 * groups, LANES), lambda h: (h, 0, 0))],
        out_specs=pl.BlockSpec((lp, qw), lambda h: (0, h)),
        out_shape=jax.ShapeDtypeStruct((lp, SWA_QW), BF16),
        compiler_params=_params(1),
        name="window_attention",
    )(rest, rest, rest, prm)


def _branch_kernel(oa_ref, ob_ref, wa_ref, wb_ref, ga_ref, gb_ref, o_ref, wa16_ref, wb16_ref):
    @pl.when(pl.program_id(1) == 0)
    def _():
        wa16_ref[...] = wa_ref[...].astype(BF16)
        wb16_ref[...] = wb_ref[...].astype(BF16)

    ya = _dot(oa_ref[...], wa16_ref[...])
    yb = _dot(ob_ref[...], wb16_ref[...])
    o = _sigmoid(ga_ref[...].astype(F32)) * ya + _sigmoid(gb_ref[...].astype(F32)) * yb
    o_ref[...] = o.astype(o_ref.dtype)


def _branch_merge(o_a, o_b, w_a, w_b, rest, d_model):
    lp = o_a.shape[0]
    tn = _pick(d_model, 512, LANES)
    tm = _pick(lp, 1056, 16)
    ga_off = (SWA_QW + 2 * SWA_KVW) // tn
    gb_off = (SWA_QW + 2 * SWA_KVW + d_model) // tn
    assert (SWA_QW + 2 * SWA_KVW) % tn == 0 and d_model % tn == 0
    return pl.pallas_call(
        _branch_kernel,
        grid=(d_model // tn, lp // tm),
        in_specs=[pl.BlockSpec((tm, DN_VW), lambda n, i: (i, 0)),
                  pl.BlockSpec((tm, SWA_QW), lambda n, i: (i, 0)),
                  pl.BlockSpec((DN_VW, tn), lambda n, i: (0, n)),
                  pl.BlockSpec((SWA_QW, tn), lambda n, i: (0, n)),
                  pl.BlockSpec((tm, tn), lambda n, i: (i, ga_off + n)),
                  pl.BlockSpec((tm, tn), lambda n, i: (i, gb_off + n))],
        out_specs=pl.BlockSpec((tm, tn), lambda n, i: (i, n)),
        out_shape=jax.ShapeDtypeStruct((lp, d_model), BF16),
        scratch_shapes=[pltpu.VMEM((DN_VW, tn), BF16), pltpu.VMEM((SWA_QW, tn), BF16)],
        compiler_params=_params(2),
        name="branch_merge",
    )(o_a, o_b, w_a, w_b, rest, rest)


def _router_kernel(h_ref, w_ref, wr_ref, xg_ref, afft_ref, *, seq_len, tr, d_model):
    i = pl.program_id(0)
    x = h_ref[...]
    xn = x * lax.rsqrt(jnp.mean(x * x, axis=-1, keepdims=True) + NORM_EPS) * w_ref[...]
    logits = _dot_nt(xn.astype(BF16), wr_ref[...].astype(BF16))
    lane = lax.broadcasted_iota(jnp.int32, (1, LANES), 1)
    lm = jnp.where(lane < N_EXPERTS, logits, -jnp.inf)
    ex = jnp.exp(lm - jnp.max(lm, axis=-1, keepdims=True))
    aff = ex / jnp.sum(ex, axis=-1, keepdims=True)
    row = i * tr + lax.broadcasted_iota(jnp.int32, (tr, 1), 0)
    aff = jnp.where((row < seq_len) & (lane < N_EXPERTS), aff, -1.0)
    nx = d_model // LANES
    for c in range(nx):
        xg_ref[:, c, :] = xn[:, LANES * c:LANES * (c + 1)]
    xg_ref[:, nx, :] = aff
    xg_ref[:, nx + 1:, :] = jnp.zeros((tr, ROW_PAD - 1, LANES), F32)
    afft_ref[...] = jnp.transpose(aff)[:N_EXPERTS, :]


def _router(h2, norm2_w, w_router, seq_len):
    lp, d = h2.shape
    tr = TOKEN_TILE
    wr = jnp.zeros((LANES, d), F32).at[:N_EXPERTS].set(jnp.swapaxes(w_router, 0, 1).astype(F32))
    nc = d // LANES + ROW_PAD
    return pl.pallas_call(
        functools.partial(_router_kernel, seq_len=seq_len, tr=tr, d_model=d),
        grid=(lp // tr,),
        in_specs=[pl.BlockSpec((tr, d), lambda i: (i, 0)),
                  pl.BlockSpec((1, d), lambda i: (0, 0)),
                  pl.BlockSpec((LANES, d), lambda i: (0, 0))],
        out_specs=[pl.BlockSpec((tr, nc, LANES), lambda i: (i, 0, 0)),
                   pl.BlockSpec((N_EXPERTS, tr), lambda i: (0, i))],
        out_shape=[jax.ShapeDtypeStruct((lp, nc, LANES), F32), jax.ShapeDtypeStruct((N_EXPERTS, lp), F32)],
        compiler_params=_params(1),
        name="router",
    )(h2, norm2_w.reshape(1, d).astype(F32), wr)


def _select_kernel(afft_ref, idx_ref, lo_ref, sel_ref, csum_ref, cols_ref, *, cap, slot_pad, lp, tile, n_tiles):
    aff = afft_ref[...]
    capf = float(cap)

    def count_ge(thr):
        return jnp.sum(jnp.where(aff >= thr, 1.0, 0.0), axis=1, keepdims=True)

    def bisect(_, carry):
        lo, hi = carry
        mid = 0.5 * (lo + hi)
        ge = count_ge(mid) >= capf
        return jnp.where(ge, mid, lo), jnp.where(ge, hi, mid)

    lo0 = jnp.zeros((N_EXPERTS, 1), F32)
    hi0 = jnp.full((N_EXPERTS, 1), 2.0, F32)
    _, hi = lax.fori_loop(0, 40, bisect, (lo0, hi0))

    def refine(st):
        hi, tau, done, _ = st
        cand = jnp.max(jnp.where(aff < hi, aff, -2.0), axis=1, keepdims=True)
        found = jnp.where(count_ge(cand) >= capf, 1.0, 0.0)
        tau = jnp.where(done > 0.0, tau, cand)
        hi = jnp.where(done + found > 0.0, hi, cand)
        done = jnp.maximum(done, found)
        return hi, tau, done, jnp.sum(1.0 - done)

    zero = jnp.zeros((N_EXPERTS, 1), F32)
    _, tau, _, _ = lax.while_loop(lambda st: st[3] > 0.0, refine, (hi, zero, zero, jnp.float32(N_EXPERTS)))
    need = capf - jnp.sum(jnp.where(aff > tau, 1.0, 0.0), axis=1, keepdims=True)

    ra = lax.broadcasted_iota(jnp.int32, (LANES, LANES), 0)
    ca = lax.broadcasted_iota(jnp.int32, (LANES, LANES), 1)
    ut = jnp.where(ra <= ca, 1.0, 0.0).astype(BF16)
    carry_eq = jnp.zeros((N_EXPERTS, 1), F32)
    carry_sel = jnp.zeros((N_EXPERTS, 1), F32)
    for j in range(lp // LANES):
        sl = slice(LANES * j, LANES * (j + 1))
        bj = aff[:, sl]
        eqf = jnp.where(bj == tau, 1.0, 0.0)
        tie_rank = _dot(eqf.astype(BF16), ut) + carry_eq - eqf
        carry_eq = carry_eq + jnp.sum(eqf, axis=1, keepdims=True)
        self = jnp.where((bj > tau) | ((bj == tau) & (tie_rank < need)), 1.0, 0.0)
        csum_ref[j] = _dot(self.astype(BF16), ut) + carry_sel
        carry_sel = carry_sel + jnp.sum(self, axis=1, keepdims=True)
        sel_ref[:, sl] = self

    lane = lax.broadcasted_iota(jnp.int32, (1, LANES), 1)
    sblk = 64
    cbase = lax.broadcasted_iota(jnp.int32, (sblk, 1), 0)
    cols_ref[...] = jnp.zeros_like(cols_ref)

    def per_expert(e, carry):
        def per_block(sb, carry2):
            cvals = (sb * sblk + cbase).astype(F32)

            def per_tile(j, acc):
                return acc + jnp.where(csum_ref[j, pl.ds(e, 1), :] <= cvals, 1.0, 0.0)

            n_lane_tiles = lp // LANES
            acc = lax.fori_loop(0, n_lane_tiles, per_tile, jnp.zeros((sblk, LANES), F32),
                                unroll=6 if n_lane_tiles % 6 == 0 else 1)
            rows = pl.ds(pl.multiple_of(sb * sblk, sblk), sblk)
            cols_ref[rows, :] = jnp.where(lane == e, jnp.sum(acc, axis=1, keepdims=True), cols_ref[rows, :])
            return carry2

        return lax.fori_loop(0, slot_pad // sblk, per_block, carry)

    lax.fori_loop(0, N_EXPERTS, per_expert, 0)
    idx_ref[...] = jnp.transpose(cols_ref[...])[:N_EXPERTS, :].astype(jnp.int32)

    sel = sel_ref[...]
    tok = lax.broadcasted_iota(jnp.int32, (1, lp), 1)
    table = jnp.zeros((N_EXPERTS, LANES), F32)
    for i in range(n_tiles + 1):
        below = jnp.sum(jnp.where(tok < N_META + tile * i, sel, 0.0), axis=1, keepdims=True)
        table = jnp.where(lane == i, below, table)
    lo_ref[...] = table.astype(jnp.int32)


def _select(afft, cap, slot_pad, tile, n_tiles):
    lp = afft.shape[1]
    assert n_tiles < LANES
    return pl.pallas_call(
        functools.partial(_select_kernel, cap=cap, slot_pad=slot_pad, lp=lp, tile=tile, n_tiles=n_tiles),
        out_shape=[jax.ShapeDtypeStruct((N_EXPERTS, slot_pad), jnp.int32),
                   jax.ShapeDtypeStruct((N_EXPERTS, LANES), jnp.int32)],
        scratch_shapes=[pltpu.VMEM((N_EXPERTS, lp), F32), pltpu.VMEM((lp // LANES, N_EXPERTS, LANES), F32),
                        pltpu.VMEM((slot_pad, LANES), F32)],
        compiler_params=pltpu.CompilerParams(vmem_limit_bytes=VMEM_LIMIT),
        name="expert_select",
    )(afft)


def _slab_copy(src, s_tok, dst, d_tok, sem, *, nc):
    s0 = pl.multiple_of(s_tok * nc, 8)
    d0 = pl.multiple_of(d_tok * nc, 8)
    return pltpu.make_async_copy(src.at[pl.ds(s0, nc)], dst.at[pl.ds(d0, nc)], sem)


def _dispatch_kernel(idx_ref, xg_ref, x_ref, aff_ref, slab_ref, sem, *, cap, cap_pad, nc, nx):
    copy = functools.partial(_slab_copy, nc=nc)
    e = pl.program_id(0)
    h0 = -(-(cap_pad // 2) // 16) * 16
    halves = ((0, h0, min(cap, h0)), (h0, cap_pad - h0, max(cap - h0, 0)))

    def fetch(ex, hf):
        first, _, used = halves[hf]

        def issue(s, c):
            copy(xg_ref, idx_ref[ex, first + s], slab_ref.at[hf], s, sem.at[hf]).start()
            return c

        lax.fori_loop(0, used, issue, 0, unroll=DMA_ISSUE_UNROLL if used % DMA_ISSUE_UNROLL == 0 else 1)

    def finish(hf):
        first, size, used = halves[hf]

        def drain(_, c):
            copy(xg_ref, 0, slab_ref.at[hf], 0, sem.at[hf]).wait()
            return c

        lax.fori_loop(0, used, drain, 0)
        rows = slice(first, first + size)
        for c in range(nx):
            x_ref[rows, LANES * c:LANES * (c + 1)] = slab_ref[hf, pl.ds(c, size, stride=nc), :].astype(BF16)
        aff_ref[rows, :] = slab_ref[hf, pl.ds(nx, size, stride=nc), :]

    @pl.when(e == 0)
    def _():
        for hf, (_, size, used) in enumerate(halves):
            if used < size:
                slab_ref[hf, used * nc:size * nc, :] = jnp.zeros(((size - used) * nc, LANES), F32)
        fetch(0, 0)

    fetch(e, 1)
    finish(0)

    @pl.when(e + 1 < N_EXPERTS)
    def _():
        fetch(e + 1, 0)

    finish(1)


def _dispatch(xg, idx, cap, cap_pad, d_model):
    nx = d_model // LANES
    nc = nx + ROW_PAD
    h0 = -(-(cap_pad // 2) // 16) * 16
    return pl.pallas_call(
        functools.partial(_dispatch_kernel, cap=cap, cap_pad=cap_pad, nc=nc, nx=nx),
        grid_spec=pltpu.PrefetchScalarGridSpec(
            num_scalar_prefetch=1,
            grid=(N_EXPERTS,),
            in_specs=[pl.BlockSpec(memory_space=pl.ANY)],
            out_specs=[pl.BlockSpec((cap_pad, d_model), lambda e, *_: (e, 0)),
                       pl.BlockSpec((cap_pad, LANES), lambda e, *_: (e, 0))],
            scratch_shapes=[pltpu.VMEM((2, h0 * nc, LANES), F32), pltpu.SemaphoreType.DMA((2,))]),
        out_shape=[jax.ShapeDtypeStruct((N_EXPERTS * cap_pad, d_model), BF16),
                   jax.ShapeDtypeStruct((N_EXPERTS * cap_pad, LANES), F32)],
        compiler_params=_params(1),
        name="expert_dispatch",
    )(idx, xg)


def _ffn_a_kernel(x_ref, wg_ref, wu_ref, o_ref):
    x16 = x_ref[...]
    g = _dot(x16, wg_ref[0].astype(BF16))
    u = _dot(x16, wu_ref[0].astype(BF16))
    o_ref[...] = (g * _sigmoid(g) * u).astype(o_ref.dtype)


def _ffn_a(xe, w_gate, w_up, cap_pad, d_model):
    _, _, ff = w_gate.shape
    tf = _pick(ff, 256, LANES)
    return pl.pallas_call(
        _ffn_a_kernel,
        grid=(N_EXPERTS, ff // tf),
        in_specs=[pl.BlockSpec((cap_pad, d_model), lambda e, f: (e, 0)),
                  pl.BlockSpec((1, d_model, tf), lambda e, f: (e, 0, f)),
                  pl.BlockSpec((1, d_model, tf), lambda e, f: (e, 0, f))],
        out_specs=pl.BlockSpec((cap_pad, tf), lambda e, f: (e, f)),
        out_shape=jax.ShapeDtypeStruct((N_EXPERTS * cap_pad, ff), BF16),
        compiler_params=_params(2),
        name="expert_ffn_in",
    )(xe, w_gate, w_up)


def _ffn_b_kernel(h_ref, wd_ref, aff_ref, o_ref):
    e = pl.program_id(0)
    y = _dot(h_ref[...], wd_ref[0].astype(BF16))
    lane = lax.broadcasted_iota(jnp.int32, (1, LANES), 1)
    gate = jnp.sum(jnp.where(lane == e, aff_ref[...], 0.0), axis=-1, keepdims=True)
    y = y * gate
    for c in range(y.shape[1] // LANES):
        o_ref[:, c, :] = y[:, LANES * c:LANES * (c + 1)]


def _ffn_b(hid, w_down, aff, cap_pad, d_model):
    _, ff, _ = w_down.shape
    td = _pick(d_model, 1024, 8 * LANES) if d_model % (8 * LANES) == 0 else d_model
    return pl.pallas_call(
        _ffn_b_kernel,
        grid=(N_EXPERTS, d_model // td),
        in_specs=[pl.BlockSpec((cap_pad, ff), lambda e, n: (e, 0)),
                  pl.BlockSpec((1, ff, td), lambda e, n: (e, 0, n)),
                  pl.BlockSpec((cap_pad, LANES), lambda e, n: (e, 0))],
        out_specs=pl.BlockSpec((cap_pad, td // LANES, LANES), lambda e, n: (e, n, 0)),
        out_shape=jax.ShapeDtypeStruct((N_EXPERTS * cap_pad, d_model // LANES, LANES), F32),
        compiler_params=_params(2),
        name="expert_ffn_out",
    )(hid, w_down, aff)


def _combine_kernel(idx_ref, lo_ref, h_ref, ye_ref, nw_ref, o_ref, hbuf_ref, acc_ref, stage_ref, sem, hsem, *,
                    tt, nx, cap_pad):
    i = pl.program_id(0)
    t0 = N_META + i * tt
    ch = COMBINE_CHUNK
    reg = COMBINE_REGION
    pitch = nx + ROW_PAD
    slot = i % 2

    def residual_copy(step, buf):
        rows = pl.ds(pl.multiple_of(N_META + step * tt, 8), tt)
        return pltpu.make_async_copy(h_ref.at[rows], hbuf_ref.at[buf], hsem.at[buf])

    @pl.when(i == 0)
    def _():
        residual_copy(0, 0).start()

    @pl.when(i + 1 < pl.num_programs(0))
    def _():
        residual_copy(i + 1, 1 - slot).start()

    def round_chunks(e, r):
        left = lo_ref[e, i + 1] - lo_ref[e, i] - r * reg
        return (jnp.minimum(left, reg) + ch - 1) // ch

    def chunk_copy(e, r, j):
        src = pl.multiple_of((e * cap_pad + lo_ref[e, i] + r * reg + j * ch) * nx, 8)
        dst = pl.multiple_of(j * ch * nx, 8)
        return pltpu.make_async_copy(ye_ref.at[pl.ds(src, ch * nx)], stage_ref.at[e, pl.ds(dst, ch * nx)],
                                     sem.at[e])

    def fetch(e, r):
        def body(j, c):
            chunk_copy(e, r, j).start()
            return c

        lax.fori_loop(0, round_chunks(e, r), body, 0)

    def add_round(e, r):
        def drain(j, c):
            chunk_copy(e, r, j).wait()
            return c

        lax.fori_loop(0, round_chunks(e, r), drain, 0)
        first = lo_ref[e, i] + r * reg

        def per_slot(s, c):
            src = pl.ds(pl.multiple_of((s - first) * nx, 8), nx)
            dst = pl.ds(pl.multiple_of((idx_ref[e, s] - t0) * pitch, 8), nx)
            acc_ref[dst, :] = acc_ref[dst, :] + stage_ref[e, src, :]
            return c

        lax.fori_loop(first, jnp.minimum(first + reg, lo_ref[e, i + 1]), per_slot, 0)

    def prefetch(e, c):
        fetch(e, 0)
        return c

    lax.fori_loop(0, N_EXPERTS, prefetch, 0)
    residual_copy(i, slot).wait()
    for c in range(nx):
        acc_ref[pl.ds(c, tt, stride=pitch), :] = hbuf_ref[slot, :, LANES * c:LANES * (c + 1)]

    def per_expert(e, carry):
        add_round(e, 0)

        def later(r, c):
            fetch(e, r)
            add_round(e, r)
            return c

        n_rounds = (lo_ref[e, i + 1] - lo_ref[e, i] + reg - 1) // reg
        lax.fori_loop(1, n_rounds, later, 0)
        return carry

    lax.fori_loop(0, N_EXPERTS, per_expert, 0)
    for c in range(nx):
        hbuf_ref[slot, :, LANES * c:LANES * (c + 1)] = acc_ref[pl.ds(c, tt, stride=pitch), :]
    x = hbuf_ref[slot]
    o_ref[...] = x * lax.rsqrt(jnp.mean(x * x, axis=-1, keepdims=True) + NORM_EPS) * nw_ref[...]


def _combine(idx, lo, h2, ye, norm_f_w, n_real, tt, cap_pad):
    d = h2.shape[1]
    nx = d // LANES
    assert COMBINE_REGION % COMBINE_CHUNK == 0
    return pl.pallas_call(
        functools.partial(_combine_kernel, tt=tt, nx=nx, cap_pad=cap_pad),
        grid_spec=pltpu.PrefetchScalarGridSpec(
            num_scalar_prefetch=2,
            grid=(n_real // tt,),
            in_specs=[pl.BlockSpec(memory_space=pl.ANY),
                      pl.BlockSpec(memory_space=pl.ANY),
                      pl.BlockSpec((1, d), lambda i, *_: (0, 0))],
            out_specs=pl.BlockSpec((tt, d), lambda i, *_: (i, 0)),
            scratch_shapes=[pltpu.VMEM((2, tt, d), F32), pltpu.VMEM((tt * (nx + ROW_PAD), LANES), F32),
                            pltpu.VMEM((N_EXPERTS, COMBINE_REGION * nx, LANES), F32),
                            pltpu.SemaphoreType.DMA((N_EXPERTS,)), pltpu.SemaphoreType.DMA((2,))]),
        out_shape=jax.ShapeDtypeStruct((n_real, d), F32),
        compiler_params=_params(1),
        name="expert_combine",
    )(idx, lo, h2, ye, norm_f_w.reshape(1, d).astype(F32))


def _layer(hp, seq_len, norm1_w, w_in, conv_w, a_log_fwd, a_log_bwd, dt_bias_fwd, dt_bias_bwd, out_norm_w,
           w_branch_a, attn_sink, w_branch_b, w_out, norm2_w, w_router, w_gate, w_up, w_down):
    lp, d = hp.shape
    main_w = 2 * DN_QK + 2 * DN_VW
    gate_w = 4 * DN_HEADS
    rest_w = SWA_QW + 2 * SWA_KVW + 2 * d
    n = _rmsnorm(hp, norm1_w, BF16)
    w_t = jnp.swapaxes(w_in, 0, 1)
    proj_main = _matmul(n, w_t, main_w, 0, BF16, w_rows=True, name="in_proj_main")
    gates = _matmul(n, w_t, LANES, main_w, F32, w_rows=True, name="in_proj_gates")
    rest = _matmul(n, w_t, rest_w, main_w + gate_w, BF16, w_rows=True, name="in_proj_rest")

    prm = jnp.zeros((8, LANES), F32)
    prm = prm.at[0, 2 * DN_HEADS:4 * DN_HEADS].set(jnp.concatenate([a_log_fwd, a_log_bwd]).astype(F32))
    prm = prm.at[1, 2 * DN_HEADS:4 * DN_HEADS].set(jnp.concatenate([dt_bias_fwd, dt_bias_bwd]).astype(F32))
    qkv = _dn_prep(proj_main, conv_w.astype(F32), seq_len)
    o_f = _deltanet(qkv, gates, prm, seq_len, backward=False)
    o_r = _deltanet(qkv, gates, prm, seq_len, backward=True)
    o_a = _gated_norm(o_f, o_r, proj_main, out_norm_w)
    o_b = _window_attention(rest, attn_sink, seq_len)

    mixed = _branch_merge(o_a, o_b, w_branch_a, w_branch_b, rest, d)
    h2 = _matmul(mixed, w_out, d, 0, F32, res=hp, name="out_proj")

    cap = EC_CAPACITY * seq_len // N_EXPERTS
    cap_pad = -(-(cap + COMBINE_CHUNK) // 16) * 16
    xg, afft = _router(h2, norm2_w, w_router, seq_len)
    n_real = seq_len - N_META
    tt = _pick(n_real, TOKEN_TILE, LANES)
    slot_pad = -(-cap_pad // LANES) * LANES
    idx, lo = _select(afft, cap, slot_pad, tt, n_real // tt)
    xe, aff = _dispatch(xg.reshape(-1, LANES), idx, cap, cap_pad, d)
    hid = _ffn_a(xe, w_gate, w_up, cap_pad, d)
    ye = _ffn_b(hid, w_down, aff, cap_pad, d)
    return functools.partial(_combine, idx, lo, h2, ye.reshape(-1, LANES), n_real=n_real, tt=tt, cap_pad=cap_pad)


def kernel(x, meta_tokens, norm1_w, w_in, conv_w, a_log_fwd, a_log_bwd, dt_bias_fwd, dt_bias_bwd, out_norm_w,
           w_branch_a, attn_sink, w_branch_b, w_out, norm2_w, w_router, w_gate, w_up, w_down, norm_f_w):
    batch, seq, d = x.shape
    depth = norm1_w.shape[0]
    assert depth == 1, "the final norm is fused into the last layer's expert combine"
    seq_len = N_META + seq
    lp = -(-seq_len // TOKEN_TILE) * TOKEN_TILE
    outs = []
    for b in range(batch):
        h = jnp.concatenate([meta_tokens.astype(x.dtype), x[b]], axis=0)
        hp = jnp.pad(h, ((0, lp - seq_len), (0, 0)))
        combine = _layer(hp, seq_len, norm1_w[0], w_in[0], conv_w[0], a_log_fwd[0], a_log_bwd[0],
                         dt_bias_fwd[0], dt_bias_bwd[0], out_norm_w[0], w_branch_a[0], attn_sink[0],
                         w_branch_b[0], w_out[0], norm2_w[0], w_router[0], w_gate[0], w_up[0], w_down[0])
        outs.append(combine(norm_f_w=norm_f_w))
    return jnp.stack(outs, axis=0)
```

```python
import functools

import jax
import jax.numpy as jnp
from jax import lax
from jax.experimental import pallas as pl
from jax.experimental.pallas import tpu as pltpu

F32 = jnp.float32
BF16 = jnp.bfloat16

N_META = 16
NORM_EPS = 1e-6
DN_HEADS = 16
DN_DK = 128
DN_DV = 128
SWA_HQ = 16
SWA_HKV = 4
SWA_D = 128
SWA_WINDOW = 128
N_EXPERTS = 16
EC_CAPACITY = 2

LANES = 128
TOKEN_TILE = 256
DN_CHUNK = 128
DN_HEADS_PER_STEP = 8
ATTN_TILE = 128
ATTN_ROWS = 64
ROW_PAD = 8
DMA_ISSUE_UNROLL = 6
COMBINE_CHUNK = 8
COMBINE_REGION = 64
VMEM_LIMIT = 56 * 1024 * 1024

DN_QK = DN_HEADS * DN_DK
DN_VW = DN_HEADS * DN_DV
SWA_QW = SWA_HQ * SWA_D
SWA_KVW = SWA_HKV * SWA_D


def _params(n_grid):
    return pltpu.CompilerParams(dimension_semantics=("arbitrary",) * n_grid, vmem_limit_bytes=VMEM_LIMIT)


def _pick(n, target, mult):
    best = None
    for t in range(mult, min(n, target) + 1, mult):
        if n % t == 0:
            best = t
    assert best is not None, (n, target, mult)
    return best


def _sigmoid(x):
    return 1.0 / (1.0 + jnp.exp(-x))


def _dot(a, b):
    return jnp.dot(a, b, preferred_element_type=F32)


def _dot_nt(a, b):
    return lax.dot_general(a, b, (((1,), (1,)), ((), ())), preferred_element_type=F32)


def _rms_kernel(x_ref, w_ref, o_ref):
    x = x_ref[...]
    ms = jnp.mean(x * x, axis=-1, keepdims=True)
    o_ref[...] = (x * lax.rsqrt(ms + NORM_EPS) * w_ref[...]).astype(o_ref.dtype)


def _rmsnorm(x, w, out_dtype):
    m, d = x.shape
    tr = _pick(m, 256, 16)
    return pl.pallas_call(
        _rms_kernel,
        grid=(m // tr,),
        in_specs=[pl.BlockSpec((tr, d), lambda i: (i, 0)), pl.BlockSpec((1, d), lambda i: (0, 0))],
        out_specs=pl.BlockSpec((tr, d), lambda i: (i, 0)),
        out_shape=jax.ShapeDtypeStruct((m, d), out_dtype),
        compiler_params=_params(1),
        name="rmsnorm",
    )(x, w.reshape(1, d).astype(F32))


def _mm_kernel(*refs, has_res, w_rows):
    if has_res:
        a_ref, w_ref, r_ref, o_ref, wb_ref = refs
    else:
        a_ref, w_ref, o_ref, wb_ref = refs

    @pl.when(pl.program_id(1) == 0)
    def _():
        w = w_ref[...]
        wb_ref[...] = (jnp.transpose(w) if w_rows else w).astype(BF16)

    acc = _dot(a_ref[...], wb_ref[...])
    if has_res:
        acc = acc + r_ref[...]
    o_ref[...] = acc.astype(o_ref.dtype)


def _matmul(a, w, n_cols, off, out_dtype, res=None, w_rows=False, name="matmul"):
    m, k = a.shape
    tn = _pick(n_cols, 512, LANES)
    tm = _pick(m, 1408 if res is None else 1056, 16)
    if w_rows:
        if off % tn == 0:
            w_spec = pl.BlockSpec((tn, k), lambda n, i: (off // tn + n, 0))
        else:
            assert off % 8 == 0
            w_spec = pl.BlockSpec((pl.Element(tn), pl.Element(k)), lambda n, i: (pl.multiple_of(off + n * tn, 8), 0))
    else:
        assert off % tn == 0
        w_spec = pl.BlockSpec((k, tn), lambda n, i: (0, off // tn + n))
    in_specs = [pl.BlockSpec((tm, k), lambda n, i: (i, 0)), w_spec]
    args = [a, w]
    if res is not None:
        in_specs.append(pl.BlockSpec((tm, tn), lambda n, i: (i, n)))
        args.append(res)
    return pl.pallas_call(
        functools.partial(_mm_kernel, has_res=res is not None, w_rows=w_rows),
        grid=(n_cols // tn, m // tm),
        in_specs=in_specs,
        out_specs=pl.BlockSpec((tm, tn), lambda n, i: (i, n)),
        out_shape=jax.ShapeDtypeStruct((m, n_cols), out_dtype),
        scratch_shapes=[pltpu.VMEM((k, tn), BF16)],
        compiler_params=_params(2),
        name=name,
    )(*args)


def _dn_prep_kernel(x_ref, p_ref, n_ref, c_ref, o_ref, *, seq_len, tl, nh, q_blocks, k_blocks):
    i = pl.program_id(0)
    j = pl.program_id(1)
    nt = pl.num_programs(0)
    row = lax.broadcasted_iota(jnp.int32, (tl, 1), 0)
    validf = ((i * tl + row) < seq_len).astype(F32)
    x = x_ref[...].astype(F32)
    prev = p_ref[...].astype(F32)[15:16, :] * (i > 0).astype(F32)
    nxt = n_ref[...].astype(F32)[0:1, :] * (i < nt - 1).astype(F32)
    xm1 = jnp.where(row == 0, prev, pltpu.roll(x, 1, 0))
    xp1 = jnp.where(row == tl - 1, nxt, pltpu.roll(x, tl - 1, 0))
    c = c_ref[...]
    y = xm1 * c[0:1] + x * c[1:2] + xp1 * c[2:3]
    y = y * _sigmoid(y)
    is_qk = (j < q_blocks + k_blocks).astype(F32)
    q_scale = jnp.where(j < q_blocks, DN_DK ** -0.5, 1.0)
    for h in range(nh):
        hs = slice(DN_DK * h, DN_DK * (h + 1))
        yh = y[:, hs]
        norm = lax.rsqrt(jnp.sum(yh * yh, axis=-1, keepdims=True) + NORM_EPS) * q_scale
        o_ref[:, hs] = (yh * (is_qk * norm + (1.0 - is_qk)) * validf).astype(o_ref.dtype)


def _dn_prep(proj_main, conv_w, seq_len):
    lp = proj_main.shape[0]
    tl = _pick(lp, 3 * TOKEN_TILE, TOKEN_TILE)
    nh = 8
    wb = nh * DN_DK
    width = 2 * DN_QK + DN_VW
    hb = tl // 16
    nhb = lp // 16
    return pl.pallas_call(
        functools.partial(_dn_prep_kernel, seq_len=seq_len, tl=tl, nh=nh, q_blocks=DN_QK // wb, k_blocks=DN_QK // wb),
        grid=(lp // tl, width // wb),
        in_specs=[pl.BlockSpec((tl, wb), lambda i, j: (i, j)),
                  pl.BlockSpec((16, wb), lambda i, j: (jnp.maximum(i * hb - 1, 0), j)),
                  pl.BlockSpec((16, wb), lambda i, j: (jnp.minimum((i + 1) * hb, nhb - 1), j)),
                  pl.BlockSpec((3, wb), lambda i, j: (0, j))],
        out_specs=pl.BlockSpec((tl, wb), lambda i, j: (i, j)),
        out_shape=jax.ShapeDtypeStruct((lp, width), BF16),
        compiler_params=_params(2),
        name="deltanet_prep",
    )(proj_main, proj_main, proj_main, conv_w)


def _dn_kernel(q_ref, k_ref, v_ref, sm_ref, prm_ref, o_ref, s_ref, *, seq_len, tl, backward):
    hg = pl.program_id(0)
    i = pl.program_id(1)
    nt = pl.num_programs(1)
    ti = nt - 1 - i if backward else i
    cs = DN_CHUNK
    cs_shift = cs.bit_length() - 1
    hpb = DN_HEADS_PER_STEP

    @pl.when(i == 0)
    def _():
        s_ref[...] = jnp.zeros_like(s_ref)

    row = lax.broadcasted_iota(jnp.int32, (tl, 1), 0)
    validf = ((ti * tl + row) < seq_len).astype(F32)

    lane = lax.broadcasted_iota(jnp.int32, (1, LANES), 1)
    sm = sm_ref[...]
    prm = prm_ref[...]
    z = sm + prm[1:2]
    softplus = jnp.maximum(z, 0.0) + jnp.log1p(jnp.exp(-jnp.abs(z)))
    beta_all = _sigmoid(sm) * validf
    g_all = -jnp.exp(prm[0:1]) * softplus * validf

    r2 = lax.broadcasted_iota(jnp.int32, (tl, tl), 0)
    c2 = lax.broadcasted_iota(jnp.int32, (tl, tl), 1)
    incl2 = ((r2 >> cs_shift) == (c2 >> cs_shift)) & ((r2 <= c2) if backward else (r2 >= c2))
    tri = jnp.where(incl2, 1.0, 0.0).astype(BF16)
    g1 = g_all.astype(BF16)
    rem = g_all - g1.astype(F32)
    g2 = rem.astype(BF16)
    g3 = (rem - g2.astype(F32)).astype(BF16)
    gc_all = _dot(tri, g1) + _dot(tri, g2) + _dot(tri, g3)

    r = lax.broadcasted_iota(jnp.int32, (cs, cs), 0)
    c = lax.broadcasted_iota(jnp.int32, (cs, cs), 1)
    incl = (r <= c) if backward else (r >= c)
    strict = (r < c) if backward else (r > c)
    levels = []
    b = 1
    while b < cs:
        sh = b.bit_length() - 1
        levels.append(((r >> (sh + 1)) == (c >> (sh + 1))) & ((r >> sh) != (c >> sh)))
        b *= 2

    chunk_order = range(tl // cs - 1, -1, -1) if backward else range(tl // cs)
    streams = []
    for j in range(hpb):
        hs = slice(DN_DK * j, DN_DK * (j + 1))
        q = q_ref[:, hs].astype(F32)
        k = k_ref[:, hs].astype(F32)
        v = v_ref[:, hs].astype(F32)
        col = (DN_HEADS if backward else 0) + hg * hpb + j
        beta = jnp.sum(jnp.where(lane == col, beta_all, 0.0), axis=-1, keepdims=True)
        g = jnp.sum(jnp.where(lane == 2 * DN_HEADS + col, g_all, 0.0), axis=-1, keepdims=True)
        gcum = jnp.sum(jnp.where(lane == 2 * DN_HEADS + col, gc_all, 0.0), axis=-1, keepdims=True)
        for ci in chunk_order:
            rs = slice(ci * cs, (ci + 1) * cs)
            qc, kc, vc, bc = q[rs], k[rs], v[rs], beta[rs]
            gtot = jnp.sum(g[rs], axis=0, keepdims=True)
            gb = jnp.broadcast_to(gcum[rs], (cs, LANES))
            gj = jnp.transpose(gb)[0:1, :]
            decay = jnp.where(incl, jnp.exp(jnp.where(incl, gb[:, 0:1] - gj, 0.0)), 0.0)
            kb = kc * bc
            k16 = kc.astype(BF16)
            egc = jnp.exp(gb)
            streams.append(dict(
                j=j, rs=rs, hs=hs, gtot=gtot,
                m=jnp.where(strict, _dot_nt(kb.astype(BF16), k16) * decay, 0.0),
                qk=(_dot_nt(qc.astype(BF16), k16) * decay).astype(BF16),
                rhs=jnp.concatenate([vc * bc, kb * egc], axis=1),
                q_dec=qc * egc,
                k_dec_t=jnp.transpose(kc * jnp.exp(gtot - gb)).astype(BF16)))

    for st in streams:
        st["e"] = -jnp.where(levels[0], st["m"], 0.0)
    for mask in levels[1:]:
        for st in streams:
            lb = jnp.where(mask, st["m"], 0.0)
            st["lb"] = lb
            st["y"] = lb + _dot(lb.astype(BF16), st["e"].astype(BF16))
        for st in streams:
            st["e"] = st["e"] - st["y"] - _dot(st["e"].astype(BF16), st["y"].astype(BF16))
    for st in streams:
        st["uw"] = st["rhs"] + _dot(st["e"].astype(BF16), st["rhs"].astype(BF16))

    state = [s_ref[j] for j in range(hpb)]
    for step in range(tl // cs):
        cur = [streams[j * (tl // cs) + step] for j in range(hpb)]
        for st in cur:
            wq = jnp.concatenate([st["uw"][:, DN_DV:], st["q_dec"]], axis=0).astype(BF16)
            st["wqs"] = _dot(wq, state[st["j"]].astype(BF16))
        for st in cur:
            v16 = (st["uw"][:, :DN_DV] - st["wqs"][:cs]).astype(BF16)
            o_ref[st["rs"], st["hs"]] = (st["wqs"][cs:] + _dot(st["qk"], v16)).astype(o_ref.dtype)
            state[st["j"]] = state[st["j"]] * jnp.exp(st["gtot"]) + _dot(st["k_dec_t"], v16)
    for j in range(hpb):
        s_ref[j] = state[j]


def _deltanet(qkv, gates, prm, seq_len, backward):
    lp = qkv.shape[0]
    tl = TOKEN_TILE
    nt = lp // tl
    hpb = DN_HEADS_PER_STEP
    wb = hpb * DN_DK
    ngroups = DN_HEADS // hpb

    def tile(i):
        return nt - 1 - i if backward else i

    def main_spec(off):
        return pl.BlockSpec((tl, wb), lambda h, i: (tile(i), off + h))

    in_specs = [main_spec(0), main_spec(ngroups), main_spec(2 * ngroups),
                pl.BlockSpec((tl, LANES), lambda h, i: (tile(i), 0)),
                pl.BlockSpec((8, LANES), lambda h, i: (0, 0))]
    args = [qkv, qkv, qkv, gates, prm]
    return pl.pallas_call(
        functools.partial(_dn_kernel, seq_len=seq_len, tl=tl, backward=backward),
        grid=(ngroups, nt),
        in_specs=in_specs,
        out_specs=pl.BlockSpec((tl, wb), lambda h, i: (tile(i), h)),
        out_shape=jax.ShapeDtypeStruct((lp, DN_VW), BF16),
        scratch_shapes=[pltpu.VMEM((hpb, DN_DK, DN_DV), F32)],
        compiler_params=_params(2),
        name="deltanet_bwd" if backward else "deltanet_fwd",
    )(*args)


def _gnorm_kernel(of_ref, ob_ref, z_ref, w_ref, o_ref, *, nh):
    for j in range(nh):
        sl = slice(DN_DV * j, DN_DV * (j + 1))
        o = of_ref[:, sl].astype(F32) + ob_ref[:, sl].astype(F32)
        o = o * lax.rsqrt(jnp.mean(o * o, axis=-1, keepdims=True) + NORM_EPS) * w_ref[...]
        z = z_ref[:, sl].astype(F32)
        o_ref[:, sl] = (o * (z * _sigmoid(z))).astype(o_ref.dtype)


def _gated_norm(o_f, o_b, proj_main, out_norm_w):
    lp = proj_main.shape[0]
    nh = 4
    wb = nh * DN_DV
    tr = _pick(lp, 768, 16)
    zoff = (2 * DN_QK + DN_VW) // wb
    return pl.pallas_call(
        functools.partial(_gnorm_kernel, nh=nh),
        grid=(lp // tr, DN_VW // wb),
        in_specs=[pl.BlockSpec((tr, wb), lambda i, j: (i, j)),
                  pl.BlockSpec((tr, wb), lambda i, j: (i, j)),
                  pl.BlockSpec((tr, wb), lambda i, j: (i, zoff + j)),
                  pl.BlockSpec((1, DN_DV), lambda i, j: (0, 0))],
        out_specs=pl.BlockSpec((tr, wb), lambda i, j: (i, j)),
        out_shape=jax.ShapeDtypeStruct((lp, DN_VW), BF16),
        compiler_params=_params(2),
        name="gated_norm",
    )(o_f, o_b, proj_main, out_norm_w.reshape(1, DN_DV).astype(F32))


def _attn_kernel(q_ref, k_ref, v_ref, prm_ref, o_ref, *, seq_len, tq, lp):
    groups = SWA_HQ // SWA_HKV
    heads = range(groups)
    prm = prm_ref[0]
    slopes = [prm[g:g + 1, 0:1] for g in heads]
    sinks = [prm[groups + g:groups + g + 1, 0:1] for g in heads]
    scale = SWA_D ** -0.5
    k_meta = k_ref[0:tq, :]
    v_meta = v_ref[0:tq, :]
    tr = ATTN_ROWS
    rq = lax.broadcasted_iota(jnp.int32, (tr, 1), 0)
    ck = lax.broadcasted_iota(jnp.int32, (1, 4 * tq), 1)
    meta_part = ck < tq
    win_part = jnp.logical_not(meta_part)

    def tile_bias(q0, k0):
        pq = q0 + rq
        pk = k0 + (ck - tq)
        real_q = pq >= N_META
        dist = pq - pk
        adist = jnp.abs(dist)
        in_win = (real_q & (adist <= SWA_WINDOW)) | (jnp.logical_not(real_q) & (-dist <= SWA_WINDOW))
        win_ok = win_part & (pk >= N_META) & (pk < seq_len) & in_win
        ok = (meta_part & (ck < N_META)) | win_ok
        return jnp.where(ok, jnp.where(real_q & win_part, -adist.astype(F32), 0.0), -jnp.inf)

    def tile(i, carry):
        q0 = pl.multiple_of(i * tr, tr)
        kb = (i * tr) // tq * tq
        k0 = pl.multiple_of(jnp.clip(kb - tq, 0, lp - 3 * tq), tq)
        kall = jnp.concatenate([k_meta, k_ref[pl.ds(k0, 3 * tq), :]], axis=0)
        vall = jnp.concatenate([v_meta, v_ref[pl.ds(k0, 3 * tq), :]], axis=0)
        bias = tile_bias(q0, k0)
        s = [_dot_nt(q_ref[pl.ds(q0, tr), SWA_D * g:SWA_D * (g + 1)], kall) * scale + slopes[g] * bias
             for g in heads]
        m = [jnp.maximum(jnp.max(s[g], axis=-1, keepdims=True), sinks[g]) for g in heads]
        p = [jnp.exp(s[g] - m[g]) for g in heads]
        den = [jnp.sum(p[g], axis=-1, keepdims=True) + jnp.exp(sinks[g] - m[g]) for g in heads]
        o = [_dot(p[g].astype(BF16), vall) for g in heads]
        for g in heads:
            o_ref[pl.ds(q0, tr), SWA_D * g:SWA_D * (g + 1)] = (o[g] / den[g]).astype(o_ref.dtype)
        return carry

    nq = lp // tr
    lax.fori_loop(0, nq, tile, 0, unroll=2 if nq % 2 == 0 else 1)


def _window_attention(rest, attn_sink, seq_len):
    lp = rest.shape[0]
    tq = ATTN_TILE
    assert lp >= 3 * tq
    groups = SWA_HQ // SWA_HKV
    koff = SWA_QW // SWA_D
    voff = koff + SWA_HKV
    slopes = 2.0 ** (-8.0 * jnp.arange(1, SWA_HQ + 1, dtype=F32) / SWA_HQ)
    prm = jnp.concatenate([slopes.reshape(SWA_HKV, groups), attn_sink.astype(F32).reshape(SWA_HKV, groups)], axis=1)
    prm = jnp.broadcast_to(prm[:, :, None], (SWA_HKV, 2 * groups, LANES))
    qw = groups * SWA_D
    return pl.pallas_call(
        functools.partial(_attn_kernel, seq_len=seq_len, tq=tq, lp=lp),
        grid=(SWA_HKV,),
        in_specs=[pl.BlockSpec((lp, qw), lambda h: (0, h)),
                  pl.BlockSpec((lp, SWA_D), lambda h: (0, koff + h)),
                  pl.BlockSpec((lp, SWA_D), lambda h: (0, voff + h)),
                  pl.BlockSpec((1, 2 * groups, LANES), lambda h: (h, 0, 0))],
        out_specs=pl.BlockSpec((lp, qw), lambda h: (0, h)),
        out_shape=jax.ShapeDtypeStruct((lp, SWA_QW), BF16),
        compiler_params=_params(1),
        name="window_attention",
    )(rest, rest, rest, prm)


def _branch_kernel(oa_ref, ob_ref, wa_ref, wb_ref, ga_ref, gb_ref, o_ref, wa16_ref, wb16_ref):
    @pl.when(pl.program_id(1) == 0)
    def _():
        wa16_ref[...] = wa_ref[...].astype(BF16)
        wb16_ref[...] = wb_ref[...].astype(BF16)

    ya = _dot(oa_ref[...], wa16_ref[...])
    yb = _dot(ob_ref[...], wb16_ref[...])
    o = _sigmoid(ga_ref[...].astype(F32)) * ya + _sigmoid(gb_ref[...].astype(F32)) * yb
    o_ref[...] = o.astype(o_ref.dtype)


def _branch_merge(o_a, o_b, w_a, w_b, rest, d_model):
    lp = o_a.shape[0]
    tn = _pick(d_model, 512, LANES)
    tm = _pick(lp, 1056, 16)
    ga_off = (SWA_QW + 2 * SWA_KVW) // tn
    gb_off = (SWA_QW + 2 * SWA_KVW + d_model) // tn
    assert (SWA_QW + 2 * SWA_KVW) % tn == 0 and d_model % tn == 0
    return pl.pallas_call(
        _branch_kernel,
        grid=(d_model // tn, lp // tm),
        in_specs=[pl.BlockSpec((tm, DN_VW), lambda n, i: (i, 0)),
                  pl.BlockSpec((tm, SWA_QW), lambda n, i: (i, 0)),
                  pl.BlockSpec((DN_VW, tn), lambda n, i: (0, n)),
                  pl.BlockSpec((SWA_QW, tn), lambda n, i: (0, n)),
                  pl.BlockSpec((tm, tn), lambda n, i: (i, ga_off + n)),
                  pl.BlockSpec((tm, tn), lambda n, i: (i, gb_off + n))],
        out_specs=pl.BlockSpec((tm, tn), lambda n, i: (i, n)),
        out_shape=jax.ShapeDtypeStruct((lp, d_model), BF16),
        scratch_shapes=[pltpu.VMEM((DN_VW, tn), BF16), pltpu.VMEM((SWA_QW, tn), BF16)],
        compiler_params=_params(2),
        name="branch_merge",
    )(o_a, o_b, w_a, w_b, rest, rest)


def _router_kernel(h_ref, w_ref, wr_ref, xg_ref, afft_ref, *, seq_len, tr, d_model):
    i = pl.program_id(0)
    x = h_ref[...]
    xn = x * lax.rsqrt(jnp.mean(x * x, axis=-1, keepdims=True) + NORM_EPS) * w_ref[...]
    logits = _dot_nt(xn.astype(BF16), wr_ref[...].astype(BF16))
    lane = lax.broadcasted_iota(jnp.int32, (1, LANES), 1)
    lm = jnp.where(lane < N_EXPERTS, logits, -jnp.inf)
    ex = jnp.exp(lm - jnp.max(lm, axis=-1, keepdims=True))
    aff = ex / jnp.sum(ex, axis=-1, keepdims=True)
    row = i * tr + lax.broadcasted_iota(jnp.int32, (tr, 1), 0)
    aff = jnp.where((row < seq_len) & (lane < N_EXPERTS), aff, -1.0)
    nx = d_model // LANES
    for c in range(nx):
        xg_ref[:, c, :] = xn[:, LANES * c:LANES * (c + 1)]
    xg_ref[:, nx, :] = aff
    xg_ref[:, nx + 1:, :] = jnp.zeros((tr, ROW_PAD - 1, LANES), F32)
    afft_ref[...] = jnp.transpose(aff)[:N_EXPERTS, :]


def _router(h2, norm2_w, w_router, seq_len):
    lp, d = h2.shape
    tr = TOKEN_TILE
    wr = jnp.zeros((LANES, d), F32).at[:N_EXPERTS].set(jnp.swapaxes(w_router, 0, 1).astype(F32))
    nc = d // LANES + ROW_PAD
    return pl.pallas_call(
        functools.partial(_router_kernel, seq_len=seq_len, tr=tr, d_model=d),
        grid=(lp // tr,),
        in_specs=[pl.BlockSpec((tr, d), lambda i: (i, 0)),
                  pl.BlockSpec((1, d), lambda i: (0, 0)),
                  pl.BlockSpec((LANES, d), lambda i: (0, 0))],
        out_specs=[pl.BlockSpec((tr, nc, LANES), lambda i: (i, 0, 0)),
                   pl.BlockSpec((N_EXPERTS, tr), lambda i: (0, i))],
        out_shape=[jax.ShapeDtypeStruct((lp, nc, LANES), F32), jax.ShapeDtypeStruct((N_EXPERTS, lp), F32)],
        compiler_params=_params(1),
        name="router",
    )(h2, norm2_w.reshape(1, d).astype(F32), wr)


def _select_kernel(afft_ref, idx_ref, lo_ref, sel_ref, csum_ref, cols_ref, *, cap, slot_pad, lp, tile, n_tiles):
    aff = afft_ref[...]
    capf = float(cap)

    def count_ge(thr):
        return jnp.sum(jnp.where(aff >= thr, 1.0, 0.0), axis=1, keepdims=True)

    def bisect(_, carry):
        lo, hi = carry
        mid = 0.5 * (lo + hi)
        ge = count_ge(mid) >= capf
        return jnp.where(ge, mid, lo), jnp.where(ge, hi, mid)

    lo0 = jnp.zeros((N_EXPERTS, 1), F32)
    hi0 = jnp.full((N_EXPERTS, 1), 2.0, F32)
    _, hi = lax.fori_loop(0, 40, bisect, (lo0, hi0))

    def refine(st):
        hi, tau, done, _ = st
        cand = jnp.max(jnp.where(aff < hi, aff, -2.0), axis=1, keepdims=True)
        found = jnp.where(count_ge(cand) >= capf, 1.0, 0.0)
        tau = jnp.where(done > 0.0, tau, cand)
        hi = jnp.where(done + found > 0.0, hi, cand)
        done = jnp.maximum(done, found)
        return hi, tau, done, jnp.sum(1.0 - done)

    zero = jnp.zeros((N_EXPERTS, 1), F32)
    _, tau, _, _ = lax.while_loop(lambda st: st[3] > 0.0, refine, (hi, zero, zero, jnp.float32(N_EXPERTS)))
    need = capf - jnp.sum(jnp.where(aff > tau, 1.0, 0.0), axis=1, keepdims=True)

    ra = lax.broadcasted_iota(jnp.int32, (LANES, LANES), 0)
    ca = lax.broadcasted_iota(jnp.int32, (LANES, LANES), 1)
    ut = jnp.where(ra <= ca, 1.0, 0.0).astype(BF16)
    carry_eq = jnp.zeros((N_EXPERTS, 1), F32)
    carry_sel = jnp.zeros((N_EXPERTS, 1), F32)
    for j in range(lp // LANES):
        sl = slice(LANES * j, LANES * (j + 1))
        bj = aff[:, sl]
        eqf = jnp.where(bj == tau, 1.0, 0.0)
        tie_rank = _dot(eqf.astype(BF16), ut) + carry_eq - eqf
        carry_eq = carry_eq + jnp.sum(eqf, axis=1, keepdims=True)
        self = jnp.where((bj > tau) | ((bj == tau) & (tie_rank < need)), 1.0, 0.0)
        csum_ref[j] = _dot(self.astype(BF16), ut) + carry_sel
        carry_sel = carry_sel + jnp.sum(self, axis=1, keepdims=True)
        sel_ref[:, sl] = self

    lane = lax.broadcasted_iota(jnp.int32, (1, LANES), 1)
    sblk = 64
    cbase = lax.broadcasted_iota(jnp.int32, (sblk, 1), 0)
    cols_ref[...] = jnp.zeros_like(cols_ref)

    def per_expert(e, carry):
        def per_block(sb, carry2):
            cvals = (sb * sblk + cbase).astype(F32)

            def per_tile(j, acc):
                return acc + jnp.where(csum_ref[j, pl.ds(e, 1), :] <= cvals, 1.0, 0.0)

            n_lane_tiles = lp // LANES
            acc = lax.fori_loop(0, n_lane_tiles, per_tile, jnp.zeros((sblk, LANES), F32),
                                unroll=6 if n_lane_tiles % 6 == 0 else 1)
            rows = pl.ds(pl.multiple_of(sb * sblk, sblk), sblk)
            cols_ref[rows, :] = jnp.where(lane == e, jnp.sum(acc, axis=1, keepdims=True), cols_ref[rows, :])
            return carry2

        return lax.fori_loop(0, slot_pad // sblk, per_block, carry)

    lax.fori_loop(0, N_EXPERTS, per_expert, 0)
    idx_ref[...] = jnp.transpose(cols_ref[...])[:N_EXPERTS, :].astype(jnp.int32)

    sel = sel_ref[...]
    tok = lax.broadcasted_iota(jnp.int32, (1, lp), 1)
    table = jnp.zeros((N_EXPERTS, LANES), F32)
    for i in range(n_tiles + 1):
        below = jnp.sum(jnp.where(tok < N_META + tile * i, sel, 0.0), axis=1, keepdims=True)
        table = jnp.where(lane == i, below, table)
    lo_ref[...] = table.astype(jnp.int32)


def _select(afft, cap, slot_pad, tile, n_tiles):
    lp = afft.shape[1]
    assert n_tiles < LANES
    return pl.pallas_call(
        functools.partial(_select_kernel, cap=cap, slot_pad=slot_pad, lp=lp, tile=tile, n_tiles=n_tiles),
        out_shape=[jax.ShapeDtypeStruct((N_EXPERTS, slot_pad), jnp.int32),
                   jax.ShapeDtypeStruct((N_EXPERTS, LANES), jnp.int32)],
        scratch_shapes=[pltpu.VMEM((N_EXPERTS, lp), F32), pltpu.VMEM((lp // LANES, N_EXPERTS, LANES), F32),
                        pltpu.VMEM((slot_pad, LANES), F32)],
        compiler_params=pltpu.CompilerParams(vmem_limit_bytes=VMEM_LIMIT),
        name="expert_select",
    )(afft)


def _slab_copy(src, s_tok, dst, d_tok, sem, *, nc):
    s0 = pl.multiple_of(s_tok * nc, 8)
    d0 = pl.multiple_of(d_tok * nc, 8)
    return pltpu.make_async_copy(src.at[pl.ds(s0, nc)], dst.at[pl.ds(d0, nc)], sem)


def _dispatch_kernel(idx_ref, xg_ref, x_ref, aff_ref, slab_ref, sem, *, cap, cap_pad, nc, nx):
    copy = functools.partial(_slab_copy, nc=nc)
    e = pl.program_id(0)
    h0 = -(-(cap_pad // 2) // 16) * 16
    halves = ((0, h0, min(cap, h0)), (h0, cap_pad - h0, max(cap - h0, 0)))

    def fetch(ex, hf):
        first, _, used = halves[hf]

        def issue(s, c):
            copy(xg_ref, idx_ref[ex, first + s], slab_ref.at[hf], s, sem.at[hf]).start()
            return c

        lax.fori_loop(0, used, issue, 0, unroll=DMA_ISSUE_UNROLL if used % DMA_ISSUE_UNROLL == 0 else 1)

    def finish(hf):
        first, size, used = halves[hf]

        def drain(_, c):
            copy(xg_ref, 0, slab_ref.at[hf], 0, sem.at[hf]).wait()
            return c

        lax.fori_loop(0, used, drain, 0)
        rows = slice(first, first + size)
        for c in range(nx):
            x_ref[rows, LANES * c:LANES * (c + 1)] = slab_ref[hf, pl.ds(c, size, stride=nc), :].astype(BF16)
        aff_ref[rows, :] = slab_ref[hf, pl.ds(nx, size, stride=nc), :]

    @pl.when(e == 0)
    def _():
        for hf, (_, size, used) in enumerate(halves):
            if used < size:
                slab_ref[hf, used * nc:size * nc, :] = jnp.zeros(((size - used) * nc, LANES), F32)
        fetch(0, 0)

    fetch(e, 1)
    finish(0)

    @pl.when(e + 1 < N_EXPERTS)
    def _():
        fetch(e + 1, 0)

    finish(1)


def _dispatch(xg, idx, cap, cap_pad, d_model):
    nx = d_model // LANES
    nc = nx + ROW_PAD
    h0 = -(-(cap_pad // 2) // 16) * 16
    return pl.pallas_call(
        functools.partial(_dispatch_kernel, cap=cap, cap_pad=cap_pad, nc=nc, nx=nx),
        grid_spec=pltpu.PrefetchScalarGridSpec(
            num_scalar_prefetch=1,
            grid=(N_EXPERTS,),
            in_specs=[pl.BlockSpec(memory_space=pl.ANY)],
            out_specs=[pl.BlockSpec((cap_pad, d_model), lambda e, *_: (e, 0)),
                       pl.BlockSpec((cap_pad, LANES), lambda e, *_: (e, 0))],
            scratch_shapes=[pltpu.VMEM((2, h0 * nc, LANES), F32), pltpu.SemaphoreType.DMA((2,))]),
        out_shape=[jax.ShapeDtypeStruct((N_EXPERTS * cap_pad, d_model), BF16),
                   jax.ShapeDtypeStruct((N_EXPERTS * cap_pad, LANES), F32)],
        compiler_params=_params(1),
        name="expert_dispatch",
    )(idx, xg)


def _ffn_a_kernel(x_ref, wg_ref, wu_ref, o_ref):
    x16 = x_ref[...]
    g = _dot(x16, wg_ref[0].astype(BF16))
    u = _dot(x16, wu_ref[0].astype(BF16))
    o_ref[...] = (g * _sigmoid(g) * u).astype(o_ref.dtype)


def _ffn_a(xe, w_gate, w_up, cap_pad, d_model):
    _, _, ff = w_gate.shape
    tf = _pick(ff, 256, LANES)
    return pl.pallas_call(
        _ffn_a_kernel,
        grid=(N_EXPERTS, ff // tf),
        in_specs=[pl.BlockSpec((cap_pad, d_model), lambda e, f: (e, 0)),
                  pl.BlockSpec((1, d_model, tf), lambda e, f: (e, 0, f)),
                  pl.BlockSpec((1, d_model, tf), lambda e, f: (e, 0, f))],
        out_specs=pl.BlockSpec((cap_pad, tf), lambda e, f: (e, f)),
        out_shape=jax.ShapeDtypeStruct((N_EXPERTS * cap_pad, ff), BF16),
        compiler_params=_params(2),
        name="expert_ffn_in",
    )(xe, w_gate, w_up)


def _ffn_b_kernel(h_ref, wd_ref, aff_ref, o_ref):
    e = pl.program_id(0)
    y = _dot(h_ref[...], wd_ref[0].astype(BF16))
    lane = lax.broadcasted_iota(jnp.int32, (1, LANES), 1)
    gate = jnp.sum(jnp.where(lane == e, aff_ref[...], 0.0), axis=-1, keepdims=True)
    y = y * gate
    for c in range(y.shape[1] // LANES):
        o_ref[:, c, :] = y[:, LANES * c:LANES * (c + 1)]


def _ffn_b(hid, w_down, aff, cap_pad, d_model):
    _, ff, _ = w_down.shape
    td = _pick(d_model, 1024, 8 * LANES) if d_model % (8 * LANES) == 0 else d_model
    return pl.pallas_call(
        _ffn_b_kernel,
        grid=(N_EXPERTS, d_model // td),
        in_specs=[pl.BlockSpec((cap_pad, ff), lambda e, n: (e, 0)),
                  pl.BlockSpec((1, ff, td), lambda e, n: (e, 0, n)),
                  pl.BlockSpec((cap_pad, LANES), lambda e, n: (e, 0))],
        out_specs=pl.BlockSpec((cap_pad, td // LANES, LANES), lambda e, n: (e, n, 0)),
        out_shape=jax.ShapeDtypeStruct((N_EXPERTS * cap_pad, d_model // LANES, LANES), F32),
        compiler_params=_params(2),
        name="expert_ffn_out",
    )(hid, w_down, aff)


def _combine_kernel(idx_ref, lo_ref, h_ref, ye_ref, nw_ref, o_ref, hbuf_ref, acc_ref, stage_ref, sem, hsem, *,
                    tt, nx, cap_pad):
    i = pl.program_id(0)
    t0 = N_META + i * tt
    ch = COMBINE_CHUNK
    reg = COMBINE_REGION
    pitch = nx + ROW_PAD
    slot = i % 2

    def residual_copy(step, buf):
        rows = pl.ds(pl.multiple_of(N_META + step * tt, 8), tt)
        return pltpu.make_async_copy(h_ref.at[rows], hbuf_ref.at[buf], hsem.at[buf])

    @pl.when(i == 0)
    def _():
        residual_copy(0, 0).start()

    @pl.when(i + 1 < pl.num_programs(0))
    def _():
        residual_copy(i + 1, 1 - slot).start()

    def round_chunks(e, r):
        left = lo_ref[e, i + 1] - lo_ref[e, i] - r * reg
        return (jnp.minimum(left, reg) + ch - 1) // ch

    def chunk_copy(e, r, j):
        src = pl.multiple_of((e * cap_pad + lo_ref[e, i] + r * reg + j * ch) * nx, 8)
        dst = pl.multiple_of(j * ch * nx, 8)
        return pltpu.make_async_copy(ye_ref.at[pl.ds(src, ch * nx)], stage_ref.at[e, pl.ds(dst, ch * nx)],
                                     sem.at[e])

    def fetch(e, r):
        def body(j, c):
            chunk_copy(e, r, j).start()
            return c

        lax.fori_loop(0, round_chunks(e, r), body, 0)

    def add_round(e, r):
        def drain(j, c):
            chunk_copy(e, r, j).wait()
            return c

        lax.fori_loop(0, round_chunks(e, r), drain, 0)
        first = lo_ref[e, i] + r * reg

        def per_slot(s, c):
            src = pl.ds(pl.multiple_of((s - first) * nx, 8), nx)
            dst = pl.ds(pl.multiple_of((idx_ref[e, s] - t0) * pitch, 8), nx)
            acc_ref[dst, :] = acc_ref[dst, :] + stage_ref[e, src, :]
            return c

        lax.fori_loop(first, jnp.minimum(first + reg, lo_ref[e, i + 1]), per_slot, 0)

    def prefetch(e, c):
        fetch(e, 0)
        return c

    lax.fori_loop(0, N_EXPERTS, prefetch, 0)
    residual_copy(i, slot).wait()
    for c in range(nx):
        acc_ref[pl.ds(c, tt, stride=pitch), :] = hbuf_ref[slot, :, LANES * c:LANES * (c + 1)]

    def per_expert(e, carry):
        add_round(e, 0)

        def later(r, c):
            fetch(e, r)
            add_round(e, r)
            return c

        n_rounds = (lo_ref[e, i + 1] - lo_ref[e, i] + reg - 1) // reg
        lax.fori_loop(1, n_rounds, later, 0)
        return carry

    lax.fori_loop(0, N_EXPERTS, per_expert, 0)
    for c in range(nx):
        hbuf_ref[slot, :, LANES * c:LANES * (c + 1)] = acc_ref[pl.ds(c, tt, stride=pitch), :]
    x = hbuf_ref[slot]
    o_ref[...] = x * lax.rsqrt(jnp.mean(x * x, axis=-1, keepdims=True) + NORM_EPS) * nw_ref[...]


def _combine(idx, lo, h2, ye, norm_f_w, n_real, tt, cap_pad):
    d = h2.shape[1]
    nx = d // LANES
    assert COMBINE_REGION % COMBINE_CHUNK == 0
    return pl.pallas_call(
        functools.partial(_combine_kernel, tt=tt, nx=nx, cap_pad=cap_pad),
        grid_spec=pltpu.PrefetchScalarGridSpec(
            num_scalar_prefetch=2,
            grid=(n_real // tt,),
            in_specs=[pl.BlockSpec(memory_space=pl.ANY),
                      pl.BlockSpec(memory_space=pl.ANY),
                      pl.BlockSpec((1, d), lambda i, *_: (0, 0))],
            out_specs=pl.BlockSpec((tt, d), lambda i, *_: (i, 0)),
            scratch_shapes=[pltpu.VMEM((2, tt, d), F32), pltpu.VMEM((tt * (nx + ROW_PAD), LANES), F32),
                            pltpu.VMEM((N_EXPERTS, COMBINE_REGION * nx, LANES), F32),
                            pltpu.SemaphoreType.DMA((N_EXPERTS,)), pltpu.SemaphoreType.DMA((2,))]),
        out_shape=jax.ShapeDtypeStruct((n_real, d), F32),
        compiler_params=_params(1),
        name="expert_combine",
    )(idx, lo, h2, ye, norm_f_w.reshape(1, d).astype(F32))


def _layer(hp, seq_len, norm1_w, w_in, conv_w, a_log_fwd, a_log_bwd, dt_bias_fwd, dt_bias_bwd, out_norm_w,
           w_branch_a, attn_sink, w_branch_b, w_out, norm2_w, w_router, w_gate, w_up, w_down):
    lp, d = hp.shape
    main_w = 2 * DN_QK + 2 * DN_VW
    gate_w = 4 * DN_HEADS
    rest_w = SWA_QW + 2 * SWA_KVW + 2 * d
    n = _rmsnorm(hp, norm1_w, BF16)
    w_t = jnp.swapaxes(w_in, 0, 1)
    proj_main = _matmul(n, w_t, main_w, 0, BF16, w_rows=True, name="in_proj_main")
    gates = _matmul(n, w_t, LANES, main_w, F32, w_rows=True, name="in_proj_gates")
    rest = _matmul(n, w_t, rest_w, main_w + gate_w, BF16, w_rows=True, name="in_proj_rest")

    prm = jnp.zeros((8, LANES), F32)
    prm = prm.at[0, 2 * DN_HEADS:4 * DN_HEADS].set(jnp.concatenate([a_log_fwd, a_log_bwd]).astype(F32))
    prm = prm.at[1, 2 * DN_HEADS:4 * DN_HEADS].set(jnp.concatenate([dt_bias_fwd, dt_bias_bwd]).astype(F32))
    qkv = _dn_prep(proj_main, conv_w.astype(F32), seq_len)
    o_f = _deltanet(qkv, gates, prm, seq_len, backward=False)
    o_r = _deltanet(qkv, gates, prm, seq_len, backward=True)
    o_a = _gated_norm(o_f, o_r, proj_main, out_norm_w)
    o_b = _window_attention(rest, attn_sink, seq_len)

    mixed = _branch_merge(o_a, o_b, w_branch_a, w_branch_b, rest, d)
    h2 = _matmul(mixed, w_out, d, 0, F32, res=hp, name="out_proj")

    cap = EC_CAPACITY * seq_len // N_EXPERTS
    cap_pad = -(-(cap + COMBINE_CHUNK) // 16) * 16
    xg, afft = _router(h2, norm2_w, w_router, seq_len)
    n_real = seq_len - N_META
    tt = _pick(n_real, TOKEN_TILE, LANES)
    slot_pad = -(-cap_pad // LANES) * LANES
    idx, lo = _select(afft, cap, slot_pad, tt, n_real // tt)
    xe, aff = _dispatch(xg.reshape(-1, LANES), idx, cap, cap_pad, d)
    hid = _ffn_a(xe, w_gate, w_up, cap_pad, d)
    ye = _ffn_b(hid, w_down, aff, cap_pad, d)
    return functools.partial(_combine, idx, lo, h2, ye.reshape(-1, LANES), n_real=n_real, tt=tt, cap_pad=cap_pad)


def kernel(x, meta_tokens, norm1_w, w_in, conv_w, a_log_fwd, a_log_bwd, dt_bias_fwd, dt_bias_bwd, out_norm_w,
           w_branch_a, attn_sink, w_branch_b, w_out, norm2_w, w_router, w_gate, w_up, w_down, norm_f_w):
    batch, seq, d = x.shape
    depth = norm1_w.shape[0]
    assert depth == 1, "the final norm is fused into the last layer's expert combine"
    seq_len = N_META + seq
    lp = -(-seq_len // TOKEN_TILE) * TOKEN_TILE
    outs = []
    for b in range(batch):
        h = jnp.concatenate([meta_tokens.astype(x.dtype), x[b]], axis=0)
        hp = jnp.pad(h, ((0, lp - seq_len), (0, 0)))
        combine = _layer(hp, seq_len, norm1_w[0], w_in[0], conv_w[0], a_log_fwd[0], a_log_bwd[0],
                         dt_bias_fwd[0], dt_bias_bwd[0], out_norm_w[0], w_branch_a[0], attn_sink[0],
                         w_branch_b[0], w_out[0], norm2_w[0], w_router[0], w_gate[0], w_up[0], w_down[0])
        outs.append(combine(norm_f_w=norm_f_w))
    return jnp.stack(outs, axis=0)
```
